```python
import jax, jax.numpy as jnp
from jax import lax
import numpy as np

D_MODEL = 1024
BATCH = 4
SEQ = 8192
DEPTH = 1

MOBA_HEADS = 8
HEAD_DIM = 64
MOBA_WIDTH = MOBA_HEADS * HEAD_DIM
MOBA_BLOCK = 256
MOBA_TOPK = 3
MOBA_Q_CHUNK = 32
ROPE_THETA = 500000.0
ROPE_DIM = HEAD_DIM // 4
RWKV_HEADS = 8
RWKV_WIDTH = RWKV_HEADS * HEAD_DIM
RWKV_DECAY_LORA = 64
RWKV_AAA_LORA = 64
RWKV_GATE_LORA = 128
RWKV_SHIFT_WIDTH = 3 * RWKV_WIDTH + RWKV_DECAY_LORA + RWKV_AAA_LORA + RWKV_GATE_LORA
GN_EPS = 64e-5
IN_SIZES = (MOBA_WIDTH, MOBA_WIDTH, MOBA_WIDTH, RWKV_SHIFT_WIDTH, 2 * D_MODEL)
IN_WIDTH = sum(IN_SIZES)
PEER_HEADS = 8
PEER_NKEYS = 128
PEER_EXPERTS = PEER_NKEYS * PEER_NKEYS
PEER_QDIM = 256
PEER_HALF = PEER_QDIM // 2
PEER_TOPK = 16
PEER_CHUNK = 128
RMS_EPS = 1e-6
NEG_INF = -1e30

kernel_name = 'hybrid_moba_rwkv7_peer_adaln'


def rms_norm(x, g, eps=RMS_EPS):
    x32 = x.astype(jnp.float32)
    y = x32 * lax.rsqrt(jnp.mean(x32 * x32, axis=-1, keepdims=True) + eps)
    return (y * g.astype(jnp.float32)).astype(x.dtype)


def rope_partial(x, pos):
    half = ROPE_DIM // 2
    inv = ROPE_THETA ** (-(jnp.arange(half, dtype=jnp.float32) * 2.0) / ROPE_DIM)
    ang = pos.astype(jnp.float32)[..., None] * inv
    cos = jnp.cos(ang)[:, :, None, :]
    sin = jnp.sin(ang)[:, :, None, :]
    x32 = x.astype(jnp.float32)
    x1 = x32[..., :half]
    x2 = x32[..., half:ROPE_DIM]
    out = jnp.concatenate([x1 * cos - x2 * sin, x2 * cos + x1 * sin, x32[..., ROPE_DIM:]], axis=-1)
    return out.astype(x.dtype)


def moba_attention(q, k, v):
    B, S, H, Dh = q.shape
    nb = -(-S // MOBA_BLOCK)
    sp = nb * MOBA_BLOCK
    pad = ((0, 0), (0, 0), (0, sp - S), (0, 0))
    q, k, v = (jnp.pad(t.transpose(0, 2, 1, 3), pad) for t in (q, k, v))
    kb = k.reshape(B, H, nb, MOBA_BLOCK, Dh)
    vb = v.reshape(B, H, nb, MOBA_BLOCK, Dh)
    scale = Dh ** -0.5
    qblk = jnp.arange(sp) // MOBA_BLOCK
    n_sel = min(MOBA_TOPK, nb - 1)
    if n_sel > 0:
        kmean = jnp.mean(kb.astype(jnp.float32), axis=3)
        gscore = jnp.einsum('bhsd,bhnd->bhsn', q.astype(jnp.float32), kmean)
        past = jnp.arange(nb)[None, :] < qblk[:, None]
        gscore = jnp.where(past, gscore, NEG_INF)
        _, sel = lax.top_k(gscore, n_sel)
        sel_ok = jnp.arange(n_sel)[None, :] < qblk[:, None]
    bi = jnp.arange(B)[:, None, None, None]
    hi = jnp.arange(H)[None, :, None, None]

    def attend_chunk(i):
        s0 = i * MOBA_Q_CHUNK
        qc = lax.dynamic_slice_in_dim(q, s0, MOBA_Q_CHUNK, axis=2)
        qpos = s0 + jnp.arange(MOBA_Q_CHUNK)
        own = s0 // MOBA_BLOCK
        k_own = lax.dynamic_index_in_dim(kb, own, axis=2, keepdims=False)
        v_own = lax.dynamic_index_in_dim(vb, own, axis=2, keepdims=False)
        kpos = own * MOBA_BLOCK + jnp.arange(MOBA_BLOCK)
        s_own = jnp.einsum('bhcd,bhkd->bhck', qc, k_own).astype(jnp.float32) * scale
        s_own = jnp.where(kpos[None, :] <= qpos[:, None], s_own, NEG_INF)
        if n_sel == 0:
            p = jax.nn.softmax(s_own, axis=-1).astype(v.dtype)
            return jnp.einsum('bhck,bhkd->bhcd', p, v_own)
        selc = lax.dynamic_slice_in_dim(sel, s0, MOBA_Q_CHUNK, axis=2)
        ok = lax.dynamic_slice_in_dim(sel_ok, s0, MOBA_Q_CHUNK, axis=0)
        k_sel = kb[bi, hi, selc]
        v_sel = vb[bi, hi, selc]
        s_sel = jnp.einsum('bhcd,bhcnkd->bhcnk', qc, k_sel).astype(jnp.float32) * scale
        s_sel = jnp.where(ok[None, None, :, :, None], s_sel, NEG_INF)
        s_all = jnp.concatenate([s_sel.reshape(B, H, MOBA_Q_CHUNK, n_sel * MOBA_BLOCK), s_own], axis=-1)
        p = jax.nn.softmax(s_all, axis=-1).astype(v.dtype)
        p_sel = p[..., :n_sel * MOBA_BLOCK].reshape(B, H, MOBA_Q_CHUNK, n_sel, MOBA_BLOCK)
        p_own = p[..., n_sel * MOBA_BLOCK:]
        return (jnp.einsum('bhcnk,bhcnkd->bhcd', p_sel, v_sel)
                + jnp.einsum('bhck,bhkd->bhcd', p_own, v_own))

    out = lax.map(attend_chunk, jnp.arange(sp // MOBA_Q_CHUNK))
    out = out.transpose(1, 0, 3, 2, 4).reshape(B, sp, H, Dh)
    return out[:, :S]


def token_shift(z, mu):
    prev = jnp.pad(z, ((0, 0), (1, 0), (0, 0)))[:, :-1]
    return z + (prev - z) * mu


def rwkv7_time_mix(zr, zk, zv, zw, za, zg, w0, w2, a0, a2, g2, k_k, k_a, r_k, ln_g, ln_b):
    B, S, _ = zr.shape
    H, N = RWKV_HEADS, HEAD_DIM
    f32 = jnp.float32
    w = -jax.nn.softplus(-(w0 + jnp.tanh(zw) @ w2)) - 0.5
    decay = jnp.exp(-jnp.exp(w.astype(f32)))
    a = jax.nn.sigmoid(a0 + za @ a2)
    g = jax.nn.sigmoid(zg) @ g2
    heads = lambda t: t.reshape(B, S, H, N).astype(f32)
    kk = heads(zk * k_k)
    kk = kk / jnp.maximum(jnp.sqrt(jnp.sum(kk * kk, axis=-1, keepdims=True)), 1e-12)
    k = zk * (1.0 + (a - 1.0) * k_a)
    r_h, k_h, v_h, a_h, d_h = heads(zr), heads(k), heads(zv), heads(a), heads(decay)

    def step(state, inp):
        r_t, d_t, k_t, v_t, kk_t, a_t = inp
        sa = jnp.einsum('bhij,bhj->bhi', state, -kk_t)
        state = (state * d_t[:, :, None, :] + sa[..., None] * (kk_t * a_t)[:, :, None, :]
                 + v_t[..., None] * k_t[:, :, None, :])
        return state, jnp.einsum('bhij,bhj->bhi', state, r_t)

    xs = tuple(t.transpose(1, 0, 2, 3) for t in (r_h, d_h, k_h, v_h, kk, a_h))
    _, y = lax.scan(step, jnp.zeros((B, H, N, N), f32), xs)
    y = y.transpose(1, 0, 2, 3)
    mean = jnp.mean(y, axis=-1, keepdims=True)
    var = jnp.mean(jnp.square(y - mean), axis=-1, keepdims=True)
    y = ((y - mean) * lax.rsqrt(var + GN_EPS)).reshape(B, S, H * N)
    y = y * ln_g.astype(f32) + ln_b.astype(f32)
    bonus = jnp.sum(r_h * k_h * r_k.astype(f32), axis=-1, keepdims=True) * v_h
    y = y + bonus.reshape(B, S, H * N)
    return (y * g.astype(f32)).astype(zr.dtype)


def peer_ffn(xn, wq, subkeys, u_tab, v_tab):
    B, S, D = xn.shape
    T = B * S
    xt = xn.reshape(T, D)
    q = (xt @ wq).reshape(T, PEER_HEADS, 2, PEER_HALF)
    s = jnp.einsum('thpd,hpnd->thpn', q, subkeys).astype(jnp.float32)
    top_s, top_i = lax.top_k(s, PEER_TOPK)
    cand_s = (top_s[:, :, 0, :, None] + top_s[:, :, 1, None, :]).reshape(T, PEER_HEADS, PEER_TOPK * PEER_TOPK)
    cand_i = (top_i[:, :, 0, :, None] * PEER_NKEYS + top_i[:, :, 1, None, :]).reshape(T, PEER_HEADS, PEER_TOPK * PEER_TOPK)
    best_s, best_j = lax.top_k(cand_s, PEER_TOPK)
    idx = jnp.take_along_axis(cand_i, best_j, axis=-1)
    gates = jax.nn.softmax(best_s, axis=-1)
    nc = T // PEER_CHUNK

    def retrieve(args):
        xc, ic, gc = args
        uc = u_tab[ic]
        vc = v_tab[ic]
        hdn = jax.nn.gelu(jnp.einsum('cd,chkd->chk', xc, uc).astype(jnp.float32), approximate=False) * gc
        return jnp.einsum('chk,chkd->cd', hdn.astype(vc.dtype), vc)

    out = lax.map(retrieve, (xt.reshape(nc, PEER_CHUNK, D),
                             idx.reshape(nc, PEER_CHUNK, PEER_HEADS, PEER_TOPK),
                             gates.reshape(nc, PEER_CHUNK, PEER_HEADS, PEER_TOPK)))
    return out.reshape(B, S, D).astype(xn.dtype)


def setup_inputs(seed: int = 0) -> dict:
    key = jax.random.key(seed)
    ks = iter(jax.random.split(key, 40))
    L = DEPTH
    D = D_MODEL
    nrm = lambda shape, sc: jax.random.normal(next(ks), shape, jnp.float32) * sc
    unif = lambda shape, lo, hi: jax.random.uniform(next(ks), shape, jnp.float32, lo, hi)
    x = nrm((BATCH, SEQ, D), 1.0)
    c = nrm((BATCH, D), 1.0)
    positions = jnp.tile(jnp.arange(SEQ, dtype=jnp.int32)[None, :], (BATCH, 1))
    return {
        'x': x, 'c': c, 'positions': positions,
        'w_ada': nrm((L, D, 6 * D), 0.5 * D ** -0.5),
        'b_ada': nrm((L, 6 * D), 0.02),
        'norm1_g': 1.0 + nrm((L, D), 0.05),
        'w_in': nrm((L, D, IN_WIDTH), D ** -0.5),
        'q_norm_g': 1.0 + nrm((L, HEAD_DIM), 0.05),
        'k_norm_g': 1.0 + nrm((L, HEAD_DIM), 0.05),
        'rwkv_mu': unif((L, RWKV_SHIFT_WIDTH), 0.0, 1.0),
        'rwkv_w0': nrm((L, RWKV_WIDTH), 1.0) - 2.0,
        'rwkv_w2': nrm((L, RWKV_DECAY_LORA, RWKV_WIDTH), 0.5 * RWKV_DECAY_LORA ** -0.5),
        'rwkv_a0': nrm((L, RWKV_WIDTH), 0.5),
        'rwkv_a2': nrm((L, RWKV_AAA_LORA, RWKV_WIDTH), 0.5 * RWKV_AAA_LORA ** -0.5),
        'rwkv_g2': nrm((L, RWKV_GATE_LORA, RWKV_WIDTH), RWKV_GATE_LORA ** -0.5),
        'rwkv_k_k': 0.85 + nrm((L, RWKV_WIDTH), 0.1),
        'rwkv_k_a': 1.0 + nrm((L, RWKV_WIDTH), 0.1),
        'rwkv_r_k': nrm((L, RWKV_HEADS, HEAD_DIM), 0.1),
        'rwkv_ln_g': 1.0 + nrm((L, RWKV_WIDTH), 0.05),
        'rwkv_ln_b': nrm((L, RWKV_WIDTH), 0.02),
        'w_proj_moba': nrm((L, MOBA_WIDTH, D), MOBA_WIDTH ** -0.5),
        'w_proj_rwkv': nrm((L, RWKV_WIDTH, D), RWKV_WIDTH ** -0.5),
        'w_out': nrm((L, D, D), D ** -0.5),
        'norm2_g': 1.0 + nrm((L, D), 0.05),
        'peer_wq': nrm((L, D, PEER_HEADS * PEER_QDIM), D ** -0.5),
        'peer_subkeys': nrm((L, PEER_HEADS, 2, PEER_NKEYS, PEER_HALF), PEER_HALF ** -0.5),
        'peer_u': nrm((L, PEER_EXPERTS, D), D ** -0.5),
        'peer_v': nrm((L, PEER_EXPERTS, D), PEER_HEADS ** -0.5),
    }


def reference(x, c, positions, w_ada, b_ada, norm1_g, w_in, q_norm_g, k_norm_g, rwkv_mu,
              rwkv_w0, rwkv_w2, rwkv_a0, rwkv_a2, rwkv_g2, rwkv_k_k, rwkv_k_a, rwkv_r_k,
              rwkv_ln_g, rwkv_ln_b, w_proj_moba, w_proj_rwkv, w_out, norm2_g, peer_wq,
              peer_subkeys, peer_u, peer_v):
    B, S, D = x.shape
    split_at = [int(v) for v in np.cumsum(IN_SIZES)[:-1]]
    rwkv_split = [int(v) for v in np.cumsum((RWKV_WIDTH, RWKV_WIDTH, RWKV_WIDTH, RWKV_DECAY_LORA, RWKV_AAA_LORA))]
    h = x
    for l in range(DEPTH):
        mod = (jax.nn.silu(c) @ w_ada[l] + b_ada[l])[:, None, :]
        sh1, sc1, gt1, sh2, sc2, gt2 = jnp.split(mod, 6, axis=-1)
        xn = rms_norm(h, norm1_g[l]) * (1.0 + sc1) + sh1
        proj = xn @ w_in[l]
        qm, km, vm, zrw, gate_logits = jnp.split(proj, split_at, axis=-1)
        qm = rope_partial(rms_norm(qm.reshape(B, S, MOBA_HEADS, HEAD_DIM), q_norm_g[l]), positions)
        km = rope_partial(rms_norm(km.reshape(B, S, MOBA_HEADS, HEAD_DIM), k_norm_g[l]), positions)
        vm = vm.reshape(B, S, MOBA_HEADS, HEAD_DIM)
        o_moba = moba_attention(qm, km, vm).reshape(B, S, MOBA_WIDTH)
        zrw = token_shift(zrw, rwkv_mu[l])
        zr, zk, zv, zw, za, zg = jnp.split(zrw, rwkv_split, axis=-1)
        o_rwkv = rwkv7_time_mix(zr, zk, zv, zw, za, zg, rwkv_w0[l], rwkv_w2[l], rwkv_a0[l],
                                rwkv_a2[l], rwkv_g2[l], rwkv_k_k[l], rwkv_k_a[l], rwkv_r_k[l],
                                rwkv_ln_g[l], rwkv_ln_b[l])
        g_moba, g_rwkv = jnp.split(jax.nn.sigmoid(gate_logits), 2, axis=-1)
        mixed = g_moba * (o_moba @ w_proj_moba[l]) + g_rwkv * (o_rwkv @ w_proj_rwkv[l])
        h = h + gt1 * (mixed @ w_out[l])
        xn2 = rms_norm(h, norm2_g[l]) * (1.0 + sc2) + sh2
        h = h + gt2 * peer_ffn(xn2, peer_wq[l], peer_subkeys[l], peer_u[l], peer_v[l])
    return h
```

```python
import functools

import numpy as np
import jax
import jax.numpy as jnp
from jax import lax
from jax.experimental import pallas as pl
from jax.experimental.pallas import tpu as pltpu

F32 = jnp.float32
BF16 = jnp.bfloat16

HEAD_DIM = 64
N_HEADS = 8
WIDTH = N_HEADS * HEAD_DIM
MOBA_BLOCK = 256
MOBA_TOPK = 3
ROPE_THETA = 500000.0
ROPE_DIM = HEAD_DIM // 4
LORA_W = 64
LORA_A = 64
LORA_G = 128
SHIFT_WIDTH = 3 * WIDTH + LORA_W + LORA_A + LORA_G
GN_EPS = 64e-5
RMS_EPS = 1e-6
NEG_INF = -1e30
PEER_HEADS = 8
PEER_NKEYS = 128
PEER_HALF = 128
PEER_TOPK = 16

TOKEN_TILE = 256
V7X_VMEM_LIMIT = 48 * 1024 * 1024


def _cparams(*sem):
    return pltpu.CompilerParams(dimension_semantics=sem, vmem_limit_bytes=V7X_VMEM_LIMIT)


def _dot(a, b):
    return jnp.dot(a.astype(BF16), b.astype(BF16), preferred_element_type=F32)


def _dot_t(a, b):
    return lax.dot_general(a.astype(BF16), b.astype(BF16), (((1,), (1,)), ((), ())),
                           preferred_element_type=F32)


def _split(a):
    hi = a.astype(BF16)
    lo = (a - hi.astype(F32)).astype(BF16)
    return hi, lo


def _dot3(a, b):
    ah, al = _split(a)
    bh, bl = _split(b)
    return (jnp.dot(ah, bh, preferred_element_type=F32) + jnp.dot(ah, bl, preferred_element_type=F32)
            + jnp.dot(al, bh, preferred_element_type=F32))


def _dot2(a, b_exact):
    ah, al = _split(a)
    return jnp.dot(ah, b_exact, preferred_element_type=F32) + jnp.dot(al, b_exact, preferred_element_type=F32)


def _head_ones(n):
    r = lax.broadcasted_iota(jnp.int32, (n, n), 0) >> 6
    c = lax.broadcasted_iota(jnp.int32, (n, n), 1) >> 6
    return (r == c).astype(BF16)


def _sigmoid(x):
    return 1.0 / (1.0 + jnp.exp(-x))


def _ada_kernel(c_ref, w_ref, b_ref, o_ref):
    c = c_ref[...]
    o_ref[...] = _dot3(c * _sigmoid(c), w_ref[...]) + b_ref[...]


def _adaln(c, w_ada, b_ada):
    bsz, d = c.shape
    n = w_ada.shape[1]
    nb = 1024
    return pl.pallas_call(
        _ada_kernel,
        grid=(n // nb,),
        in_specs=[pl.BlockSpec((bsz, d), lambda j: (0, 0)),
                  pl.BlockSpec((d, nb), lambda j: (0, j)),
                  pl.BlockSpec((1, nb), lambda j: (0, j))],
        out_specs=pl.BlockSpec((bsz, nb), lambda j: (0, j)),
        out_shape=jax.ShapeDtypeStruct((bsz, n), F32),
        compiler_params=_cparams("arbitrary"),
        name="adaln_mod",
    )(c, w_ada, b_ada.reshape(1, n))


def _modnorm(x, g, mod, d, which):
    ms = jnp.mean(x * x, axis=-1, keepdims=True)
    y = x * lax.rsqrt(ms + RMS_EPS) * g
    sh = mod[:, (3 * which) * d:(3 * which + 1) * d]
    sc = mod[:, (3 * which + 1) * d:(3 * which + 2) * d]
    return y * (1.0 + sc) + sh


def _inproj_kernel(x_ref, mod_ref, g_ref, w_ref, o_ref):
    d = x_ref.shape[1]
    xn = _modnorm(x_ref[...], g_ref[...], mod_ref[0], d, 0)
    o_ref[...] = jnp.dot(xn.astype(BF16), w_ref[...], preferred_element_type=F32)


def _inproj(x2, mod3, g, w_bf16, tps):
    t, d = x2.shape
    n = w_bf16.shape[1]
    tm = TOKEN_TILE
    return pl.pallas_call(
        _inproj_kernel,
        grid=(t // tm,),
        in_specs=[pl.BlockSpec((tm, d), lambda i: (i, 0)),
                  pl.BlockSpec((1, 1, mod3.shape[2]), lambda i: (i // tps, 0, 0)),
                  pl.BlockSpec((1, d), lambda i: (0, 0)),
                  pl.BlockSpec((d, n), lambda i: (0, 0))],
        out_specs=pl.BlockSpec((tm, n), lambda i: (i, 0)),
        out_shape=jax.ShapeDtypeStruct((t, n), F32),
        compiler_params=_cparams("parallel"),
        name="in_proj",
    )(x2, mod3, g, w_bf16)


def _qkprep_kernel(qkv_ref, pos_ref, gq_ref, gk_ref, inv_ref, qo_ref, ko_ref, vo_ref, km_ref):
    w = WIDTH
    ones = _head_ones(w)
    ang = pos_ref[...].astype(F32) * inv_ref[...]
    cos = jnp.tile(jnp.cos(ang), (1, w // 128))
    sin = jnp.tile(jnp.sin(ang), (1, w // 128))
    dd = lax.broadcasted_iota(jnp.int32, (1, w), 1) & (HEAD_DIM - 1)
    half = ROPE_DIM // 2
    s_lo = jnp.where(dd < half, -1.0, 0.0) * sin
    s_hi = jnp.where((dd >= half) & (dd < ROPE_DIM), 1.0, 0.0) * sin

    def norm_rope(xh, g):
        ssq = _dot2(xh * xh, ones)
        y = xh * lax.rsqrt(ssq * (1.0 / HEAD_DIM) + RMS_EPS) * g
        return y * cos + pltpu.roll(y, w - half, 1) * s_lo + pltpu.roll(y, half, 1) * s_hi

    q = norm_rope(qkv_ref[:, 0:w], gq_ref[...])
    k = norm_rope(qkv_ref[:, w:2 * w], gk_ref[...])
    qo_ref[...] = (q * (HEAD_DIM ** -0.5)).astype(BF16)
    ko_ref[...] = k.astype(BF16)
    vo_ref[...] = qkv_ref[:, 2 * w:3 * w].astype(BF16)
    km_ref[0] = jnp.mean(k, axis=0, keepdims=True)


def _qkprep(qkv, pos2, gq, gk, inv128):
    t = qkv.shape[0]
    tm = MOBA_BLOCK
    w = WIDTH
    row = lambda i: (i, 0)
    fixed = lambda i: (0, 0)
    return pl.pallas_call(
        _qkprep_kernel,
        grid=(t // tm,),
        in_specs=[pl.BlockSpec((tm, 3 * w), row), pl.BlockSpec((tm, 1), row),
                  pl.BlockSpec((1, w), fixed), pl.BlockSpec((1, w), fixed), pl.BlockSpec((1, 128), fixed)],
        out_specs=[pl.BlockSpec((tm, w), row), pl.BlockSpec((tm, w), row), pl.BlockSpec((tm, w), row),
                   pl.BlockSpec((1, 1, w), lambda i: (i, 0, 0))],
        out_shape=[jax.ShapeDtypeStruct((t, w), BF16), jax.ShapeDtypeStruct((t, w), BF16),
                   jax.ShapeDtypeStruct((t, w), BF16), jax.ShapeDtypeStruct((t // tm, 1, w), F32)],
        compiler_params=_cparams("parallel"),
        name="moba_qk_prep",
    )(qkv, pos2, gq, gk, inv128)


def _moba_kernel(q_ref, k_ref, v_ref, km_ref, o_ref):
    blk = MOBA_BLOCK
    i = pl.program_id(2)
    q = q_ref[0]
    km = km_ref[0]
    nb = km.shape[0]
    lane = lax.broadcasted_iota(jnp.int32, (blk, 128), 1)
    col = lax.broadcasted_iota(jnp.int32, (blk, nb), 1)
    qpos = lax.broadcasted_iota(jnp.int32, (blk, blk), 0)
    kpos = lax.broadcasted_iota(jnp.int32, (blk, blk), 1)
    row0 = pl.multiple_of(i * blk, blk)
    k_own = k_ref[0, pl.ds(row0, blk), :]
    v_own = v_ref[0, pl.ds(row0, blk), :]

    qs, biases, carry0 = [], [], []
    for hh in range(2):
        qh = jnp.where((lane >> 6) == hh, q, jnp.zeros_like(q))
        g = jnp.where(col < i, _dot_t(qh, km), -jnp.inf)
        sel = jnp.zeros((blk, nb), jnp.bool_)
        for _ in range(MOBA_TOPK):
            m = jnp.max(g, axis=-1, keepdims=True)
            idx = jnp.min(jnp.where(g == m, col, nb), axis=-1, keepdims=True)
            pick = (col == idx) & (g > -jnp.inf)
            sel = sel | pick
            g = jnp.where(pick, -jnp.inf, g)
        biases.append(jnp.where(sel, 0.0, NEG_INF))
        s = jnp.where(kpos <= qpos, _dot_t(qh, k_own), NEG_INF)
        m0 = jnp.max(s, axis=-1, keepdims=True)
        p = jnp.exp(s - m0)
        carry0 += [m0, jnp.sum(p, axis=-1, keepdims=True), _dot(p, v_own)]
        qs.append(qh)

    def body(j, carry):
        r0 = pl.multiple_of(j * blk, blk)
        kj = k_ref[0, pl.ds(r0, blk), :]
        vj = v_ref[0, pl.ds(r0, blk), :]
        out = []
        for hh in range(2):
            m, l, acc = carry[3 * hh:3 * hh + 3]
            bj = jnp.sum(jnp.where(col == j, biases[hh], 0.0), axis=-1, keepdims=True)
            s = _dot_t(qs[hh], kj) + bj
            m_new = jnp.maximum(m, jnp.max(s, axis=-1, keepdims=True))
            alpha = jnp.exp(m - m_new)
            p = jnp.exp(s - m_new)
            out += [m_new, alpha * l + jnp.sum(p, axis=-1, keepdims=True), alpha * acc + _dot(p, vj)]
        return tuple(out)

    fin = lax.fori_loop(0, i, body, tuple(carry0))
    o0 = fin[2] / fin[1]
    o1 = fin[5] / fin[4]
    o_ref[0] = jnp.where(lane < HEAD_DIM, o0, o1).astype(o_ref.dtype)


def _moba(qb, kb, vb, km, bsz, seq):
    w = WIDTH
    nb = seq // MOBA_BLOCK
    q3 = qb.reshape(bsz, seq, w)
    k3 = kb.reshape(bsz, seq, w)
    v3 = vb.reshape(bsz, seq, w)
    km3 = km.reshape(bsz, nb, w)
    return pl.pallas_call(
        _moba_kernel,
        grid=(bsz, w // 128, nb),
        in_specs=[pl.BlockSpec((1, MOBA_BLOCK, 128), lambda b, p, i: (b, i, p)),
                  pl.BlockSpec((1, seq, 128), lambda b, p, i: (b, 0, p)),
                  pl.BlockSpec((1, seq, 128), lambda b, p, i: (b, 0, p)),
                  pl.BlockSpec((1, nb, 128), lambda b, p, i: (b, 0, p))],
        out_specs=pl.BlockSpec((1, MOBA_BLOCK, 128), lambda b, p, i: (b, i, p)),
        out_shape=jax.ShapeDtypeStruct((bsz, seq, w), BF16),
        compiler_params=_cparams("parallel", "parallel", "arbitrary"),
        name="moba_attention",
    )(q3, k3, v3, km3)


def _rwkvprep_kernel(z_ref, zp_ref, mu_ref, w0_ref, w2_ref, a0_ref, a2_ref, g2_ref, kk_ref, ka_ref, rk_ref,
                     r_o, lw_o, k_o, v_o, kn_o, b_o, bonus_o, g_o, *, tps):
    i = pl.program_id(0)
    w = WIDTH
    z = z_ref[...]
    tm = z.shape[0]
    prev_last = jnp.where((i % tps) == 0, 0.0, zp_ref[7:8, :])
    rows = lax.broadcasted_iota(jnp.int32, (tm, 1), 0)
    prev = jnp.where(rows == 0, prev_last, pltpu.roll(z, 1, 0))
    zs = z + (prev - z) * mu_ref[...]
    zr, zk, zv = zs[:, 0:w], zs[:, w:2 * w], zs[:, 2 * w:3 * w]
    zwa = zs[:, 3 * w:3 * w + 128]
    zg = zs[:, 3 * w + 128:3 * w + 256]
    nw = -(w0_ref[...] + _dot3(jnp.tanh(zwa), w2_ref[...]))
    softplus = jnp.maximum(nw, 0.0) + jnp.log(1.0 + jnp.exp(-jnp.abs(nw)))
    lw_o[...] = -jnp.exp(-softplus - 0.5)
    a = _sigmoid(a0_ref[...] + _dot3(zwa, a2_ref[...]))
    g_o[...] = _dot3(_sigmoid(zg), g2_ref[...])
    ones = _head_ones(w)
    kn = zk * kk_ref[...]
    kn = kn / jnp.maximum(jnp.sqrt(_dot2(kn * kn, ones)), 1e-12)
    k = zk * (1.0 + (a - 1.0) * ka_ref[...])
    r_o[...] = zr
    k_o[...] = k
    v_o[...] = zv
    kn_o[...] = kn
    b_o[...] = kn * a
    bonus_o[...] = _dot2(zr * k * rk_ref[...], ones) * zv


def _rwkvprep(zrw, tps, mu, w0, w2aug, a0, a2aug, g2, k_k, k_a, r_k):
    t = zrw.shape[0]
    tm = TOKEN_TILE
    w = WIDTH
    row = lambda i: (i, 0)
    fixed = lambda i: (0, 0)
    vec = pl.BlockSpec((1, w), fixed)
    outs = [jax.ShapeDtypeStruct((t, w), F32)] * 8
    return pl.pallas_call(
        functools.partial(_rwkvprep_kernel, tps=tps),
        grid=(t // tm,),
        in_specs=[pl.BlockSpec((tm, SHIFT_WIDTH), row),
                  pl.BlockSpec((8, SHIFT_WIDTH), lambda i: (jnp.maximum(i * (tm // 8) - 1, 0), 0)),
                  pl.BlockSpec((1, SHIFT_WIDTH), fixed),
                  vec, pl.BlockSpec((128, w), fixed), vec, pl.BlockSpec((128, w), fixed),
                  pl.BlockSpec((LORA_G, w), fixed), vec, vec, vec],
        out_specs=[pl.BlockSpec((tm, w), row)] * 8,
        out_shape=outs,
        compiler_params=_cparams("parallel"),
        name="rwkv_prep",
    )(zrw, zrw, mu, w0, w2aug, a0, a2aug, g2, k_k, k_a, r_k)


RWKV_CHUNK = 32


def _scan_kernel(r_ref, lw_ref, k_ref, v_ref, kn_ref, b_ref, bonus_ref, g_ref, lng_ref, lnb_ref, o_ref, h_ref):
    cl = r_ref.shape[0]
    w = WIDTH
    hc = N_HEADS * cl

    @pl.when(pl.program_id(1) == 0)
    def _():
        h_ref[...] = jnp.zeros_like(h_ref)

    lw = lw_ref[...]
    tri = (lax.broadcasted_iota(jnp.int32, (cl, cl), 0) >= lax.broadcasted_iota(jnp.int32, (cl, cl), 1)).astype(BF16)
    lh, ll = _split(lw)
    cs = jnp.dot(tri, lh, preferred_element_type=F32) + jnp.dot(tri, ll, preferred_element_type=F32)
    cs_end = cs[cl - 1:cl, :]
    rh = r_ref[...] * jnp.exp(cs)
    kh = kn_ref[...] * jnp.exp(cs - lw)
    inv = jnp.exp(-cs)
    bt = b_ref[...] * inv
    kt = k_ref[...] * inv
    dec_end = jnp.exp(cs_end - cs)
    bte = b_ref[...] * dec_end
    kte = k_ref[...] * dec_end

    hmask = (lax.broadcasted_iota(jnp.int32, (hc, w), 0) // cl) == (lax.broadcasted_iota(jnp.int32, (hc, w), 1) >> 6)
    stack = lambda a: jnp.where(hmask, jnp.tile(a, (N_HEADS, 1)), 0.0)
    unstack = lambda a: functools.reduce(lambda p, q: p + q, [a[h * cl:(h + 1) * cl] for h in range(N_HEADS)])
    khs, rhs, bts, kts, vs = stack(kh), stack(rh), stack(bt), stack(kt), stack(v_ref[...])

    ri = lax.broadcasted_iota(jnp.int32, (hc, hc), 0)
    ci = lax.broadcasted_iota(jnp.int32, (hc, hc), 1)
    strict = ri > ci
    incl = ri >= ci
    lb = jnp.where(strict, _dot_t(khs, bts), 0.0)
    lk = jnp.where(strict, _dot_t(khs, kts), 0.0)
    ab = jnp.where(incl, _dot_t(rhs, bts), 0.0)
    ak = jnp.where(incl, _dot_t(rhs, kts), 0.0)

    m = -lb
    tinv = jnp.where(ri == ci, 1.0, 0.0) + m
    for _ in range(int(np.log2(cl)) - 1):
        m = _dot(m, m)
        tinv = tinv + _dot(tinv, m)

    ws = _dot(tinv, khs)
    uvs = -_dot(tinv, _dot(lk, vs))
    qs = rhs - _dot(ab, ws)
    yv = unstack(_dot(ab, uvs) + _dot(ak, vs))
    h0 = h_ref[...]
    y = _dot(unstack(qs), h0) + yv

    tn = lambda a, b: lax.dot_general(a.astype(BF16), b.astype(BF16), (((0,), (0,)), ((), ())),
                                      preferred_element_type=F32)
    di = lax.broadcasted_iota(jnp.int32, (w, w), 0) == lax.broadcasted_iota(jnp.int32, (w, w), 1)
    abar = jnp.where(di, jnp.exp(cs_end), 0.0) - tn(stack(bte), ws)
    h_ref[...] = _dot3(abar, h0) + tn(stack(bte), uvs) + tn(stack(kte), vs)

    ones = _head_ones(w)
    mean = _dot2(y, ones) * (1.0 / HEAD_DIM)
    yc = y - mean
    var = _dot2(yc * yc, ones) * (1.0 / HEAD_DIM)
    yn = yc * lax.rsqrt(var + GN_EPS) * lng_ref[...] + lnb_ref[...] + bonus_ref[...]
    o_ref[...] = (yn * g_ref[...]).astype(o_ref.dtype)


def _rwkv_scan(r, lw, k, v, kn, b, bonus, g, ln_g, ln_b, bsz, seq):
    t, w = r.shape
    cl = RWKV_CHUNK
    cps = seq // cl
    row = pl.BlockSpec((cl, w), lambda bi, c: (bi * cps + c, 0))
    vec = pl.BlockSpec((1, w), lambda bi, c: (0, 0))
    return pl.pallas_call(
        _scan_kernel,
        grid=(bsz, cps),
        in_specs=[row] * 8 + [vec, vec],
        out_specs=row,
        out_shape=jax.ShapeDtypeStruct((t, w), BF16),
        scratch_shapes=[pltpu.VMEM((w, w), F32)],
        compiler_params=_cparams("parallel", "arbitrary"),
        name="rwkv_scan",
    )(r, lw, k, v, kn, b, bonus, g, ln_g, ln_b)


def _rwkv_branch(zrw, p, l, bsz, seq):
    w = WIDTH
    vec = lambda a: a.reshape(1, -1)
    zpad = lambda a, top: jnp.concatenate([a, jnp.zeros_like(a)] if top else [jnp.zeros_like(a), a], axis=0)
    outs = _rwkvprep(zrw, seq // TOKEN_TILE, vec(p['rwkv_mu'][l]), vec(p['rwkv_w0'][l]),
                     zpad(p['rwkv_w2'][l], True), vec(p['rwkv_a0'][l]), zpad(p['rwkv_a2'][l], False),
                     p['rwkv_g2'][l], vec(p['rwkv_k_k'][l]), vec(p['rwkv_k_a'][l]), vec(p['rwkv_r_k'][l]))
    return _rwkv_scan(*outs, vec(p['rwkv_ln_g'][l]), vec(p['rwkv_ln_b'][l]), bsz, seq)


def _merge_kernel(x_ref, om_ref, or_ref, gl_ref, mod_ref, g2_ref, wpm_ref, wpr_ref, wo_ref, wq_ref,
                  h_o, xn_o, q_o):
    d = x_ref.shape[1]
    mod = mod_ref[0]
    pm = jnp.dot(om_ref[...], wpm_ref[...], preferred_element_type=F32)
    pr = jnp.dot(or_ref[...], wpr_ref[...], preferred_element_type=F32)
    gl = gl_ref[...]
    mixed = _sigmoid(gl[:, :d]) * pm + _sigmoid(gl[:, d:]) * pr
    h = x_ref[...] + mod[:, 2 * d:3 * d] * _dot(mixed, wo_ref[...])
    h_o[...] = h
    xn2 = _modnorm(h, g2_ref[...], mod, d, 1)
    xn_o[...] = xn2
    q_o[...] = _dot(xn2, wq_ref[...])


def _merge(x2, o_moba, o_rwkv, gl, mod3, g2, wpm, wpr, wo, wq, tps):
    t, d = x2.shape
    tm = TOKEN_TILE
    w = WIDTH
    nq = wq.shape[1]
    row = lambda i: (i, 0)
    fixed = lambda i: (0, 0)
    return pl.pallas_call(
        _merge_kernel,
        grid=(t // tm,),
        in_specs=[pl.BlockSpec((tm, d), row), pl.BlockSpec((tm, w), row), pl.BlockSpec((tm, w), row),
                  pl.BlockSpec((tm, 2 * d), row),
                  pl.BlockSpec((1, 1, mod3.shape[2]), lambda i: (i // tps, 0, 0)),
                  pl.BlockSpec((1, d), fixed), pl.BlockSpec((w, d), fixed), pl.BlockSpec((w, d), fixed),
                  pl.BlockSpec((d, d), fixed), pl.BlockSpec((d, nq), fixed)],
        out_specs=[pl.BlockSpec((tm, d), row), pl.BlockSpec((tm, d), row), pl.BlockSpec((tm, nq), row)],
        out_shape=[jax.ShapeDtypeStruct((t, d), F32), jax.ShapeDtypeStruct((t, d), F32),
                   jax.ShapeDtypeStruct((t, nq), F32)],
        compiler_params=_cparams("parallel"),
        name="merge_proj",
    )(x2, o_moba, o_rwkv, gl, mod3, g2, wpm, wpr, wo, wq)


PEER_TILE = 128


def _split3(a):
    hi = a.astype(BF16)
    r1 = a - hi.astype(F32)
    mid = r1.astype(BF16)
    lo = (r1 - mid.astype(F32)).astype(BF16)
    return hi, mid, lo


def _dot_sel(a, sel01):
    return functools.reduce(lambda p, q: p + q,
                            [jnp.dot(part, sel01, preferred_element_type=F32) for part in _split3(a)])


def _top16(s, col, width, out_col):
    tm = s.shape[0]

    def rnd(r, carry):
        s, vals, poss = carry
        m = jnp.max(s, axis=-1, keepdims=True)
        ix = jnp.min(jnp.where(s == m, col, width), axis=-1, keepdims=True)
        hit = out_col == r
        vals = jnp.where(hit, m, vals)
        poss = jnp.where(hit, ix.astype(F32), poss)
        return jnp.where(col == ix, -jnp.inf, s), vals, poss

    z = jnp.zeros((tm, 128), F32)
    _, vals, poss = lax.fori_loop(0, PEER_TOPK, rnd, (s, z, z))
    return vals, poss


def _peer_topk_kernel(q_ref, sk_ref, idx_o, gate_o):
    tm = q_ref.shape[0]
    nk = PEER_NKEYS
    kk = PEER_TOPK
    col = lax.broadcasted_iota(jnp.int32, (tm, 128), 1)
    col2 = lax.broadcasted_iota(jnp.int32, (tm, kk * kk), 1)
    r128 = lax.broadcasted_iota(jnp.int32, (128, kk * kk), 0)
    c256 = lax.broadcasted_iota(jnp.int32, (128, kk * kk), 1)
    rep = ((c256 >> 4) == r128).astype(BF16)
    til = ((c256 & (kk - 1)) == r128).astype(BF16)
    pr = lax.broadcasted_iota(jnp.int32, (128, 128), 0)
    pc = lax.broadcasted_iota(jnp.int32, (128, 128), 1)
    idx_acc = jnp.zeros((tm, 128), F32)
    gate_acc = jnp.zeros((tm, 128), F32)
    for h in range(PEER_HEADS):
        tops = []
        for p in range(2):
            hp = 2 * h + p
            qh, ql = _split(q_ref[:, hp * PEER_HALF:(hp + 1) * PEER_HALF])
            kh, kl = _split(sk_ref[hp])
            tdot = lambda a, b: lax.dot_general(a, b, (((1,), (1,)), ((), ())), preferred_element_type=F32)
            s = tdot(qh, kh) + tdot(qh, kl) + tdot(ql, kh)
            tops.append(_top16(s, col, nk, col))
        (ts0, ti0), (ts1, ti1) = tops
        cand = _dot_sel(ts0, rep) + _dot_sel(ts1, til)
        best, pos = _top16(cand, col2, kk * kk, col)
        pos = pos.astype(jnp.int32)
        sel0 = jnp.zeros((tm, 128), F32)
        sel1 = jnp.zeros((tm, 128), F32)
        for r in range(kk):
            pr_ = pos[:, r:r + 1]
            i0 = jnp.sum(jnp.where(col == (pr_ >> 4), ti0, 0.0), axis=-1, keepdims=True)
            i1 = jnp.sum(jnp.where(col == (pr_ & (kk - 1)), ti1, 0.0), axis=-1, keepdims=True)
            sel0 = jnp.where(col == r, i0, sel0)
            sel1 = jnp.where(col == r, i1, sel1)
        bm = jnp.where(col < kk, best, -jnp.inf)
        e = jnp.exp(bm - jnp.max(bm, axis=-1, keepdims=True))
        gates = e / jnp.sum(e, axis=-1, keepdims=True)
        place = ((pc == pr + h * kk) & (pr < kk)).astype(BF16)
        gate_acc = gate_acc + _dot_sel(gates, place)
        idx_acc = (idx_acc + jnp.dot(sel0.astype(BF16), place, preferred_element_type=F32) * float(nk)
                   + jnp.dot(sel1.astype(BF16), place, preferred_element_type=F32))
    idx_o[...] = idx_acc.astype(jnp.int32)
    gate_o[...] = gate_acc


def _peer_topk(q, subkeys16):
    t, nq = q.shape
    tm = PEER_TILE
    return pl.pallas_call(
        _peer_topk_kernel,
        grid=(t // tm,),
        in_specs=[pl.BlockSpec((tm, nq), lambda i: (i, 0)),
                  pl.BlockSpec(subkeys16.shape, lambda i: (0, 0, 0))],
        out_specs=[pl.BlockSpec((tm, 128), lambda i: (i, 0)), pl.BlockSpec((tm, 128), lambda i: (i, 0))],
        out_shape=[jax.ShapeDtypeStruct((t, 128), jnp.int32), jax.ShapeDtypeStruct((t, 128), F32)],
        compiler_params=_cparams("parallel"),
        name="peer_topk",
    )(q, subkeys16)


GATHER_TOKENS = 16
N_PICK = PEER_HEADS * PEER_TOPK


def _peer_kernel(idx_cur, idx_nxt, gate_ref, xn_ref, h_ref, mod_ref, uv_hbm, o_ref, buf, sem):
    i = pl.program_id(0)
    n = pl.num_programs(0)
    slot = i % 2
    d = h_ref.shape[1]
    gt2 = mod_ref[0][:, 5 * d:6 * d]

    def row_copy(idx_ref, s, tok, e):
        return pltpu.make_async_copy(uv_hbm.at[pl.ds(idx_ref[tok, e], 1), :],
                                     buf.at[s, pl.ds(tok * N_PICK + e, 1), :], sem.at[s, tok])

    def issue_token(idx_ref, s, tok):
        for e in range(N_PICK):
            row_copy(idx_ref, s, tok, e).start()

    @pl.when(i == 0)
    def _():
        lax.fori_loop(0, GATHER_TOKENS, lambda tok, c: (issue_token(idx_cur, 0, tok), c)[1], 0)

    def token(tok, c):
        @pl.when(i + 1 < n)
        def _():
            issue_token(idx_nxt, 1 - slot, tok)

        for e in range(N_PICK):
            row_copy(idx_cur, slot, tok, e).wait()
        words = buf[slot, pl.ds(pl.multiple_of(tok * N_PICK, N_PICK), N_PICK), :]
        u = lax.bitcast_convert_type(words << 16, F32).astype(BF16)
        v = lax.bitcast_convert_type(words & jnp.uint32(0xFFFF0000), F32).astype(BF16)
        xt = xn_ref[pl.ds(tok, 1), :].astype(BF16)
        hd = lax.dot_general(xt, u, (((1,), (1,)), ((), ())), preferred_element_type=F32)
        hd = 0.5 * hd * (1.0 + lax.erf(hd * (2.0 ** -0.5))) * gate_ref[pl.ds(tok, 1), :]
        out = jnp.dot(hd.astype(BF16), v, preferred_element_type=F32)
        o_ref[pl.ds(tok, 1), :] = h_ref[pl.ds(tok, 1), :] + gt2 * out
        return c

    lax.fori_loop(0, GATHER_TOKENS, token, 0)


def _peer(idx, gates, xn2, h1, mod3, uv_words, tokens_per_seq):
    t, d = h1.shape
    tt = GATHER_TOKENS
    nsteps = t // tt
    row = lambda i: (i, 0)
    return pl.pallas_call(
        _peer_kernel,
        grid=(nsteps,),
        in_specs=[pl.BlockSpec((tt, N_PICK), row, memory_space=pltpu.SMEM),
                  pl.BlockSpec((tt, N_PICK), lambda i: (jnp.minimum(i + 1, nsteps - 1), 0), memory_space=pltpu.SMEM),
                  pl.BlockSpec((tt, N_PICK), row), pl.BlockSpec((tt, d), row), pl.BlockSpec((tt, d), row),
                  pl.BlockSpec((1, 1, mod3.shape[2]), lambda i: ((i * tt) // tokens_per_seq, 0, 0)),
                  pl.BlockSpec(memory_space=pl.ANY)],
        out_specs=pl.BlockSpec((tt, d), row),
        out_shape=jax.ShapeDtypeStruct((t, d), F32),
        scratch_shapes=[pltpu.VMEM((2, tt * N_PICK, d), jnp.uint32), pltpu.SemaphoreType.DMA((2, tt))],
        compiler_params=_cparams("arbitrary"),
        name="peer_experts",
    )(idx, idx, gates, xn2, h1, mod3, uv_words)


def _pack_uv(u, v):
    ub = lax.bitcast_convert_type(u.astype(BF16), jnp.uint16).astype(jnp.uint32)
    vb = lax.bitcast_convert_type(v.astype(BF16), jnp.uint16).astype(jnp.uint32)
    return ub | (vb << 16)


def _rope_inv128():
    half = ROPE_DIM // 2
    inv = ROPE_THETA ** (-(jnp.arange(half, dtype=F32) * 2.0) / ROPE_DIM)
    lane = np.arange(128) % HEAD_DIM
    out = jnp.where(lane < ROPE_DIM, inv[lane % half], 0.0)
    return out.reshape(1, 128)


def kernel(x, c, positions, w_ada, b_ada, norm1_g, w_in, q_norm_g, k_norm_g, rwkv_mu, rwkv_w0, rwkv_w2, rwkv_a0, rwkv_a2, rwkv_g2, rwkv_k_k, rwkv_k_a, rwkv_r_k, rwkv_ln_g, rwkv_ln_b, w_proj_moba, w_proj_rwkv, w_out, norm2_g, peer_wq, peer_subkeys, peer_u, peer_v):
    bsz, seq, d = x.shape
    t = bsz * seq
    tps = seq // TOKEN_TILE
    l = 0
    x2 = x.reshape(t, d)
    mod3 = _adaln(c, w_ada[l], b_ada[l]).reshape(bsz, 1, 6 * d)
    w_in_b = w_in[l].astype(BF16)
    g1 = norm1_g[l].reshape(1, d)
    qkv = _inproj(x2, mod3, g1, w_in_b[:, :3 * WIDTH], tps)
    qb, kb, vb, km = _qkprep(qkv, positions.reshape(t, 1), jnp.tile(q_norm_g[l], N_HEADS).reshape(1, WIDTH),
                             jnp.tile(k_norm_g[l], N_HEADS).reshape(1, WIDTH), _rope_inv128())
    o_moba = _moba(qb, kb, vb, km, bsz, seq)
    params = dict(rwkv_mu=rwkv_mu, rwkv_w0=rwkv_w0, rwkv_w2=rwkv_w2, rwkv_a0=rwkv_a0, rwkv_a2=rwkv_a2,
                  rwkv_g2=rwkv_g2, rwkv_k_k=rwkv_k_k, rwkv_k_a=rwkv_k_a, rwkv_r_k=rwkv_r_k.reshape(1, WIDTH),
                  rwkv_ln_g=rwkv_ln_g, rwkv_ln_b=rwkv_ln_b)
    zrw = _inproj(x2, mod3, g1, w_in_b[:, 3 * WIDTH:3 * WIDTH + SHIFT_WIDTH], tps)
    o_rwkv = _rwkv_branch(zrw, params, l, bsz, seq)
    gl = _inproj(x2, mod3, g1, w_in_b[:, 3 * WIDTH + SHIFT_WIDTH:], tps)
    h1, xn2, q = _merge(x2, o_moba.reshape(t, WIDTH), o_rwkv, gl, mod3, norm2_g[l].reshape(1, d),
                        w_proj_moba[l].astype(BF16), w_proj_rwkv[l].astype(BF16), w_out[l].astype(BF16),
                        peer_wq[l].astype(BF16), tps)
    idx, gates = _peer_topk(q, peer_subkeys[l].reshape(2 * PEER_HEADS, PEER_NKEYS, PEER_HALF))
    out = _peer(idx, gates, xn2, h1, mod3, _pack_uv(peer_u[l], peer_v[l]), seq)
    return out.reshape(bsz, seq, d)
```

```python
import functools

import numpy as np
import jax
import jax.numpy as jnp
from jax import lax
from jax.experimental import pallas as pl
from jax.experimental.pallas import tpu as pltpu

F32 = jnp.float32
BF16 = jnp.bfloat16

HEAD_DIM = 64
N_HEADS = 8
WIDTH = N_HEADS * HEAD_DIM
MOBA_BLOCK = 256
MOBA_TOPK = 3
ROPE_THETA = 500000.0
ROPE_DIM = HEAD_DIM // 4
LORA_W = 64
LORA_A = 64
LORA_G = 128
SHIFT_WIDTH = 3 * WIDTH + LORA_W + LORA_A + LORA_G
GN_EPS = 64e-5
RMS_EPS = 1e-6
NEG_INF = -1e30
PEER_HEADS = 8
PEER_NKEYS = 128
PEER_HALF = 128
PEER_TOPK = 16

TOKEN_TILE = 256
V7X_VMEM_LIMIT = 48 * 1024 * 1024


def _cparams(*sem, flags=None):
    return pltpu.CompilerParams(dimension_semantics=sem, vmem_limit_bytes=V7X_VMEM_LIMIT, flags=flags)


def _dot(a, b):
    return jnp.dot(a.astype(BF16), b.astype(BF16), preferred_element_type=F32)


def _dot_t(a, b):
    return lax.dot_general(a.astype(BF16), b.astype(BF16), (((1,), (1,)), ((), ())),
                           preferred_element_type=F32)


def _split(a):
    hi = a.astype(BF16)
    lo = (a - hi.astype(F32)).astype(BF16)
    return hi, lo


def _dot3(a, b):
    ah, al = _split(a)
    bh, bl = _split(b)
    return (jnp.dot(ah, bh, preferred_element_type=F32) + jnp.dot(ah, bl, preferred_element_type=F32)
            + jnp.dot(al, bh, preferred_element_type=F32))


def _dot2(a, b_exact):
    ah, al = _split(a)
    return jnp.dot(ah, b_exact, preferred_element_type=F32) + jnp.dot(al, b_exact, preferred_element_type=F32)


def _head_ones(n):
    r = lax.broadcasted_iota(jnp.int32, (n, n), 0) >> 6
    c = lax.broadcasted_iota(jnp.int32, (n, n), 1) >> 6
    return (r == c).astype(BF16)


def _sigmoid(x):
    return 1.0 / (1.0 + jnp.exp(-x))


def _ada_kernel(c_ref, w_ref, b_ref, o_ref):
    c = c_ref[...]
    o_ref[...] = _dot3(c * _sigmoid(c), w_ref[...]) + b_ref[...]


def _adaln(c, w_ada, b_ada):
    bsz, d = c.shape
    n = w_ada.shape[1]
    nb = 1024
    return pl.pallas_call(
        _ada_kernel,
        grid=(n // nb,),
        in_specs=[pl.BlockSpec((bsz, d), lambda j: (0, 0)),
                  pl.BlockSpec((d, nb), lambda j: (0, j)),
                  pl.BlockSpec((1, nb), lambda j: (0, j))],
        out_specs=pl.BlockSpec((bsz, nb), lambda j: (0, j)),
        out_shape=jax.ShapeDtypeStruct((bsz, n), F32),
        compiler_params=_cparams("arbitrary"),
        name="adaln_mod",
    )(c, w_ada, b_ada.reshape(1, n))


def _modnorm(x, g, mod, d, which):
    ms = jnp.mean(x * x, axis=-1, keepdims=True)
    y = x * lax.rsqrt(ms + RMS_EPS) * g
    sh = mod[:, (3 * which) * d:(3 * which + 1) * d]
    sc = mod[:, (3 * which + 1) * d:(3 * which + 2) * d]
    return y * (1.0 + sc) + sh


def _inproj_kernel(x_ref, mod_ref, g_ref, w_ref, o_ref):
    d = x_ref.shape[1]
    xn = _modnorm(x_ref[...], g_ref[...], mod_ref[0], d, 0)
    o_ref[...] = jnp.dot(xn.astype(BF16), w_ref[...], preferred_element_type=F32)


def _inproj(x2, mod3, g, w_bf16, tps):
    t, d = x2.shape
    n = w_bf16.shape[1]
    tm = TOKEN_TILE
    return pl.pallas_call(
        _inproj_kernel,
        grid=(t // tm,),
        in_specs=[pl.BlockSpec((tm, d), lambda i: (i, 0)),
                  pl.BlockSpec((1, 1, mod3.shape[2]), lambda i: (i // tps, 0, 0)),
                  pl.BlockSpec((1, d), lambda i: (0, 0)),
                  pl.BlockSpec((d, n), lambda i: (0, 0))],
        out_specs=pl.BlockSpec((tm, n), lambda i: (i, 0)),
        out_shape=jax.ShapeDtypeStruct((t, n), F32),
        compiler_params=_cparams("parallel"),
        name="in_proj",
    )(x2, mod3, g, w_bf16)


def _qkprep_kernel(qkv_ref, pos_ref, gq_ref, gk_ref, inv_ref, qo_ref, ko_ref, vo_ref, km_ref):
    w = WIDTH
    ones = _head_ones(w)
    ang = pos_ref[...].astype(F32) * inv_ref[...]
    cos = jnp.tile(jnp.cos(ang), (1, w // 128))
    sin = jnp.tile(jnp.sin(ang), (1, w // 128))
    dd = lax.broadcasted_iota(jnp.int32, (1, w), 1) & (HEAD_DIM - 1)
    half = ROPE_DIM // 2
    s_lo = jnp.where(dd < half, -1.0, 0.0) * sin
    s_hi = jnp.where((dd >= half) & (dd < ROPE_DIM), 1.0, 0.0) * sin

    def norm_rope(xh, g):
        ssq = _dot2(xh * xh, ones)
        y = xh * lax.rsqrt(ssq * (1.0 / HEAD_DIM) + RMS_EPS) * g
        return y * cos + pltpu.roll(y, w - half, 1) * s_lo + pltpu.roll(y, half, 1) * s_hi

    q = norm_rope(qkv_ref[:, 0:w], gq_ref[...])
    k = norm_rope(qkv_ref[:, w:2 * w], gk_ref[...])
    qo_ref[...] = (q * (HEAD_DIM ** -0.5)).astype(BF16)
    ko_ref[...] = k.astype(BF16)
    vo_ref[...] = qkv_ref[:, 2 * w:3 * w].astype(BF16)
    km_ref[0] = jnp.mean(k, axis=0, keepdims=True)


def _qkprep(qkv, pos2, gq, gk, inv128):
    t = qkv.shape[0]
    tm = MOBA_BLOCK
    w = WIDTH
    row = lambda i: (i, 0)
    fixed = lambda i: (0, 0)
    return pl.pallas_call(
        _qkprep_kernel,
        grid=(t // tm,),
        in_specs=[pl.BlockSpec((tm, 3 * w), row), pl.BlockSpec((tm, 1), row),
                  pl.BlockSpec((1, w), fixed), pl.BlockSpec((1, w), fixed), pl.BlockSpec((1, 128), fixed)],
        out_specs=[pl.BlockSpec((tm, w), row), pl.BlockSpec((tm, w), row), pl.BlockSpec((tm, w), row),
                   pl.BlockSpec((1, 1, w), lambda i: (i, 0, 0))],
        out_shape=[jax.ShapeDtypeStruct((t, w), BF16), jax.ShapeDtypeStruct((t, w), BF16),
                   jax.ShapeDtypeStruct((t, w), BF16), jax.ShapeDtypeStruct((t // tm, 1, w), F32)],
        compiler_params=_cparams("parallel"),
        name="moba_qk_prep",
    )(qkv, pos2, gq, gk, inv128)


def _moba_kernel(q_ref, k_ref, v_ref, km_ref, o_ref):
    blk = MOBA_BLOCK
    i = pl.program_id(2)
    q = q_ref[0]
    km = km_ref[0]
    nb = km.shape[0]
    lane = lax.broadcasted_iota(jnp.int32, (blk, 128), 1)
    col = lax.broadcasted_iota(jnp.int32, (blk, nb), 1)
    qpos = lax.broadcasted_iota(jnp.int32, (blk, blk), 0)
    kpos = lax.broadcasted_iota(jnp.int32, (blk, blk), 1)
    row0 = pl.multiple_of(i * blk, blk)
    k_own = k_ref[0, pl.ds(row0, blk), :]
    v_own = v_ref[0, pl.ds(row0, blk), :]

    qs, biases, carry0 = [], [], []
    for hh in range(2):
        qh = jnp.where((lane >> 6) == hh, q, jnp.zeros_like(q))
        g = jnp.where(col < i, _dot_t(qh, km), -jnp.inf)
        sel = jnp.zeros((blk, nb), jnp.bool_)
        for _ in range(MOBA_TOPK):
            m = jnp.max(g, axis=-1, keepdims=True)
            idx = jnp.min(jnp.where(g == m, col, nb), axis=-1, keepdims=True)
            pick = (col == idx) & (g > -jnp.inf)
            sel = sel | pick
            g = jnp.where(pick, -jnp.inf, g)
        biases.append(jnp.where(sel, 0.0, NEG_INF))
        s = jnp.where(kpos <= qpos, _dot_t(qh, k_own), NEG_INF)
        m0 = jnp.max(s, axis=-1, keepdims=True)
        p = jnp.exp(s - m0)
        carry0 += [m0, jnp.sum(p, axis=-1, keepdims=True), _dot(p, v_own)]
        qs.append(qh)

    def body(jj, carry):
        r0 = pl.multiple_of(jj * (2 * blk), 2 * blk)
        kj = k_ref[0, pl.ds(r0, 2 * blk), :]
        vj = v_ref[0, pl.ds(r0, 2 * blk), :]
        out = []
        for hh in range(2):
            m, l, acc = carry[3 * hh:3 * hh + 3]
            ss = []
            for u in range(2):
                bj = jnp.sum(jnp.where(col == 2 * jj + u, biases[hh], 0.0), axis=-1, keepdims=True)
                ss.append(_dot_t(qs[hh], kj[u * blk:(u + 1) * blk]) + bj)
            m_new = jnp.maximum(m, jnp.maximum(jnp.max(ss[0], axis=-1, keepdims=True),
                                               jnp.max(ss[1], axis=-1, keepdims=True)))
            alpha = jnp.exp(m - m_new)
            p = jnp.concatenate([jnp.exp(ss[0] - m_new), jnp.exp(ss[1] - m_new)], axis=1)
            out += [m_new, alpha * l + jnp.sum(p, axis=-1, keepdims=True), alpha * acc + _dot(p, vj)]
        return tuple(out)

    fin = lax.fori_loop(0, (i + 1) // 2, body, tuple(carry0))
    o0 = fin[2] / fin[1]
    o1 = fin[5] / fin[4]
    o_ref[0] = jnp.where(lane < HEAD_DIM, o0, o1).astype(o_ref.dtype)


def _moba(qb, kb, vb, km, bsz, seq):
    w = WIDTH
    nb = seq // MOBA_BLOCK
    q3 = qb.reshape(bsz, seq, w)
    k3 = kb.reshape(bsz, seq, w)
    v3 = vb.reshape(bsz, seq, w)
    km3 = km.reshape(bsz, nb, w)
    return pl.pallas_call(
        _moba_kernel,
        grid=(bsz, w // 128, nb),
        in_specs=[pl.BlockSpec((1, MOBA_BLOCK, 128), lambda b, p, i: (b, i, p)),
                  pl.BlockSpec((1, seq, 128), lambda b, p, i: (b, 0, p)),
                  pl.BlockSpec((1, seq, 128), lambda b, p, i: (b, 0, p)),
                  pl.BlockSpec((1, nb, 128), lambda b, p, i: (b, 0, p))],
        out_specs=pl.BlockSpec((1, MOBA_BLOCK, 128), lambda b, p, i: (b, i, p)),
        out_shape=jax.ShapeDtypeStruct((bsz, seq, w), BF16),
        compiler_params=_cparams("parallel", "parallel", "arbitrary"),
        name="moba_attention",
    )(q3, k3, v3, km3)


def _rwkvprep_kernel(z_ref, zp_ref, mu_ref, w0_ref, w2_ref, a0_ref, a2_ref, g2_ref, kk_ref, ka_ref, rk_ref,
                     r_o, lw_o, k_o, v_o, kn_o, b_o, bonus_o, g_o, *, tps):
    i = pl.program_id(0)
    w = WIDTH
    z = z_ref[...]
    tm = z.shape[0]
    prev_last = jnp.where((i % tps) == 0, 0.0, zp_ref[7:8, :])
    rows = lax.broadcasted_iota(jnp.int32, (tm, 1), 0)
    prev = jnp.where(rows == 0, prev_last, pltpu.roll(z, 1, 0))
    zs = z + (prev - z) * mu_ref[...]
    zr, zk, zv = zs[:, 0:w], zs[:, w:2 * w], zs[:, 2 * w:3 * w]
    zwa = zs[:, 3 * w:3 * w + 128]
    zg = zs[:, 3 * w + 128:3 * w + 256]
    nw = -(w0_ref[...] + _dot3(jnp.tanh(zwa), w2_ref[...]))
    softplus = jnp.maximum(nw, 0.0) + jnp.log(1.0 + jnp.exp(-jnp.abs(nw)))
    lw_o[...] = -jnp.exp(-softplus - 0.5)
    a = _sigmoid(a0_ref[...] + _dot3(zwa, a2_ref[...]))
    g_o[...] = _dot3(_sigmoid(zg), g2_ref[...])
    ones = _head_ones(w)
    kn = zk * kk_ref[...]
    kn = kn / jnp.maximum(jnp.sqrt(_dot2(kn * kn, ones)), 1e-12)
    k = zk * (1.0 + (a - 1.0) * ka_ref[...])
    r_o[...] = zr
    k_o[...] = k
    v_o[...] = zv
    kn_o[...] = kn
    b_o[...] = kn * a
    bonus_o[...] = _dot2(zr * k * rk_ref[...], ones) * zv


def _rwkvprep(zrw, tps, mu, w0, w2aug, a0, a2aug, g2, k_k, k_a, r_k):
    t = zrw.shape[0]
    tm = TOKEN_TILE
    w = WIDTH
    row = lambda i: (i, 0)
    fixed = lambda i: (0, 0)
    vec = pl.BlockSpec((1, w), fixed)
    outs = [jax.ShapeDtypeStruct((t, w), F32)] * 8
    return pl.pallas_call(
        functools.partial(_rwkvprep_kernel, tps=tps),
        grid=(t // tm,),
        in_specs=[pl.BlockSpec((tm, SHIFT_WIDTH), row),
                  pl.BlockSpec((8, SHIFT_WIDTH), lambda i: (jnp.maximum(i * (tm // 8) - 1, 0), 0)),
                  pl.BlockSpec((1, SHIFT_WIDTH), fixed),
                  vec, pl.BlockSpec((128, w), fixed), vec, pl.BlockSpec((128, w), fixed),
                  pl.BlockSpec((LORA_G, w), fixed), vec, vec, vec],
        out_specs=[pl.BlockSpec((tm, w), row)] * 8,
        out_shape=outs,
        compiler_params=_cparams("parallel"),
        name="rwkv_prep",
    )(zrw, zrw, mu, w0, w2aug, a0, a2aug, g2, k_k, k_a, r_k)


RWKV_CHUNK = 32


SCAN_GROUP = 4
SCAN_CHUNKS_PER_STEP = 4


def _scan_group(r, lw, k, v, kn, b, h0):
    cl, gw = r.shape
    hc = SCAN_GROUP * cl
    tri = (lax.broadcasted_iota(jnp.int32, (cl, cl), 0) >= lax.broadcasted_iota(jnp.int32, (cl, cl), 1)).astype(BF16)
    lh, ll = _split(lw)
    cs = jnp.dot(tri, lh, preferred_element_type=F32) + jnp.dot(tri, ll, preferred_element_type=F32)
    cs_end = cs[cl - 1:cl, :]
    rh = r * jnp.exp(cs)
    kh = kn * jnp.exp(cs - lw)
    inv = jnp.exp(-cs)
    bt = b * inv
    kt = k * inv
    dec_end = jnp.exp(cs_end - cs)
    bte = b * dec_end
    kte = k * dec_end

    hmask = (lax.broadcasted_iota(jnp.int32, (hc, gw), 0) // cl) == (lax.broadcasted_iota(jnp.int32, (hc, gw), 1) >> 6)
    stack = lambda a: jnp.where(hmask, jnp.tile(a, (SCAN_GROUP, 1)), 0.0)
    unstack = lambda a: functools.reduce(lambda p, q: p + q, [a[h * cl:(h + 1) * cl] for h in range(SCAN_GROUP)])
    khs, rhs, bts, kts, vs = stack(kh), stack(rh), stack(bt), stack(kt), stack(v)

    ri = lax.broadcasted_iota(jnp.int32, (hc, hc), 0)
    ci = lax.broadcasted_iota(jnp.int32, (hc, hc), 1)
    strict = ri > ci
    incl = ri >= ci
    lb = jnp.where(strict, _dot_t(khs, bts), 0.0)
    lk = jnp.where(strict, _dot_t(khs, kts), 0.0)
    ab = jnp.where(incl, _dot_t(rhs, bts), 0.0)
    ak = jnp.where(incl, _dot_t(rhs, kts), 0.0)

    m = -lb
    tinv = jnp.where(ri == ci, 1.0, 0.0) + m
    for _ in range(int(np.log2(cl)) - 1):
        m = _dot(m, m)
        tinv = tinv + _dot(tinv, m)

    ws = _dot(tinv, khs)
    uvs = -_dot(tinv, _dot(lk, vs))
    qs = rhs - _dot(ab, ws)
    yv = unstack(_dot(ab, uvs) + _dot(ak, vs))
    y = _dot(unstack(qs), h0) + yv

    tn = lambda a, b: lax.dot_general(a.astype(BF16), b.astype(BF16), (((0,), (0,)), ((), ())),
                                      preferred_element_type=F32)
    di = lax.broadcasted_iota(jnp.int32, (gw, gw), 0) == lax.broadcasted_iota(jnp.int32, (gw, gw), 1)
    btes = stack(bte)
    abar = jnp.where(di, jnp.exp(cs_end), 0.0) - tn(btes, ws)
    return y, _dot3(abar, h0) + tn(btes, uvs) + tn(stack(kte), vs)


def _scan_kernel(r_ref, lw_ref, k_ref, v_ref, kn_ref, b_ref, bonus_ref, g_ref, lng_ref, lnb_ref, o_ref, h_ref):
    gw = SCAN_GROUP * HEAD_DIM

    @pl.when(pl.program_id(1) == 0)
    def _():
        h_ref[...] = jnp.zeros_like(h_ref)

    ones = _head_ones(gw)
    cl = RWKV_CHUNK
    for grp in range(N_HEADS // SCAN_GROUP):
        ln = slice(grp * gw, (grp + 1) * gw)
        h = h_ref[grp]
        for c in range(SCAN_CHUNKS_PER_STEP):
            rows = slice(c * cl, (c + 1) * cl)
            y, h = _scan_group(r_ref[rows, ln], lw_ref[rows, ln], k_ref[rows, ln], v_ref[rows, ln],
                               kn_ref[rows, ln], b_ref[rows, ln], h)
            mean = _dot2(y, ones) * (1.0 / HEAD_DIM)
            yc = y - mean
            var = _dot2(yc * yc, ones) * (1.0 / HEAD_DIM)
            yn = yc * lax.rsqrt(var + GN_EPS) * lng_ref[:, ln] + lnb_ref[:, ln] + bonus_ref[rows, ln]
            o_ref[rows, ln] = (yn * g_ref[rows, ln]).astype(o_ref.dtype)
        h_ref[grp] = h


def _rwkv_scan(r, lw, k, v, kn, b, bonus, g, ln_g, ln_b, bsz, seq):
    t, w = r.shape
    cl = RWKV_CHUNK * SCAN_CHUNKS_PER_STEP
    cps = seq // cl
    row = pl.BlockSpec((cl, w), lambda bi, c: (bi * cps + c, 0))
    vec = pl.BlockSpec((1, w), lambda bi, c: (0, 0))
    return pl.pallas_call(
        _scan_kernel,
        grid=(bsz, cps),
        in_specs=[row] * 8 + [vec, vec],
        out_specs=row,
        out_shape=jax.ShapeDtypeStruct((t, w), BF16),
        scratch_shapes=[pltpu.VMEM((N_HEADS // SCAN_GROUP, SCAN_GROUP * HEAD_DIM, SCAN_GROUP * HEAD_DIM), F32)],
        compiler_params=_cparams("parallel", "arbitrary"),
        name="rwkv_scan",
    )(r, lw, k, v, kn, b, bonus, g, ln_g, ln_b)


def _rwkv_branch(zrw, p, l, bsz, seq):
    w = WIDTH
    vec = lambda a: a.reshape(1, -1)
    zpad = lambda a, top: jnp.concatenate([a, jnp.zeros_like(a)] if top else [jnp.zeros_like(a), a], axis=0)
    outs = _rwkvprep(zrw, seq // TOKEN_TILE, vec(p['rwkv_mu'][l]), vec(p['rwkv_w0'][l]),
                     zpad(p['rwkv_w2'][l], True), vec(p['rwkv_a0'][l]), zpad(p['rwkv_a2'][l], False),
                     p['rwkv_g2'][l], vec(p['rwkv_k_k'][l]), vec(p['rwkv_k_a'][l]), vec(p['rwkv_r_k'][l]))
    return _rwkv_scan(*outs, vec(p['rwkv_ln_g'][l]), vec(p['rwkv_ln_b'][l]), bsz, seq)


def _merge_kernel(x_ref, om_ref, or_ref, gl_ref, mod_ref, g2_ref, wpm_ref, wpr_ref, wo_ref, wq_ref,
                  h_o, xn_o, q_o):
    d = x_ref.shape[1]
    mod = mod_ref[0]
    pm = jnp.dot(om_ref[...], wpm_ref[...], preferred_element_type=F32)
    pr = jnp.dot(or_ref[...], wpr_ref[...], preferred_element_type=F32)
    gl = gl_ref[...]
    mixed = _sigmoid(gl[:, :d]) * pm + _sigmoid(gl[:, d:]) * pr
    h = x_ref[...] + mod[:, 2 * d:3 * d] * _dot(mixed, wo_ref[...])
    h_o[...] = h
    xn2 = _modnorm(h, g2_ref[...], mod, d, 1)
    xn_o[...] = xn2
    q_o[...] = _dot(xn2, wq_ref[...])


def _merge(x2, o_moba, o_rwkv, gl, mod3, g2, wpm, wpr, wo, wq, tps):
    t, d = x2.shape
    tm = TOKEN_TILE
    w = WIDTH
    nq = wq.shape[1]
    row = lambda i: (i, 0)
    fixed = lambda i: (0, 0)
    return pl.pallas_call(
        _merge_kernel,
        grid=(t // tm,),
        in_specs=[pl.BlockSpec((tm, d), row), pl.BlockSpec((tm, w), row), pl.BlockSpec((tm, w), row),
                  pl.BlockSpec((tm, 2 * d), row),
                  pl.BlockSpec((1, 1, mod3.shape[2]), lambda i: (i // tps, 0, 0)),
                  pl.BlockSpec((1, d), fixed), pl.BlockSpec((w, d), fixed), pl.BlockSpec((w, d), fixed),
                  pl.BlockSpec((d, d), fixed), pl.BlockSpec((d, nq), fixed)],
        out_specs=[pl.BlockSpec((tm, d), row), pl.BlockSpec((tm, d), row), pl.BlockSpec((tm, nq), row)],
        out_shape=[jax.ShapeDtypeStruct((t, d), F32), jax.ShapeDtypeStruct((t, d), F32),
                   jax.ShapeDtypeStruct((t, nq), F32)],
        compiler_params=_cparams("parallel"),
        name="merge_proj",
    )(x2, o_moba, o_rwkv, gl, mod3, g2, wpm, wpr, wo, wq)


PEER_TILE = 128


def _top16_rows(s):
    n, tm = s.shape
    key = lax.broadcasted_iota(jnp.int32, (n, tm), 0)
    rank = lax.broadcasted_iota(jnp.int32, (PEER_TOPK, tm), 0)

    def rnd(r, carry):
        s, vals, idxs = carry
        m = jnp.max(s, axis=0, keepdims=True)
        ix = jnp.min(jnp.where(s == m, key, n), axis=0, keepdims=True)
        hit = rank == r
        return (jnp.where(key == ix, -jnp.inf, s), jnp.where(hit, m, vals),
                jnp.where(hit, ix.astype(F32), idxs))

    z = jnp.zeros((PEER_TOPK, tm), F32)
    _, vals, idxs = lax.fori_loop(0, PEER_TOPK, rnd, (s, z, z))
    return vals, idxs


def _peer_topk_kernel(q_ref, sk_ref, idx_o, gate_o):
    tm = q_ref.shape[0]
    kk = PEER_TOPK
    rank = lax.broadcasted_iota(jnp.int32, (kk, tm), 0)
    tdot = lambda a, b: lax.dot_general(a, b, (((1,), (1,)), ((), ())), preferred_element_type=F32)
    idx_rows, gate_rows = [], []
    for h in range(PEER_HEADS):
        tops = []
        for p in range(2):
            hp = 2 * h + p
            qh, ql = _split(q_ref[:, hp * PEER_HALF:(hp + 1) * PEER_HALF])
            kh, kl = _split(sk_ref[hp])
            tops.append(_top16_rows(tdot(kh, qh) + tdot(kh, ql) + tdot(kl, qh)))
        (ts0, ti0), (ts1, ti1) = tops
        cand = jnp.concatenate([ts0[a:a + 1, :] + ts1 for a in range(kk)], axis=0)
        best, pos = _top16_rows(cand)
        pos = pos.astype(jnp.int32)
        ids = jnp.zeros((kk, tm), F32)
        for r in range(kk):
            pr = pos[r:r + 1, :]
            i0 = jnp.sum(jnp.where(rank == (pr >> 4), ti0, 0.0), axis=0, keepdims=True)
            i1 = jnp.sum(jnp.where(rank == (pr & (kk - 1)), ti1, 0.0), axis=0, keepdims=True)
            ids = jnp.where(rank == r, i0 * float(PEER_NKEYS) + i1, ids)
        e = jnp.exp(best - jnp.max(best, axis=0, keepdims=True))
        gate_rows.append(e / jnp.sum(e, axis=0, keepdims=True))
        idx_rows.append(ids)
    idx_o[...] = jnp.concatenate(idx_rows, axis=0).T.astype(jnp.int32)
    gate_o[...] = jnp.concatenate(gate_rows, axis=0).T


def _peer_topk(q, subkeys16):
    t, nq = q.shape
    tm = PEER_TILE
    return pl.pallas_call(
        _peer_topk_kernel,
        grid=(t // tm,),
        in_specs=[pl.BlockSpec((tm, nq), lambda i: (i, 0)),
                  pl.BlockSpec(subkeys16.shape, lambda i: (0, 0, 0))],
        out_specs=[pl.BlockSpec((tm, 128), lambda i: (i, 0)), pl.BlockSpec((tm, 128), lambda i: (i, 0))],
        out_shape=[jax.ShapeDtypeStruct((t, 128), jnp.int32), jax.ShapeDtypeStruct((t, 128), F32)],
        compiler_params=_cparams("parallel"),
        name="peer_topk",
    )(q, subkeys16)


GATHER_TOKENS = 16
N_PICK = PEER_HEADS * PEER_TOPK


def _peer_kernel(idx_cur, idx_nxt, gate_ref, xn_ref, h_ref, mod_ref, uv_hbm, o_ref, buf, sem):
    i = pl.program_id(0)
    n = pl.num_programs(0)
    slot = i % 2
    d = h_ref.shape[1]
    gt2 = mod_ref[0][:, 5 * d:6 * d]

    def row_copy(idx_ref, s, tok, e):
        return pltpu.make_async_copy(uv_hbm.at[idx_ref[tok, e]],
                                     buf.at[s * GATHER_TOKENS + tok, pl.ds(e, 1), :], sem.at[s, tok])

    def issue_token(idx_ref, s, tok):
        for e in range(N_PICK):
            row_copy(idx_ref, s, tok, e).start(priority=e % 2)

    @pl.when(i == 0)
    def _():
        lax.fori_loop(0, GATHER_TOKENS, lambda tok, c: (issue_token(idx_cur, 0, tok), c)[1], 0)

    def token(tok, c):
        for e in range(N_PICK):
            row_copy(idx_cur, slot, tok, e).wait()
        words = buf[slot * GATHER_TOKENS + tok]
        u = lax.bitcast_convert_type(words << 16, F32).astype(BF16)
        v = lax.bitcast_convert_type(words & jnp.uint32(0xFFFF0000), F32).astype(BF16)
        xt = xn_ref[pl.ds(tok, 1), :].astype(BF16)
        hd = lax.dot_general(xt, u, (((1,), (1,)), ((), ())), preferred_element_type=F32)
        hd = 0.5 * hd * (1.0 + lax.erf(hd * (2.0 ** -0.5))) * gate_ref[pl.ds(tok, 1), :]
        out = jnp.dot(hd.astype(BF16), v, preferred_element_type=F32)
        o_ref[pl.ds(tok, 1), :] = h_ref[pl.ds(tok, 1), :] + gt2 * out
        issue_token(idx_nxt, 1 - slot, tok)
        return c

    lax.fori_loop(0, GATHER_TOKENS, token, 0)

    @pl.when(i == n - 1)
    def _():
        def drain(tok, c):
            for e in range(N_PICK):
                row_copy(idx_nxt, 1 - slot, tok, e).wait()
            return c
        lax.fori_loop(0, GATHER_TOKENS, drain, 0)


def _peer(idx, gates, xn2, h1, mod3, uv_words, tokens_per_seq):
    t, d = h1.shape
    tt = GATHER_TOKENS
    nsteps = t // tt
    row = lambda i: (i, 0)
    return pl.pallas_call(
        _peer_kernel,
        grid=(nsteps,),
        in_specs=[pl.BlockSpec((tt, N_PICK), row, memory_space=pltpu.SMEM),
                  pl.BlockSpec((tt, N_PICK), lambda i: (jnp.minimum(i + 1, nsteps - 1), 0), memory_space=pltpu.SMEM),
                  pl.BlockSpec((tt, N_PICK), row), pl.BlockSpec((tt, d), row), pl.BlockSpec((tt, d), row),
                  pl.BlockSpec((1, 1, mod3.shape[2]), lambda i: ((i * tt) // tokens_per_seq, 0, 0)),
                  pl.BlockSpec(memory_space=pl.ANY)],
        out_specs=pl.BlockSpec((tt, d), row),
        out_shape=jax.ShapeDtypeStruct((t, d), F32),
        scratch_shapes=[pltpu.VMEM((2 * tt, N_PICK, d), jnp.uint32), pltpu.SemaphoreType.DMA((2, tt))],
        compiler_params=_cparams("arbitrary"),
        name="peer_experts",
    )(idx, idx, gates, xn2, h1, mod3, uv_words)


def _pack_uv(u, v):
    ub = lax.bitcast_convert_type(u.astype(BF16), jnp.uint16).astype(jnp.uint32)
    vb = lax.bitcast_convert_type(v.astype(BF16), jnp.uint16).astype(jnp.uint32)
    return (ub | (vb << 16)).reshape(u.shape[0], 1, u.shape[1])


def _rope_inv128():
    half = ROPE_DIM // 2
    inv = ROPE_THETA ** (-(jnp.arange(half, dtype=F32) * 2.0) / ROPE_DIM)
    lane = np.arange(128) % HEAD_DIM
    out = jnp.where(lane < ROPE_DIM, inv[lane % half], 0.0)
    return out.reshape(1, 128)


def kernel(x, c, positions, w_ada, b_ada, norm1_g, w_in, q_norm_g, k_norm_g, rwkv_mu, rwkv_w0, rwkv_w2, rwkv_a0, rwkv_a2, rwkv_g2, rwkv_k_k, rwkv_k_a, rwkv_r_k, rwkv_ln_g, rwkv_ln_b, w_proj_moba, w_proj_rwkv, w_out, norm2_g, peer_wq, peer_subkeys, peer_u, peer_v):
    bsz, seq, d = x.shape
    t = bsz * seq
    tps = seq // TOKEN_TILE
    l = 0
    x2 = x.reshape(t, d)
    mod3 = _adaln(c, w_ada[l], b_ada[l]).reshape(bsz, 1, 6 * d)
    w_in_b = w_in[l].astype(BF16)
    g1 = norm1_g[l].reshape(1, d)
    qkv = _inproj(x2, mod3, g1, w_in_b[:, :3 * WIDTH], tps)
    qb, kb, vb, km = _qkprep(qkv, positions.reshape(t, 1), jnp.tile(q_norm_g[l], N_HEADS).reshape(1, WIDTH),
                             jnp.tile(k_norm_g[l], N_HEADS).reshape(1, WIDTH), _rope_inv128())
    o_moba = _moba(qb, kb, vb, km, bsz, seq)
    params = dict(rwkv_mu=rwkv_mu, rwkv_w0=rwkv_w0, rwkv_w2=rwkv_w2, rwkv_a0=rwkv_a0, rwkv_a2=rwkv_a2,
                  rwkv_g2=rwkv_g2, rwkv_k_k=rwkv_k_k, rwkv_k_a=rwkv_k_a, rwkv_r_k=rwkv_r_k.reshape(1, WIDTH),
                  rwkv_ln_g=rwkv_ln_g, rwkv_ln_b=rwkv_ln_b)
    zrw = _inproj(x2, mod3, g1, w_in_b[:, 3 * WIDTH:3 * WIDTH + SHIFT_WIDTH], tps)
    o_rwkv = _rwkv_branch(zrw, params, l, bsz, seq)
    gl = _inproj(x2, mod3, g1, w_in_b[:, 3 * WIDTH + SHIFT_WIDTH:], tps)
    h1, xn2, q = _merge(x2, o_moba.reshape(t, WIDTH), o_rwkv, gl, mod3, norm2_g[l].reshape(1, d),
                        w_proj_moba[l].astype(BF16), w_proj_rwkv[l].astype(BF16), w_out[l].astype(BF16),
                        peer_wq[l].astype(BF16), tps)
    idx, gates = _peer_topk(q, peer_subkeys[l].reshape(2 * PEER_HEADS, PEER_NKEYS, PEER_HALF))
    out = _peer(idx, gates, xn2, h1, mod3, _pack_uv(peer_u[l], peer_v[l]), seq)
    return out.reshape(bsz, seq, d)
```

```python
import functools

import numpy as np
import jax
import jax.numpy as jnp
from jax import lax
from jax.experimental import pallas as pl
from jax.experimental.pallas import tpu as pltpu

F32 = jnp.float32
BF16 = jnp.bfloat16

HEAD_DIM = 64
N_HEADS = 8
WIDTH = N_HEADS * HEAD_DIM
MOBA_BLOCK = 256
MOBA_TOPK = 3
ROPE_THETA = 500000.0
ROPE_DIM = HEAD_DIM // 4
LORA_W = 64
LORA_A = 64
LORA_G = 128
SHIFT_WIDTH = 3 * WIDTH + LORA_W + LORA_A + LORA_G
GN_EPS = 64e-5
RMS_EPS = 1e-6
NEG_INF = -1e30
PEER_HEADS = 8
PEER_NKEYS = 128
PEER_HALF = 128
PEER_TOPK = 16

TOKEN_TILE = 256
V7X_VMEM_LIMIT = 48 * 1024 * 1024


def _cparams(*sem, flags=None):
    return pltpu.CompilerParams(dimension_semantics=sem, vmem_limit_bytes=V7X_VMEM_LIMIT, flags=flags)


def _dot(a, b):
    return jnp.dot(a.astype(BF16), b.astype(BF16), preferred_element_type=F32)


def _dot_t(a, b):
    return lax.dot_general(a.astype(BF16), b.astype(BF16), (((1,), (1,)), ((), ())),
                           preferred_element_type=F32)


def _split(a):
    hi = a.astype(BF16)
    lo = (a - hi.astype(F32)).astype(BF16)
    return hi, lo


def _split3(a):
    hi = a.astype(BF16)
    r1 = a - hi.astype(F32)
    mid = r1.astype(BF16)
    return hi, mid, (r1 - mid.astype(F32)).astype(BF16)


def _dot3(a, b):
    ah, al = _split(a)
    bh, bl = _split(b)
    return (jnp.dot(ah, bh, preferred_element_type=F32) + jnp.dot(ah, bl, preferred_element_type=F32)
            + jnp.dot(al, bh, preferred_element_type=F32))


def _dot2(a, b_exact):
    ah, al = _split(a)
    return jnp.dot(ah, b_exact, preferred_element_type=F32) + jnp.dot(al, b_exact, preferred_element_type=F32)


def _head_ones(n):
    r = lax.broadcasted_iota(jnp.int32, (n, n), 0) >> 6
    c = lax.broadcasted_iota(jnp.int32, (n, n), 1) >> 6
    return (r == c).astype(BF16)


def _sigmoid(x):
    return 1.0 / (1.0 + jnp.exp(-x))


def _ada_kernel(c_ref, w_ref, b_ref, o_ref):
    c = c_ref[...]
    o_ref[...] = _dot3(c * _sigmoid(c), w_ref[...]) + b_ref[...]


def _adaln(c, w_ada, b_ada):
    bsz, d = c.shape
    n = w_ada.shape[1]
    nb = 1024
    return pl.pallas_call(
        _ada_kernel,
        grid=(n // nb,),
        in_specs=[pl.BlockSpec((bsz, d), lambda j: (0, 0)),
                  pl.BlockSpec((d, nb), lambda j: (0, j)),
                  pl.BlockSpec((1, nb), lambda j: (0, j))],
        out_specs=pl.BlockSpec((bsz, nb), lambda j: (0, j)),
        out_shape=jax.ShapeDtypeStruct((bsz, n), F32),
        compiler_params=_cparams("arbitrary"),
        name="adaln_mod",
    )(c, w_ada, b_ada.reshape(1, n))


def _modnorm(x, g, mod, d, which):
    ms = jnp.mean(x * x, axis=-1, keepdims=True)
    y = x * lax.rsqrt(ms + RMS_EPS) * g
    sh = mod[:, (3 * which) * d:(3 * which + 1) * d]
    sc = mod[:, (3 * which + 1) * d:(3 * which + 2) * d]
    return y * (1.0 + sc) + sh


def _inproj_kernel(x_ref, mod_ref, g_ref, w_ref, o_ref):
    d = x_ref.shape[1]
    xn = _modnorm(x_ref[...], g_ref[...], mod_ref[0], d, 0)
    o_ref[...] = jnp.dot(xn.astype(BF16), w_ref[...], preferred_element_type=F32)


def _inproj(x2, mod3, g, w_bf16, tps):
    t, d = x2.shape
    n = w_bf16.shape[1]
    tm = TOKEN_TILE
    return pl.pallas_call(
        _inproj_kernel,
        grid=(t // tm,),
        in_specs=[pl.BlockSpec((tm, d), lambda i: (i, 0)),
                  pl.BlockSpec((1, 1, mod3.shape[2]), lambda i: (i // tps, 0, 0)),
                  pl.BlockSpec((1, d), lambda i: (0, 0)),
                  pl.BlockSpec((d, n), lambda i: (0, 0))],
        out_specs=pl.BlockSpec((tm, n), lambda i: (i, 0)),
        out_shape=jax.ShapeDtypeStruct((t, n), F32),
        compiler_params=_cparams("parallel"),
        name="in_proj",
    )(x2, mod3, g, w_bf16)


def _qkprep_kernel(qkv_ref, pos_ref, gq_ref, gk_ref, inv_ref, qo_ref, ko_ref, vo_ref, km_ref):
    w = WIDTH
    ones = _head_ones(w)
    ang = pos_ref[...].astype(F32) * inv_ref[...]
    cos = jnp.tile(jnp.cos(ang), (1, w // 128))
    sin = jnp.tile(jnp.sin(ang), (1, w // 128))
    dd = lax.broadcasted_iota(jnp.int32, (1, w), 1) & (HEAD_DIM - 1)
    half = ROPE_DIM // 2
    s_lo = jnp.where(dd < half, -1.0, 0.0) * sin
    s_hi = jnp.where((dd >= half) & (dd < ROPE_DIM), 1.0, 0.0) * sin

    def norm_rope(xh, g):
        ssq = _dot2(xh * xh, ones)
        y = xh * lax.rsqrt(ssq * (1.0 / HEAD_DIM) + RMS_EPS) * g
        return y * cos + pltpu.roll(y, w - half, 1) * s_lo + pltpu.roll(y, half, 1) * s_hi

    q = norm_rope(qkv_ref[:, 0:w], gq_ref[...])
    k = norm_rope(qkv_ref[:, w:2 * w], gk_ref[...])
    qo_ref[...] = (q * (HEAD_DIM ** -0.5)).astype(BF16)
    ko_ref[...] = k.astype(BF16)
    vo_ref[...] = qkv_ref[:, 2 * w:3 * w].astype(BF16)
    km_ref[0] = jnp.mean(k, axis=0, keepdims=True)


def _qkprep(qkv, pos2, gq, gk, inv128):
    t = qkv.shape[0]
    tm = MOBA_BLOCK
    w = WIDTH
    row = lambda i: (i, 0)
    fixed = lambda i: (0, 0)
    return pl.pallas_call(
        _qkprep_kernel,
        grid=(t // tm,),
        in_specs=[pl.BlockSpec((tm, 3 * w), row), pl.BlockSpec((tm, 1), row),
                  pl.BlockSpec((1, w), fixed), pl.BlockSpec((1, w), fixed), pl.BlockSpec((1, 128), fixed)],
        out_specs=[pl.BlockSpec((tm, w), row), pl.BlockSpec((tm, w), row), pl.BlockSpec((tm, w), row),
                   pl.BlockSpec((1, 1, w), lambda i: (i, 0, 0))],
        out_shape=[jax.ShapeDtypeStruct((t, w), BF16), jax.ShapeDtypeStruct((t, w), BF16),
                   jax.ShapeDtypeStruct((t, w), BF16), jax.ShapeDtypeStruct((t // tm, 1, w), F32)],
        compiler_params=_cparams("parallel"),
        name="moba_qk_prep",
    )(qkv, pos2, gq, gk, inv128)


def _moba_kernel(q_ref, k_ref, v_ref, km_ref, o_ref):
    blk = MOBA_BLOCK
    i = pl.program_id(2)
    q = q_ref[0]
    km = km_ref[0]
    nb = km.shape[0]
    lane = lax.broadcasted_iota(jnp.int32, (blk, 128), 1)
    col = lax.broadcasted_iota(jnp.int32, (blk, nb), 1)
    qpos = lax.broadcasted_iota(jnp.int32, (blk, blk), 0)
    kpos = lax.broadcasted_iota(jnp.int32, (blk, blk), 1)
    row0 = pl.multiple_of(i * blk, blk)
    k_own = k_ref[0, pl.ds(row0, blk), :]
    v_own = v_ref[0, pl.ds(row0, blk), :]

    qs, biases, carry0 = [], [], []
    for hh in range(2):
        qh = jnp.where((lane >> 6) == hh, q, jnp.zeros_like(q))
        g = jnp.where(col < i, _dot_t(qh, km), -jnp.inf)
        sel = jnp.zeros((blk, nb), jnp.bool_)
        for _ in range(MOBA_TOPK):
            m = jnp.max(g, axis=-1, keepdims=True)
            idx = jnp.min(jnp.where(g == m, col, nb), axis=-1, keepdims=True)
            pick = (col == idx) & (g > -jnp.inf)
            sel = sel | pick
            g = jnp.where(pick, -jnp.inf, g)
        biases.append(jnp.where(sel, 0.0, NEG_INF))
        s = jnp.where(kpos <= qpos, _dot_t(qh, k_own), NEG_INF)
        m0 = jnp.max(s, axis=-1, keepdims=True)
        p = jnp.exp(s - m0)
        carry0 += [m0, jnp.sum(p, axis=-1, keepdims=True), _dot(p, v_own)]
        qs.append(qh)

    def body(jj, carry):
        r0 = pl.multiple_of(jj * (2 * blk), 2 * blk)
        kj = k_ref[0, pl.ds(r0, 2 * blk), :]
        vj = v_ref[0, pl.ds(r0, 2 * blk), :]
        hu = [(hh, u) for hh in range(2) for u in range(2)]
        raw = [_dot_t(qs[hh], kj[u * blk:(u + 1) * blk]) for hh, u in hu]
        bj = [jnp.sum(jnp.where(col == 2 * jj + u, biases[hh], 0.0), axis=-1, keepdims=True) for hh, u in hu]
        ss = [a + b for a, b in zip(raw, bj)]
        mx = [jnp.max(a, axis=-1, keepdims=True) for a in ss]
        m_new = [jnp.maximum(carry[3 * hh], jnp.maximum(mx[2 * hh], mx[2 * hh + 1])) for hh in range(2)]
        alpha = [jnp.exp(carry[3 * hh] - m_new[hh]) for hh in range(2)]
        ps = [jnp.exp(a - m_new[hh]) for a, (hh, u) in zip(ss, hu)]
        pcat = [jnp.concatenate([ps[2 * hh], ps[2 * hh + 1]], axis=1) for hh in range(2)]
        lsum = [jnp.sum(pc, axis=-1, keepdims=True) for pc in pcat]
        pv = [_dot(pc, vj) for pc in pcat]
        out = []
        for hh in range(2):
            out += [m_new[hh], alpha[hh] * carry[3 * hh + 1] + lsum[hh], alpha[hh] * carry[3 * hh + 2] + pv[hh]]
        return tuple(out)

    fin = lax.fori_loop(0, (i + 1) // 2, body, tuple(carry0))
    o0 = fin[2] / fin[1]
    o1 = fin[5] / fin[4]
    o_ref[0] = jnp.where(lane < HEAD_DIM, o0, o1).astype(o_ref.dtype)


def _moba(qb, kb, vb, km, bsz, seq):
    w = WIDTH
    nb = seq // MOBA_BLOCK
    q3 = qb.reshape(bsz, seq, w)
    k3 = kb.reshape(bsz, seq, w)
    v3 = vb.reshape(bsz, seq, w)
    km3 = km.reshape(bsz, nb, w)
    return pl.pallas_call(
        _moba_kernel,
        grid=(bsz, w // 128, nb),
        in_specs=[pl.BlockSpec((1, MOBA_BLOCK, 128), lambda b, p, i: (b, i, p)),
                  pl.BlockSpec((1, seq, 128), lambda b, p, i: (b, 0, p)),
                  pl.BlockSpec((1, seq, 128), lambda b, p, i: (b, 0, p)),
                  pl.BlockSpec((1, nb, 128), lambda b, p, i: (b, 0, p))],
        out_specs=pl.BlockSpec((1, MOBA_BLOCK, 128), lambda b, p, i: (b, i, p)),
        out_shape=jax.ShapeDtypeStruct((bsz, seq, w), BF16),
        compiler_params=_cparams("parallel", "parallel", "arbitrary"),
        name="moba_attention",
    )(q3, k3, v3, km3)


def _rwkvprep_kernel(z_ref, zp_ref, mu_ref, w0_ref, w2_ref, a0_ref, a2_ref, g2_ref, kk_ref, ka_ref, rk_ref,
                     r_o, lw_o, k_o, v_o, kn_o, b_o, bonus_o, g_o, *, tps):
    i = pl.program_id(0)
    w = WIDTH
    z = z_ref[...]
    tm = z.shape[0]
    prev_last = jnp.where((i % tps) == 0, 0.0, zp_ref[7:8, :])
    rows = lax.broadcasted_iota(jnp.int32, (tm, 1), 0)
    prev = jnp.where(rows == 0, prev_last, pltpu.roll(z, 1, 0))
    zs = z + (prev - z) * mu_ref[...]
    zr, zk, zv = zs[:, 0:w], zs[:, w:2 * w], zs[:, 2 * w:3 * w]
    zwa = zs[:, 3 * w:3 * w + 128]
    zg = zs[:, 3 * w + 128:3 * w + 256]
    nw = -(w0_ref[...] + _dot3(jnp.tanh(zwa), w2_ref[...]))
    softplus = jnp.maximum(nw, 0.0) + jnp.log(1.0 + jnp.exp(-jnp.abs(nw)))
    lw_o[...] = -jnp.exp(-softplus - 0.5)
    a = _sigmoid(a0_ref[...] + _dot3(zwa, a2_ref[...]))
    g_o[...] = _dot3(_sigmoid(zg), g2_ref[...])
    ones = _head_ones(w)
    kn = zk * kk_ref[...]
    kn = kn / jnp.maximum(jnp.sqrt(_dot2(kn * kn, ones)), 1e-12)
    k = zk * (1.0 + (a - 1.0) * ka_ref[...])
    r_o[...] = zr
    k_o[...] = k
    v_o[...] = zv
    kn_o[...] = kn
    b_o[...] = kn * a
    bonus_o[...] = _dot2(zr * k * rk_ref[...], ones) * zv


def _rwkvprep(zrw, tps, mu, w0, w2aug, a0, a2aug, g2, k_k, k_a, r_k):
    t = zrw.shape[0]
    tm = TOKEN_TILE
    w = WIDTH
    row = lambda i: (i, 0)
    fixed = lambda i: (0, 0)
    vec = pl.BlockSpec((1, w), fixed)
    outs = [jax.ShapeDtypeStruct((t, w), F32)] * 8
    return pl.pallas_call(
        functools.partial(_rwkvprep_kernel, tps=tps),
        grid=(t // tm,),
        in_specs=[pl.BlockSpec((tm, SHIFT_WIDTH), row),
                  pl.BlockSpec((8, SHIFT_WIDTH), lambda i: (jnp.maximum(i * (tm // 8) - 1, 0), 0)),
                  pl.BlockSpec((1, SHIFT_WIDTH), fixed),
                  vec, pl.BlockSpec((128, w), fixed), vec, pl.BlockSpec((128, w), fixed),
                  pl.BlockSpec((LORA_G, w), fixed), vec, vec, vec],
        out_specs=[pl.BlockSpec((tm, w), row)] * 8,
        out_shape=outs,
        compiler_params=_cparams("parallel"),
        name="rwkv_prep",
    )(zrw, zrw, mu, w0, w2aug, a0, a2aug, g2, k_k, k_a, r_k)


RWKV_CHUNK = 32


SCAN_GROUP = 4
SCAN_CHUNKS_PER_STEP = 4


def _scan_chunks(probs):
    cl, gw = probs[0][0].shape
    hc = SCAN_GROUP * cl
    bdot = lambda a, b: jnp.dot(a, b, preferred_element_type=F32)
    tn = lambda a, b: lax.dot_general(a, b, (((0,), (0,)), ((), ())), preferred_element_type=F32)
    tri = (lax.broadcasted_iota(jnp.int32, (cl, cl), 0) >= lax.broadcasted_iota(jnp.int32, (cl, cl), 1)).astype(BF16)
    hmask = (lax.broadcasted_iota(jnp.int32, (hc, gw), 0) // cl) == (lax.broadcasted_iota(jnp.int32, (hc, gw), 1) >> 6)
    stack = lambda a: jnp.where(hmask, jnp.tile(a, (SCAN_GROUP, 1)), 0.0).astype(BF16)
    unstack = lambda a: functools.reduce(lambda p, q: p + q, [a[h * cl:(h + 1) * cl] for h in range(SCAN_GROUP)])
    ri = lax.broadcasted_iota(jnp.int32, (hc, hc), 0)
    ci = lax.broadcasted_iota(jnp.int32, (hc, hc), 1)
    strict = ri > ci
    incl = ri >= ci
    eye = jnp.where(ri == ci, 1.0, 0.0)
    di = lax.broadcasted_iota(jnp.int32, (gw, gw), 0) == lax.broadcasted_iota(jnp.int32, (gw, gw), 1)

    cs2 = [bdot(tri, jnp.concatenate(_split(p[1]), axis=1)) for p in probs]
    ops = []
    for (r, lw, k, v, kn, b), c2 in zip(probs, cs2):
        cs = c2[:, :gw] + c2[:, gw:]
        cs_end = cs[cl - 1:cl, :]
        inv = jnp.exp(-cs)
        dec_end = jnp.exp(cs_end - cs)
        ops.append(dict(khs=stack(kn * jnp.exp(cs - lw)),
                        rhs=stack(r * jnp.exp(cs)),
                        bts=stack(b * inv), kts=stack(k * inv), vs=stack(v),
                        btes=stack(b * dec_end), ktes=stack(k * dec_end), gend=jnp.exp(cs_end)))
    grams = [lax.dot_general(jnp.concatenate([o['khs'], o['rhs']], axis=0),
                             jnp.concatenate([o['bts'], o['kts']], axis=0),
                             (((1,), (1,)), ((), ())), preferred_element_type=F32) for o in ops]
    for o, g in zip(ops, grams):
        lb = jnp.where(strict, g[:hc, :hc], 0.0)
        o['lkak'] = jnp.concatenate([jnp.where(strict, g[:hc, hc:], 0.0),
                                     jnp.where(incl, g[hc:, hc:], 0.0)], axis=0).astype(BF16)
        o['ab'] = jnp.where(incl, g[hc:, :hc], 0.0).astype(BF16)
        o['tinv'] = eye - lb
        o['pw'] = (-lb).astype(BF16)
    sq = [bdot(o['pw'], o['pw']) for o in ops]
    for o, q in zip(ops, sq):
        o['pw'] = q.astype(BF16)
    levels = int(np.log2(cl)) - 1
    for lvl in range(levels):
        both = [bdot(jnp.concatenate([o['pw'], o['tinv'].astype(BF16)], axis=0), o['pw']) for o in ops]
        for o, bo in zip(ops, both):
            o['tinv'] = o['tinv'] + bo[hc:]
            o['pw'] = bo[:hc].astype(BF16)
    lkak = [bdot(o['lkak'], o['vs']) for o in ops]
    wu = [bdot(o['tinv'].astype(BF16), jnp.concatenate([o['khs'], lv[:hc].astype(BF16)], axis=1))
          for o, lv in zip(ops, lkak)]
    wub = [jnp.concatenate([w[:, :gw], -w[:, gw:]], axis=1).astype(BF16) for w in wu]
    abw = [bdot(o['ab'], w) for o, w in zip(ops, wub)]
    gmat = [tn(o['btes'], w[:, :gw]) for o, w in zip(ops, wub)]
    hv = [tn(jnp.concatenate([o['btes'], o['ktes']], axis=0), jnp.concatenate([w[:, gw:], o['vs']], axis=0))
          for o, w in zip(ops, wub)]
    out = []
    for o, aw, lv, gm, h_v in zip(ops, abw, lkak, gmat, hv):
        qt = unstack(o['rhs'].astype(F32) - aw[:, :gw])
        yv = unstack(aw[:, gw:] + lv[hc:])
        out.append((qt, yv, jnp.where(di, o['gend'], 0.0) - gm, h_v))
    return out


def _scan_kernel(r_ref, lw_ref, k_ref, v_ref, kn_ref, b_ref, bonus_ref, g_ref, lng_ref, lnb_ref, o_ref, h_ref):
    gw = SCAN_GROUP * HEAD_DIM
    cl = RWKV_CHUNK
    ngrp = N_HEADS // SCAN_GROUP

    @pl.when(pl.program_id(1) == 0)
    def _():
        h_ref[...] = jnp.zeros_like(h_ref)

    where = [(grp, c) for grp in range(ngrp) for c in range(SCAN_CHUNKS_PER_STEP)]
    sl = lambda grp, c: (slice(c * cl, (c + 1) * cl), slice(grp * gw, (grp + 1) * gw))
    parts = _scan_chunks([tuple(ref[sl(grp, c)] for ref in (r_ref, lw_ref, k_ref, v_ref, kn_ref, b_ref))
                          for grp, c in where])
    ones = _head_ones(gw)
    bdot = lambda a, b: jnp.dot(a, b, preferred_element_type=F32)
    for grp in range(ngrp):
        h = h_ref[grp]
        for c in range(SCAN_CHUNKS_PER_STEP):
            rows, ln = sl(grp, c)
            qt, yv, abar, h_v = parts[where.index((grp, c))]
            hh, hl = _split(h)
            y = bdot(qt.astype(BF16), hh) + yv
            ah, al = _split(abar)
            ahl = bdot(jnp.concatenate([ah, al], axis=0), hh)
            h = ahl[:gw] + ahl[gw:] + bdot(ah, hl) + h_v
            mean = _dot2(y, ones) * (1.0 / HEAD_DIM)
            yc = y - mean
            var = _dot2(yc * yc, ones) * (1.0 / HEAD_DIM)
            yn = yc * lax.rsqrt(var + GN_EPS) * lng_ref[:, ln] + lnb_ref[:, ln] + bonus_ref[rows, ln]
            o_ref[rows, ln] = (yn * g_ref[rows, ln]).astype(o_ref.dtype)
        h_ref[grp] = h


def _rwkv_scan(r, lw, k, v, kn, b, bonus, g, ln_g, ln_b, bsz, seq):
    t, w = r.shape
    cl = RWKV_CHUNK * SCAN_CHUNKS_PER_STEP
    cps = seq // cl
    row = pl.BlockSpec((cl, w), lambda bi, c: (bi * cps + c, 0))
    vec = pl.BlockSpec((1, w), lambda bi, c: (0, 0))
    return pl.pallas_call(
        _scan_kernel,
        grid=(bsz, cps),
        in_specs=[row] * 8 + [vec, vec],
        out_specs=row,
        out_shape=jax.ShapeDtypeStruct((t, w), BF16),
        scratch_shapes=[pltpu.VMEM((N_HEADS // SCAN_GROUP, SCAN_GROUP * HEAD_DIM, SCAN_GROUP * HEAD_DIM), F32)],
        compiler_params=_cparams("parallel", "arbitrary"),
        name="rwkv_scan",
    )(r, lw, k, v, kn, b, bonus, g, ln_g, ln_b)


def _rwkv_branch(zrw, p, l, bsz, seq):
    w = WIDTH
    vec = lambda a: a.reshape(1, -1)
    zpad = lambda a, top: jnp.concatenate([a, jnp.zeros_like(a)] if top else [jnp.zeros_like(a), a], axis=0)
    outs = _rwkvprep(zrw, seq // TOKEN_TILE, vec(p['rwkv_mu'][l]), vec(p['rwkv_w0'][l]),
                     zpad(p['rwkv_w2'][l], True), vec(p['rwkv_a0'][l]), zpad(p['rwkv_a2'][l], False),
                     p['rwkv_g2'][l], vec(p['rwkv_k_k'][l]), vec(p['rwkv_k_a'][l]), vec(p['rwkv_r_k'][l]))
    return _rwkv_scan(*outs, vec(p['rwkv_ln_g'][l]), vec(p['rwkv_ln_b'][l]), bsz, seq)


def _merge_kernel(x_ref, om_ref, or_ref, gl_ref, mod_ref, g2_ref, wpm_ref, wpr_ref, wo_ref, wq_ref,
                  h_o, xn_o, q_o):
    d = x_ref.shape[1]
    mod = mod_ref[0]
    pm = jnp.dot(om_ref[...], wpm_ref[...], preferred_element_type=F32)
    pr = jnp.dot(or_ref[...], wpr_ref[...], preferred_element_type=F32)
    gl = gl_ref[...]
    mixed = _sigmoid(gl[:, :d]) * pm + _sigmoid(gl[:, d:]) * pr
    h = x_ref[...] + mod[:, 2 * d:3 * d] * _dot(mixed, wo_ref[...])
    h_o[...] = h
    xn2 = _modnorm(h, g2_ref[...], mod, d, 1)
    xn_o[...] = xn2
    q_o[...] = _dot(xn2, wq_ref[...])


def _merge(x2, o_moba, o_rwkv, gl, mod3, g2, wpm, wpr, wo, wq, tps):
    t, d = x2.shape
    tm = TOKEN_TILE
    w = WIDTH
    nq = wq.shape[1]
    row = lambda i: (i, 0)
    fixed = lambda i: (0, 0)
    return pl.pallas_call(
        _merge_kernel,
        grid=(t // tm,),
        in_specs=[pl.BlockSpec((tm, d), row), pl.BlockSpec((tm, w), row), pl.BlockSpec((tm, w), row),
                  pl.BlockSpec((tm, 2 * d), row),
                  pl.BlockSpec((1, 1, mod3.shape[2]), lambda i: (i // tps, 0, 0)),
                  pl.BlockSpec((1, d), fixed), pl.BlockSpec((w, d), fixed), pl.BlockSpec((w, d), fixed),
                  pl.BlockSpec((d, d), fixed), pl.BlockSpec((d, nq), fixed)],
        out_specs=[pl.BlockSpec((tm, d), row), pl.BlockSpec((tm, d), row), pl.BlockSpec((tm, nq), row)],
        out_shape=[jax.ShapeDtypeStruct((t, d), F32), jax.ShapeDtypeStruct((t, d), F32),
                   jax.ShapeDtypeStruct((t, nq), F32)],
        compiler_params=_cparams("parallel"),
        name="merge_proj",
    )(x2, o_moba, o_rwkv, gl, mod3, g2, wpm, wpr, wo, wq)


PEER_TILE = 128


def _top16_rows(problems):
    tm = problems[0][0].shape[1]
    rank = lax.broadcasted_iota(jnp.int32, (PEER_TOPK, tm), 0)
    keys = [k for _, k in problems]
    big = jnp.int32(1 << 30)

    def rnd(r, carry):
        out = []
        hit = rank == r
        for (s, vals, idxs), key in zip(carry, keys):
            m = jnp.max(s, axis=0, keepdims=True)
            ix = jnp.min(jnp.where(s == m, key, big), axis=0, keepdims=True)
            out.append((jnp.where(key == ix, -jnp.inf, s), jnp.where(hit, m, vals),
                        jnp.where(hit, ix.astype(F32), idxs)))
        return tuple(out)

    z = jnp.zeros((PEER_TOPK, tm), F32)
    fin = lax.fori_loop(0, PEER_TOPK, rnd, tuple((s, z, z) for s, _ in problems))
    return [(vals, idxs) for _, vals, idxs in fin]


def _peer_topk_kernel(q_ref, sk_ref, idx_o, gate_o):
    tm = q_ref.shape[0]
    kk = PEER_TOPK
    nk = PEER_NKEYS
    rank = lax.broadcasted_iota(jnp.int32, (kk, tm), 0)
    pos_key = lax.broadcasted_iota(jnp.int32, (nk, tm), 0)
    row8 = lax.broadcasted_iota(jnp.int32, (8, tm), 0)
    tdot = lambda a, b: lax.dot_general(a, b, (((1,), (1,)), ((), ())), preferred_element_type=F32)
    cand_key = jnp.concatenate([rank] + [row8 + a * kk for a in range(1, 8)] + [(row8 + 8) * kk], axis=0)

    tops = []
    for h in range(PEER_HEADS):
        probs = []
        for p in range(2):
            hp = 2 * h + p
            qh, ql = _split(q_ref[:, hp * PEER_HALF:(hp + 1) * PEER_HALF])
            kh, kl = _split(sk_ref[hp])
            probs.append((tdot(kh, qh) + tdot(kh, ql) + tdot(kl, qh), pos_key))
        tops.append(_top16_rows(probs))

    idx_rows, gate_rows = [], []
    for h0 in range(0, PEER_HEADS, 2):
        probs = []
        for h in (h0, h0 + 1):
            (ts0, _), (ts1, _) = tops[h]
            cand = jnp.concatenate([ts0[0:1] + ts1] + [ts0[a:a + 1] + ts1[0:8] for a in range(1, 8)]
                                   + [ts0[8:16] + ts1[0:1]], axis=0)
            probs.append((cand, cand_key))
        for h, (best, pos) in zip((h0, h0 + 1), _top16_rows(probs)):
            (_, ti0), (_, ti1) = tops[h]
            pos = pos.astype(jnp.int32)
            ids = jnp.zeros((kk, tm), F32)
            for r in range(kk):
                pr = pos[r:r + 1, :]
                i0 = jnp.sum(jnp.where(rank == (pr >> 4), ti0, 0.0), axis=0, keepdims=True)
                i1 = jnp.sum(jnp.where(rank == (pr & (kk - 1)), ti1, 0.0), axis=0, keepdims=True)
                ids = jnp.where(rank == r, i0 * float(nk) + i1, ids)
            e = jnp.exp(best - jnp.max(best, axis=0, keepdims=True))
            gate_rows.append(e / jnp.sum(e, axis=0, keepdims=True))
            idx_rows.append(ids)
    idx_o[...] = jnp.concatenate(idx_rows, axis=0).T.astype(jnp.int32)
    gate_o[...] = jnp.concatenate(gate_rows, axis=0).T


def _peer_topk(q, subkeys16):
    t, nq = q.shape
    tm = PEER_TILE
    return pl.pallas_call(
        _peer_topk_kernel,
        grid=(t // tm,),
        in_specs=[pl.BlockSpec((tm, nq), lambda i: (i, 0)),
                  pl.BlockSpec(subkeys16.shape, lambda i: (0, 0, 0))],
        out_specs=[pl.BlockSpec((tm, 128), lambda i: (i, 0)), pl.BlockSpec((tm, 128), lambda i: (i, 0))],
        out_shape=[jax.ShapeDtypeStruct((t, 128), jnp.int32), jax.ShapeDtypeStruct((t, 128), F32)],
        compiler_params=_cparams("parallel"),
        name="peer_topk",
    )(q, subkeys16)


GATHER_TOKENS = 16
N_PICK = PEER_HEADS * PEER_TOPK


def _peer_kernel(idx_cur, idx_nxt, gate_ref, xn_ref, h_ref, mod_ref, uv_hbm, o_ref, buf, sem):
    i = pl.program_id(0)
    n = pl.num_programs(0)
    slot = i % 2
    d = h_ref.shape[1]
    gt2 = mod_ref[0][:, 5 * d:6 * d]

    def row_copy(idx_ref, s, tok, e):
        return pltpu.make_async_copy(uv_hbm.at[idx_ref[tok, e]],
                                     buf.at[s * GATHER_TOKENS + tok, pl.ds(e, 1), :], sem.at[s, tok])

    def issue_token(idx_ref, s, tok):
        for e in range(N_PICK):
            row_copy(idx_ref, s, tok, e).start(priority=e % 2)

    @pl.when(i == 0)
    def _():
        lax.fori_loop(0, GATHER_TOKENS, lambda tok, c: (issue_token(idx_cur, 0, tok), c)[1], 0)

    def token(tok, c):
        for e in range(N_PICK):
            row_copy(idx_cur, slot, tok, e).wait()
        words = buf[slot * GATHER_TOKENS + tok]
        u = lax.bitcast_convert_type(words << 16, F32).astype(BF16)
        v = lax.bitcast_convert_type(words & jnp.uint32(0xFFFF0000), F32).astype(BF16)
        xt = xn_ref[pl.ds(tok, 1), :].astype(BF16)
        hd = lax.dot_general(xt, u, (((1,), (1,)), ((), ())), preferred_element_type=F32)
        hd = 0.5 * hd * (1.0 + lax.erf(hd * (2.0 ** -0.5))) * gate_ref[pl.ds(tok, 1), :]
        out = jnp.dot(hd.astype(BF16), v, preferred_element_type=F32)
        o_ref[pl.ds(tok, 1), :] = h_ref[pl.ds(tok, 1), :] + gt2 * out
        issue_token(idx_nxt, 1 - slot, tok)
        return c

    lax.fori_loop(0, GATHER_TOKENS, token, 0)

    @pl.when(i == n - 1)
    def _():
        def drain(tok, c):
            for e in range(N_PICK):
                row_copy(idx_nxt, 1 - slot, tok, e).wait()
            return c
        lax.fori_loop(0, GATHER_TOKENS, drain, 0)


def _peer(idx, gates, xn2, h1, mod3, uv_words, tokens_per_seq):
    t, d = h1.shape
    tt = GATHER_TOKENS
    nsteps = t // tt
    row = lambda i: (i, 0)
    return pl.pallas_call(
        _peer_kernel,
        grid=(nsteps,),
        in_specs=[pl.BlockSpec((tt, N_PICK), row, memory_space=pltpu.SMEM),
                  pl.BlockSpec((tt, N_PICK), lambda i: (jnp.minimum(i + 1, nsteps - 1), 0), memory_space=pltpu.SMEM),
                  pl.BlockSpec((tt, N_PICK), row), pl.BlockSpec((tt, d), row), pl.BlockSpec((tt, d), row),
                  pl.BlockSpec((1, 1, mod3.shape[2]), lambda i: ((i * tt) // tokens_per_seq, 0, 0)),
                  pl.BlockSpec(memory_space=pl.ANY)],
        out_specs=pl.BlockSpec((tt, d), row),
        out_shape=jax.ShapeDtypeStruct((t, d), F32),
        scratch_shapes=[pltpu.VMEM((2 * tt, N_PICK, d), jnp.uint32), pltpu.SemaphoreType.DMA((2, tt))],
        compiler_params=_cparams("arbitrary"),
        name="peer_experts",
    )(idx, idx, gates, xn2, h1, mod3, uv_words)


def _pack_uv(u, v):
    ub = lax.bitcast_convert_type(u.astype(BF16), jnp.uint16).astype(jnp.uint32)
    vb = lax.bitcast_convert_type(v.astype(BF16), jnp.uint16).astype(jnp.uint32)
    return (ub | (vb << 16)).reshape(u.shape[0], 1, u.shape[1])


def _rope_inv128():
    half = ROPE_DIM // 2
    inv = ROPE_THETA ** (-(jnp.arange(half, dtype=F32) * 2.0) / ROPE_DIM)
    lane = np.arange(128) % HEAD_DIM
    out = jnp.where(lane < ROPE_DIM, inv[lane % half], 0.0)
    return out.reshape(1, 128)


def kernel(x, c, positions, w_ada, b_ada, norm1_g, w_in, q_norm_g, k_norm_g, rwkv_mu, rwkv_w0, rwkv_w2, rwkv_a0, rwkv_a2, rwkv_g2, rwkv_k_k, rwkv_k_a, rwkv_r_k, rwkv_ln_g, rwkv_ln_b, w_proj_moba, w_proj_rwkv, w_out, norm2_g, peer_wq, peer_subkeys, peer_u, peer_v):
    bsz, seq, d = x.shape
    t = bsz * seq
    tps = seq // TOKEN_TILE
    l = 0
    x2 = x.reshape(t, d)
    mod3 = _adaln(c, w_ada[l], b_ada[l]).reshape(bsz, 1, 6 * d)
    w_in_b = w_in[l].astype(BF16)
    g1 = norm1_g[l].reshape(1, d)
    qkv = _inproj(x2, mod3, g1, w_in_b[:, :3 * WIDTH], tps)
    qb, kb, vb, km = _qkprep(qkv, positions.reshape(t, 1), jnp.tile(q_norm_g[l], N_HEADS).reshape(1, WIDTH),
                             jnp.tile(k_norm_g[l], N_HEADS).reshape(1, WIDTH), _rope_inv128())
    o_moba = _moba(qb, kb, vb, km, bsz, seq)
    params = dict(rwkv_mu=rwkv_mu, rwkv_w0=rwkv_w0, rwkv_w2=rwkv_w2, rwkv_a0=rwkv_a0, rwkv_a2=rwkv_a2,
                  rwkv_g2=rwkv_g2, rwkv_k_k=rwkv_k_k, rwkv_k_a=rwkv_k_a, rwkv_r_k=rwkv_r_k.reshape(1, WIDTH),
                  rwkv_ln_g=rwkv_ln_g, rwkv_ln_b=rwkv_ln_b)
    zrw = _inproj(x2, mod3, g1, w_in_b[:, 3 * WIDTH:3 * WIDTH + SHIFT_WIDTH], tps)
    o_rwkv = _rwkv_branch(zrw, params, l, bsz, seq)
    gl = _inproj(x2, mod3, g1, w_in_b[:, 3 * WIDTH + SHIFT_WIDTH:], tps)
    h1, xn2, q = _merge(x2, o_moba.reshape(t, WIDTH), o_rwkv, gl, mod3, norm2_g[l].reshape(1, d),
                        w_proj_moba[l].astype(BF16), w_proj_rwkv[l].astype(BF16), w_out[l].astype(BF16),
                        peer_wq[l].astype(BF16), tps)
    idx, gates = _peer_topk(q, peer_subkeys[l].reshape(2 * PEER_HEADS, PEER_NKEYS, PEER_HALF))
    out = _peer(idx, gates, xn2, h1, mod3, _pack_uv(peer_u[l], peer_v[l]), seq)
    return out.reshape(bsz, seq, d)
```

```python
import functools

import numpy as np
import jax
import jax.numpy as jnp
from jax import lax
from jax.experimental import pallas as pl
from jax.experimental.pallas import tpu as pltpu

F32 = jnp.float32
BF16 = jnp.bfloat16

HEAD_DIM = 64
N_HEADS = 8
WIDTH = N_HEADS * HEAD_DIM
MOBA_BLOCK = 256
MOBA_TOPK = 3
ROPE_THETA = 500000.0
ROPE_DIM = HEAD_DIM // 4
LORA_W = 64
LORA_A = 64
LORA_G = 128
SHIFT_WIDTH = 3 * WIDTH + LORA_W + LORA_A + LORA_G
GN_EPS = 64e-5
RMS_EPS = 1e-6
NEG_INF = -1e30
PEER_HEADS = 8
PEER_NKEYS = 128
PEER_HALF = 128
PEER_TOPK = 16

TOKEN_TILE = 256
V7X_VMEM_LIMIT = 48 * 1024 * 1024


def _cparams(*sem, flags=None):
    return pltpu.CompilerParams(dimension_semantics=sem, vmem_limit_bytes=V7X_VMEM_LIMIT, flags=flags)


def _dot(a, b):
    return jnp.dot(a.astype(BF16), b.astype(BF16), preferred_element_type=F32)


def _dot_t(a, b):
    return lax.dot_general(a.astype(BF16), b.astype(BF16), (((1,), (1,)), ((), ())),
                           preferred_element_type=F32)


def _split(a):
    hi = a.astype(BF16)
    lo = (a - hi.astype(F32)).astype(BF16)
    return hi, lo


def _split3(a):
    hi = a.astype(BF16)
    r1 = a - hi.astype(F32)
    mid = r1.astype(BF16)
    return hi, mid, (r1 - mid.astype(F32)).astype(BF16)


def _dot3(a, b):
    ah, al = _split(a)
    bh, bl = _split(b)
    return (jnp.dot(ah, bh, preferred_element_type=F32) + jnp.dot(ah, bl, preferred_element_type=F32)
            + jnp.dot(al, bh, preferred_element_type=F32))


def _dot2(a, b_exact):
    ah, al = _split(a)
    return jnp.dot(ah, b_exact, preferred_element_type=F32) + jnp.dot(al, b_exact, preferred_element_type=F32)


def _head_ones(n):
    r = lax.broadcasted_iota(jnp.int32, (n, n), 0) >> 6
    c = lax.broadcasted_iota(jnp.int32, (n, n), 1) >> 6
    return (r == c).astype(BF16)


def _sigmoid(x):
    return 1.0 / (1.0 + jnp.exp(-x))


def _ada_kernel(c_ref, w_ref, b_ref, o_ref):
    c = c_ref[...]
    o_ref[...] = _dot3(c * _sigmoid(c), w_ref[...]) + b_ref[...]


def _adaln(c, w_ada, b_ada):
    bsz, d = c.shape
    n = w_ada.shape[1]
    nb = 1024
    return pl.pallas_call(
        _ada_kernel,
        grid=(n // nb,),
        in_specs=[pl.BlockSpec((bsz, d), lambda j: (0, 0)),
                  pl.BlockSpec((d, nb), lambda j: (0, j)),
                  pl.BlockSpec((1, nb), lambda j: (0, j))],
        out_specs=pl.BlockSpec((bsz, nb), lambda j: (0, j)),
        out_shape=jax.ShapeDtypeStruct((bsz, n), F32),
        compiler_params=_cparams("arbitrary"),
        name="adaln_mod",
    )(c, w_ada, b_ada.reshape(1, n))


def _modnorm(x, g, mod, d, which):
    ms = jnp.mean(x * x, axis=-1, keepdims=True)
    y = x * lax.rsqrt(ms + RMS_EPS) * g
    sh = mod[:, (3 * which) * d:(3 * which + 1) * d]
    sc = mod[:, (3 * which + 1) * d:(3 * which + 2) * d]
    return y * (1.0 + sc) + sh


def _inproj_kernel(x_ref, mod_ref, g_ref, w_ref, o_ref):
    d = x_ref.shape[1]
    xn = _modnorm(x_ref[...], g_ref[...], mod_ref[0], d, 0)
    o_ref[...] = jnp.dot(xn.astype(BF16), w_ref[...], preferred_element_type=F32)


def _inproj(x2, mod3, g, w_bf16, tps):
    t, d = x2.shape
    n = w_bf16.shape[1]
    tm = TOKEN_TILE
    return pl.pallas_call(
        _inproj_kernel,
        grid=(t // tm,),
        in_specs=[pl.BlockSpec((tm, d), lambda i: (i, 0)),
                  pl.BlockSpec((1, 1, mod3.shape[2]), lambda i: (i // tps, 0, 0)),
                  pl.BlockSpec((1, d), lambda i: (0, 0)),
                  pl.BlockSpec((d, n), lambda i: (0, 0))],
        out_specs=pl.BlockSpec((tm, n), lambda i: (i, 0)),
        out_shape=jax.ShapeDtypeStruct((t, n), F32),
        compiler_params=_cparams("parallel"),
        name="in_proj",
    )(x2, mod3, g, w_bf16)


def _qkprep_kernel(qkv_ref, pos_ref, gq_ref, gk_ref, inv_ref, qo_ref, ko_ref, vo_ref, km_ref):
    w = WIDTH
    ones = _head_ones(w)
    ang = pos_ref[...].astype(F32) * inv_ref[...]
    cos = jnp.tile(jnp.cos(ang), (1, w // 128))
    sin = jnp.tile(jnp.sin(ang), (1, w // 128))
    dd = lax.broadcasted_iota(jnp.int32, (1, w), 1) & (HEAD_DIM - 1)
    half = ROPE_DIM // 2
    s_lo = jnp.where(dd < half, -1.0, 0.0) * sin
    s_hi = jnp.where((dd >= half) & (dd < ROPE_DIM), 1.0, 0.0) * sin

    def norm_rope(xh, g):
        ssq = _dot2(xh * xh, ones)
        y = xh * lax.rsqrt(ssq * (1.0 / HEAD_DIM) + RMS_EPS) * g
        return y * cos + pltpu.roll(y, w - half, 1) * s_lo + pltpu.roll(y, half, 1) * s_hi

    q = norm_rope(qkv_ref[:, 0:w], gq_ref[...])
    k = norm_rope(qkv_ref[:, w:2 * w], gk_ref[...])
    qo_ref[...] = (q * (HEAD_DIM ** -0.5)).astype(BF16)
    ko_ref[...] = k.astype(BF16)
    vo_ref[...] = qkv_ref[:, 2 * w:3 * w].astype(BF16)
    km_ref[0] = jnp.mean(k, axis=0, keepdims=True)


def _qkprep(qkv, pos2, gq, gk, inv128):
    t = qkv.shape[0]
    tm = MOBA_BLOCK
    w = WIDTH
    row = lambda i: (i, 0)
    fixed = lambda i: (0, 0)
    return pl.pallas_call(
        _qkprep_kernel,
        grid=(t // tm,),
        in_specs=[pl.BlockSpec((tm, 3 * w), row), pl.BlockSpec((tm, 1), row),
                  pl.BlockSpec((1, w), fixed), pl.BlockSpec((1, w), fixed), pl.BlockSpec((1, 128), fixed)],
        out_specs=[pl.BlockSpec((tm, w), row), pl.BlockSpec((tm, w), row), pl.BlockSpec((tm, w), row),
                   pl.BlockSpec((1, 1, w), lambda i: (i, 0, 0))],
        out_shape=[jax.ShapeDtypeStruct((t, w), BF16), jax.ShapeDtypeStruct((t, w), BF16),
                   jax.ShapeDtypeStruct((t, w), BF16), jax.ShapeDtypeStruct((t // tm, 1, w), F32)],
        compiler_params=_cparams("parallel"),
        name="moba_qk_prep",
    )(qkv, pos2, gq, gk, inv128)


def _moba_kernel(q_ref, k_ref, v_ref, km_ref, o_ref):
    blk = MOBA_BLOCK
    i = pl.program_id(2)
    q = q_ref[0]
    km = km_ref[0]
    nb = km.shape[0]
    lane = lax.broadcasted_iota(jnp.int32, (blk, 128), 1)
    col = lax.broadcasted_iota(jnp.int32, (blk, nb), 1)
    qpos = lax.broadcasted_iota(jnp.int32, (blk, blk), 0)
    kpos = lax.broadcasted_iota(jnp.int32, (blk, blk), 1)
    row0 = pl.multiple_of(i * blk, blk)
    k_own = k_ref[0, pl.ds(row0, blk), :]
    v_own = v_ref[0, pl.ds(row0, blk), :]

    qs, biases, carry0 = [], [], []
    for hh in range(2):
        qh = jnp.where((lane >> 6) == hh, q, jnp.zeros_like(q))
        g = jnp.where(col < i, _dot_t(qh, km), -jnp.inf)
        sel = jnp.zeros((blk, nb), jnp.bool_)
        for _ in range(MOBA_TOPK):
            m = jnp.max(g, axis=-1, keepdims=True)
            idx = jnp.min(jnp.where(g == m, col, nb), axis=-1, keepdims=True)
            pick = (col == idx) & (g > -jnp.inf)
            sel = sel | pick
            g = jnp.where(pick, -jnp.inf, g)
        biases.append(jnp.where(sel, 0.0, NEG_INF))
        s = jnp.where(kpos <= qpos, _dot_t(qh, k_own), NEG_INF)
        m0 = jnp.max(s, axis=-1, keepdims=True)
        p = jnp.exp(s - m0)
        carry0 += [m0, jnp.sum(p, axis=-1, keepdims=True), _dot(p, v_own)]
        qs.append(qh)

    def body(jj, carry):
        r0 = pl.multiple_of(jj * (2 * blk), 2 * blk)
        kj = k_ref[0, pl.ds(r0, 2 * blk), :]
        vj = v_ref[0, pl.ds(r0, 2 * blk), :]
        hu = [(hh, u) for hh in range(2) for u in range(2)]
        raw = [_dot_t(qs[hh], kj[u * blk:(u + 1) * blk]) for hh, u in hu]
        bj = [jnp.sum(jnp.where(col == 2 * jj + u, biases[hh], 0.0), axis=-1, keepdims=True) for hh, u in hu]
        ss = [a + b for a, b in zip(raw, bj)]
        mx = [jnp.max(a, axis=-1, keepdims=True) for a in ss]
        m_new = [jnp.maximum(carry[3 * hh], jnp.maximum(mx[2 * hh], mx[2 * hh + 1])) for hh in range(2)]
        alpha = [jnp.exp(carry[3 * hh] - m_new[hh]) for hh in range(2)]
        ps = [jnp.exp(a - m_new[hh]) for a, (hh, u) in zip(ss, hu)]
        pcat = [jnp.concatenate([ps[2 * hh], ps[2 * hh + 1]], axis=1) for hh in range(2)]
        lsum = [jnp.sum(pc, axis=-1, keepdims=True) for pc in pcat]
        pv = [_dot(pc, vj) for pc in pcat]
        out = []
        for hh in range(2):
            out += [m_new[hh], alpha[hh] * carry[3 * hh + 1] + lsum[hh], alpha[hh] * carry[3 * hh + 2] + pv[hh]]
        return tuple(out)

    fin = lax.fori_loop(0, (i + 1) // 2, body, tuple(carry0))
    o0 = fin[2] / fin[1]
    o1 = fin[5] / fin[4]
    o_ref[0] = jnp.where(lane < HEAD_DIM, o0, o1).astype(o_ref.dtype)


def _moba(qb, kb, vb, km, bsz, seq):
    w = WIDTH
    nb = seq // MOBA_BLOCK
    q3 = qb.reshape(bsz, seq, w)
    k3 = kb.reshape(bsz, seq, w)
    v3 = vb.reshape(bsz, seq, w)
    km3 = km.reshape(bsz, nb, w)
    return pl.pallas_call(
        _moba_kernel,
        grid=(bsz, w // 128, nb),
        in_specs=[pl.BlockSpec((1, MOBA_BLOCK, 128), lambda b, p, i: (b, i, p)),
                  pl.BlockSpec((1, seq, 128), lambda b, p, i: (b, 0, p)),
                  pl.BlockSpec((1, seq, 128), lambda b, p, i: (b, 0, p)),
                  pl.BlockSpec((1, nb, 128), lambda b, p, i: (b, 0, p))],
        out_specs=pl.BlockSpec((1, MOBA_BLOCK, 128), lambda b, p, i: (b, i, p)),
        out_shape=jax.ShapeDtypeStruct((bsz, seq, w), BF16),
        compiler_params=_cparams("parallel", "parallel", "arbitrary"),
        name="moba_attention",
    )(q3, k3, v3, km3)


def _rwkvprep_kernel(z_ref, zp_ref, mu_ref, w0_ref, w2_ref, a0_ref, a2_ref, g2_ref, kk_ref, ka_ref, rk_ref,
                     r_o, lw_o, k_o, v_o, kn_o, b_o, bonus_o, g_o, *, tps):
    i = pl.program_id(0)
    w = WIDTH
    z = z_ref[...]
    tm = z.shape[0]
    prev_last = jnp.where((i % tps) == 0, 0.0, zp_ref[7:8, :])
    rows = lax.broadcasted_iota(jnp.int32, (tm, 1), 0)
    prev = jnp.where(rows == 0, prev_last, pltpu.roll(z, 1, 0))
    zs = z + (prev - z) * mu_ref[...]
    zr, zk, zv = zs[:, 0:w], zs[:, w:2 * w], zs[:, 2 * w:3 * w]
    zwa = zs[:, 3 * w:3 * w + 128]
    zg = zs[:, 3 * w + 128:3 * w + 256]
    nw = -(w0_ref[...] + _dot3(jnp.tanh(zwa), w2_ref[...]))
    softplus = jnp.maximum(nw, 0.0) + jnp.log(1.0 + jnp.exp(-jnp.abs(nw)))
    lw_o[...] = -jnp.exp(-softplus - 0.5)
    a = _sigmoid(a0_ref[...] + _dot3(zwa, a2_ref[...]))
    g_o[...] = _dot3(_sigmoid(zg), g2_ref[...])
    ones = _head_ones(w)
    kn = zk * kk_ref[...]
    kn = kn / jnp.maximum(jnp.sqrt(_dot2(kn * kn, ones)), 1e-12)
    k = zk * (1.0 + (a - 1.0) * ka_ref[...])
    r_o[...] = zr
    k_o[...] = k
    v_o[...] = zv
    kn_o[...] = kn
    b_o[...] = kn * a
    bonus_o[...] = _dot2(zr * k * rk_ref[...], ones) * zv


def _rwkvprep(zrw, tps, mu, w0, w2aug, a0, a2aug, g2, k_k, k_a, r_k):
    t = zrw.shape[0]
    tm = TOKEN_TILE
    w = WIDTH
    row = lambda i: (i, 0)
    fixed = lambda i: (0, 0)
    vec = pl.BlockSpec((1, w), fixed)
    outs = [jax.ShapeDtypeStruct((t, w), F32)] * 8
    return pl.pallas_call(
        functools.partial(_rwkvprep_kernel, tps=tps),
        grid=(t // tm,),
        in_specs=[pl.BlockSpec((tm, SHIFT_WIDTH), row),
                  pl.BlockSpec((8, SHIFT_WIDTH), lambda i: (jnp.maximum(i * (tm // 8) - 1, 0), 0)),
                  pl.BlockSpec((1, SHIFT_WIDTH), fixed),
                  vec, pl.BlockSpec((128, w), fixed), vec, pl.BlockSpec((128, w), fixed),
                  pl.BlockSpec((LORA_G, w), fixed), vec, vec, vec],
        out_specs=[pl.BlockSpec((tm, w), row)] * 8,
        out_shape=outs,
        compiler_params=_cparams("parallel"),
        name="rwkv_prep",
    )(zrw, zrw, mu, w0, w2aug, a0, a2aug, g2, k_k, k_a, r_k)


RWKV_CHUNK = 32


SCAN_GROUP = 4
SCAN_CHUNKS_PER_STEP = 4


def _scan_chunks(probs):
    cl, gw = probs[0][0].shape
    hc = SCAN_GROUP * cl
    bdot = lambda a, b: jnp.dot(a, b, preferred_element_type=F32)
    tn = lambda a, b: lax.dot_general(a, b, (((0,), (0,)), ((), ())), preferred_element_type=F32)
    tri = (lax.broadcasted_iota(jnp.int32, (cl, cl), 0) >= lax.broadcasted_iota(jnp.int32, (cl, cl), 1)).astype(BF16)
    hmask = (lax.broadcasted_iota(jnp.int32, (hc, gw), 0) // cl) == (lax.broadcasted_iota(jnp.int32, (hc, gw), 1) >> 6)
    stack = lambda a: jnp.where(hmask, jnp.tile(a, (SCAN_GROUP, 1)), 0.0).astype(BF16)
    unstack = lambda a: functools.reduce(lambda p, q: p + q, [a[h * cl:(h + 1) * cl] for h in range(SCAN_GROUP)])
    ri = lax.broadcasted_iota(jnp.int32, (hc, hc), 0)
    ci = lax.broadcasted_iota(jnp.int32, (hc, hc), 1)
    strict = ri > ci
    incl = ri >= ci
    eye = jnp.where(ri == ci, 1.0, 0.0)
    di = lax.broadcasted_iota(jnp.int32, (gw, gw), 0) == lax.broadcasted_iota(jnp.int32, (gw, gw), 1)

    cs2 = [bdot(tri, jnp.concatenate(_split(p[1]), axis=1)) for p in probs]
    ops = []
    for (r, lw, k, v, kn, b), c2 in zip(probs, cs2):
        cs = c2[:, :gw] + c2[:, gw:]
        cs_end = cs[cl - 1:cl, :]
        inv = jnp.exp(-cs)
        dec_end = jnp.exp(cs_end - cs)
        ops.append(dict(khs=stack(kn * jnp.exp(cs - lw)),
                        rhs=stack(r * jnp.exp(cs)),
                        bts=stack(b * inv), kts=stack(k * inv), vs=stack(v),
                        btes=stack(b * dec_end), ktes=stack(k * dec_end), gend=jnp.exp(cs_end)))
    grams = [lax.dot_general(jnp.concatenate([o['khs'], o['rhs']], axis=0),
                             jnp.concatenate([o['bts'], o['kts']], axis=0),
                             (((1,), (1,)), ((), ())), preferred_element_type=F32) for o in ops]
    for o, g in zip(ops, grams):
        lb = jnp.where(strict, g[:hc, :hc], 0.0)
        o['lkak'] = jnp.concatenate([jnp.where(strict, g[:hc, hc:], 0.0),
                                     jnp.where(incl, g[hc:, hc:], 0.0)], axis=0).astype(BF16)
        o['ab'] = jnp.where(incl, g[hc:, :hc], 0.0).astype(BF16)
        o['tinv'] = eye - lb
        o['pw'] = (-lb).astype(BF16)
    sq = [bdot(o['pw'], o['pw']) for o in ops]
    for o, q in zip(ops, sq):
        o['pw'] = q.astype(BF16)
    levels = int(np.log2(cl)) - 1
    for lvl in range(levels):
        both = [bdot(jnp.concatenate([o['pw'], o['tinv'].astype(BF16)], axis=0), o['pw']) for o in ops]
        for o, bo in zip(ops, both):
            o['tinv'] = o['tinv'] + bo[hc:]
            o['pw'] = bo[:hc].astype(BF16)
    lkak = [bdot(o['lkak'], o['vs']) for o in ops]
    wu = [bdot(o['tinv'].astype(BF16), jnp.concatenate([o['khs'], lv[:hc].astype(BF16)], axis=1))
          for o, lv in zip(ops, lkak)]
    wub = [jnp.concatenate([w[:, :gw], -w[:, gw:]], axis=1).astype(BF16) for w in wu]
    abw = [bdot(o['ab'], w) for o, w in zip(ops, wub)]
    gmat = [tn(o['btes'], w[:, :gw]) for o, w in zip(ops, wub)]
    hv = [tn(jnp.concatenate([o['btes'], o['ktes']], axis=0), jnp.concatenate([w[:, gw:], o['vs']], axis=0))
          for o, w in zip(ops, wub)]
    out = []
    for o, aw, lv, gm, h_v in zip(ops, abw, lkak, gmat, hv):
        qt = unstack(o['rhs'].astype(F32) - aw[:, :gw])
        yv = unstack(aw[:, gw:] + lv[hc:])
        out.append((qt, yv, jnp.where(di, o['gend'], 0.0) - gm, h_v))
    return out


def _scan_kernel(r_ref, lw_ref, k_ref, v_ref, kn_ref, b_ref, bonus_ref, g_ref, lng_ref, lnb_ref, o_ref, h_ref):
    gw = SCAN_GROUP * HEAD_DIM
    cl = RWKV_CHUNK
    ngrp = N_HEADS // SCAN_GROUP

    @pl.when(pl.program_id(1) == 0)
    def _():
        h_ref[...] = jnp.zeros_like(h_ref)

    where = [(grp, c) for grp in range(ngrp) for c in range(SCAN_CHUNKS_PER_STEP)]
    sl = lambda grp, c: (slice(c * cl, (c + 1) * cl), slice(grp * gw, (grp + 1) * gw))
    parts = _scan_chunks([tuple(ref[sl(grp, c)] for ref in (r_ref, lw_ref, k_ref, v_ref, kn_ref, b_ref))
                          for grp, c in where])
    ones = _head_ones(gw)
    bdot = lambda a, b: jnp.dot(a, b, preferred_element_type=F32)
    for grp in range(ngrp):
        h = h_ref[grp]
        for c in range(SCAN_CHUNKS_PER_STEP):
            rows, ln = sl(grp, c)
            qt, yv, abar, h_v = parts[where.index((grp, c))]
            hh, hl = _split(h)
            y = bdot(qt.astype(BF16), hh) + yv
            ah, al = _split(abar)
            ahl = bdot(jnp.concatenate([ah, al], axis=0), hh)
            h = ahl[:gw] + ahl[gw:] + bdot(ah, hl) + h_v
            mean = _dot2(y, ones) * (1.0 / HEAD_DIM)
            yc = y - mean
            var = _dot2(yc * yc, ones) * (1.0 / HEAD_DIM)
            yn = yc * lax.rsqrt(var + GN_EPS) * lng_ref[:, ln] + lnb_ref[:, ln] + bonus_ref[rows, ln]
            o_ref[rows, ln] = (yn * g_ref[rows, ln]).astype(o_ref.dtype)
        h_ref[grp] = h


def _rwkv_scan(r, lw, k, v, kn, b, bonus, g, ln_g, ln_b, bsz, seq):
    t, w = r.shape
    cl = RWKV_CHUNK * SCAN_CHUNKS_PER_STEP
    cps = seq // cl
    row = pl.BlockSpec((cl, w), lambda bi, c: (bi * cps + c, 0))
    vec = pl.BlockSpec((1, w), lambda bi, c: (0, 0))
    return pl.pallas_call(
        _scan_kernel,
        grid=(bsz, cps),
        in_specs=[row] * 8 + [vec, vec],
        out_specs=row,
        out_shape=jax.ShapeDtypeStruct((t, w), BF16),
        scratch_shapes=[pltpu.VMEM((N_HEADS // SCAN_GROUP, SCAN_GROUP * HEAD_DIM, SCAN_GROUP * HEAD_DIM), F32)],
        compiler_params=_cparams("parallel", "arbitrary"),
        name="rwkv_scan",
    )(r, lw, k, v, kn, b, bonus, g, ln_g, ln_b)


def _rwkv_branch(zrw, p, l, bsz, seq):
    w = WIDTH
    vec = lambda a: a.reshape(1, -1)
    zpad = lambda a, top: jnp.concatenate([a, jnp.zeros_like(a)] if top else [jnp.zeros_like(a), a], axis=0)
    outs = _rwkvprep(zrw, seq // TOKEN_TILE, vec(p['rwkv_mu'][l]), vec(p['rwkv_w0'][l]),
                     zpad(p['rwkv_w2'][l], True), vec(p['rwkv_a0'][l]), zpad(p['rwkv_a2'][l], False),
                     p['rwkv_g2'][l], vec(p['rwkv_k_k'][l]), vec(p['rwkv_k_a'][l]), vec(p['rwkv_r_k'][l]))
    return _rwkv_scan(*outs, vec(p['rwkv_ln_g'][l]), vec(p['rwkv_ln_b'][l]), bsz, seq)


def _merge_kernel(x_ref, om_ref, or_ref, gl_ref, mod_ref, g2_ref, wpm_ref, wpr_ref, wo_ref, wq_ref,
                  h_o, xn_o, q_o):
    d = x_ref.shape[1]
    mod = mod_ref[0]
    pm = jnp.dot(om_ref[...], wpm_ref[...], preferred_element_type=F32)
    pr = jnp.dot(or_ref[...], wpr_ref[...], preferred_element_type=F32)
    gl = gl_ref[...]
    mixed = _sigmoid(gl[:, :d]) * pm + _sigmoid(gl[:, d:]) * pr
    h = x_ref[...] + mod[:, 2 * d:3 * d] * _dot(mixed, wo_ref[...])
    h_o[...] = h
    xn2 = _modnorm(h, g2_ref[...], mod, d, 1)
    xn_o[...] = xn2
    q_o[...] = _dot(xn2, wq_ref[...])


def _merge(x2, o_moba, o_rwkv, gl, mod3, g2, wpm, wpr, wo, wq, tps):
    t, d = x2.shape
    tm = TOKEN_TILE
    w = WIDTH
    nq = wq.shape[1]
    row = lambda i: (i, 0)
    fixed = lambda i: (0, 0)
    return pl.pallas_call(
        _merge_kernel,
        grid=(t // tm,),
        in_specs=[pl.BlockSpec((tm, d), row), pl.BlockSpec((tm, w), row), pl.BlockSpec((tm, w), row),
                  pl.BlockSpec((tm, 2 * d), row),
                  pl.BlockSpec((1, 1, mod3.shape[2]), lambda i: (i // tps, 0, 0)),
                  pl.BlockSpec((1, d), fixed), pl.BlockSpec((w, d), fixed), pl.BlockSpec((w, d), fixed),
                  pl.BlockSpec((d, d), fixed), pl.BlockSpec((d, nq), fixed)],
        out_specs=[pl.BlockSpec((tm, d), row), pl.BlockSpec((tm, d), row), pl.BlockSpec((tm, nq), row)],
        out_shape=[jax.ShapeDtypeStruct((t, d), F32), jax.ShapeDtypeStruct((t, d), F32),
                   jax.ShapeDtypeStruct((t, nq), F32)],
        compiler_params=_cparams("parallel"),
        name="merge_proj",
    )(x2, o_moba, o_rwkv, gl, mod3, g2, wpm, wpr, wo, wq)


PEER_TILE = 128


def _top16_rows(problems):
    tm = problems[0][0].shape[1]
    rank = lax.broadcasted_iota(jnp.int32, (PEER_TOPK, tm), 0)
    keys = [k for _, k in problems]
    big = jnp.int32(1 << 30)

    def rnd(r, carry):
        out = []
        hit = rank == r
        for (s, vals, idxs), key in zip(carry, keys):
            m = jnp.max(s, axis=0, keepdims=True)
            ix = jnp.min(jnp.where(s == m, key, big), axis=0, keepdims=True)
            out.append((jnp.where(key == ix, -jnp.inf, s), jnp.where(hit, m, vals),
                        jnp.where(hit, ix.astype(F32), idxs)))
        return tuple(out)

    z = jnp.zeros((PEER_TOPK, tm), F32)
    fin = lax.fori_loop(0, PEER_TOPK, rnd, tuple((s, z, z) for s, _ in problems))
    return [(vals, idxs) for _, vals, idxs in fin]


def _peer_topk_kernel(q_ref, sk_ref, idx_o, gate_o):
    tm = q_ref.shape[0]
    kk = PEER_TOPK
    nk = PEER_NKEYS
    rank = lax.broadcasted_iota(jnp.int32, (kk, tm), 0)
    pos_key = lax.broadcasted_iota(jnp.int32, (nk, tm), 0)
    row8 = lax.broadcasted_iota(jnp.int32, (8, tm), 0)
    tdot = lambda a, b: lax.dot_general(a, b, (((1,), (1,)), ((), ())), preferred_element_type=F32)
    cand_key = jnp.concatenate([rank] + [row8 + a * kk for a in range(1, 8)] + [(row8 + 8) * kk], axis=0)

    tops = []
    for h in range(PEER_HEADS):
        probs = []
        for p in range(2):
            hp = 2 * h + p
            qh, ql = _split(q_ref[:, hp * PEER_HALF:(hp + 1) * PEER_HALF])
            kh, kl = _split(sk_ref[hp])
            probs.append((tdot(kh, qh) + tdot(kh, ql) + tdot(kl, qh), pos_key))
        tops.append(_top16_rows(probs))

    idx_rows, gate_rows = [], []
    for h0 in range(0, PEER_HEADS, 2):
        probs = []
        for h in (h0, h0 + 1):
            (ts0, _), (ts1, _) = tops[h]
            cand = jnp.concatenate([ts0[0:1] + ts1] + [ts0[a:a + 1] + ts1[0:8] for a in range(1, 8)]
                                   + [ts0[8:16] + ts1[0:1]], axis=0)
            probs.append((cand, cand_key))
        for h, (best, pos) in zip((h0, h0 + 1), _top16_rows(probs)):
            (_, ti0), (_, ti1) = tops[h]
            pos = pos.astype(jnp.int32)
            ids = jnp.zeros((kk, tm), F32)
            for r in range(kk):
                pr = pos[r:r + 1, :]
                i0 = jnp.sum(jnp.where(rank == (pr >> 4), ti0, 0.0), axis=0, keepdims=True)
                i1 = jnp.sum(jnp.where(rank == (pr & (kk - 1)), ti1, 0.0), axis=0, keepdims=True)
                ids = jnp.where(rank == r, i0 * float(nk) + i1, ids)
            e = jnp.exp(best - jnp.max(best, axis=0, keepdims=True))
            gate_rows.append(e / jnp.sum(e, axis=0, keepdims=True))
            idx_rows.append(ids)
    idx_o[...] = jnp.concatenate(idx_rows, axis=0).T.astype(jnp.int32)
    gate_o[...] = jnp.concatenate(gate_rows, axis=0).T


def _peer_topk(q, subkeys16):
    t, nq = q.shape
    tm = PEER_TILE
    return pl.pallas_call(
        _peer_topk_kernel,
        grid=(t // tm,),
        in_specs=[pl.BlockSpec((tm, nq), lambda i: (i, 0)),
                  pl.BlockSpec(subkeys16.shape, lambda i: (0, 0, 0))],
        out_specs=[pl.BlockSpec((tm, 128), lambda i: (i, 0)), pl.BlockSpec((tm, 128), lambda i: (i, 0))],
        out_shape=[jax.ShapeDtypeStruct((t, 128), jnp.int32), jax.ShapeDtypeStruct((t, 128), F32)],
        compiler_params=_cparams("parallel"),
        name="peer_topk",
    )(q, subkeys16)


GATHER_TOKENS = 16
N_PICK = PEER_HEADS * PEER_TOPK
N_BURST = 4


def _peer_kernel(idx_cur, idx_nxt, gate_ref, xn_ref, h_ref, mod_ref, uv_hbm, o_ref, buf_a, buf_b, sem_a, sem_b):
    i = pl.program_id(0)
    n = pl.num_programs(0)
    tt = GATHER_TOKENS
    d = h_ref.shape[1]
    gt2 = mod_ref[0][:, 5 * d:6 * d]
    bufs = (buf_a, buf_b)
    sems = (sem_a, sem_b)
    per = N_PICK // N_BURST

    def row_copy(idx_ref, row, half, tok, e):
        return pltpu.make_async_copy(uv_hbm.at[idx_ref[row, e]], bufs[half].at[tok, pl.ds(e, 1), :],
                                     sems[half].at[tok])

    def burst(idx_ref, row0, half, toks, b):
        for k, tok in enumerate(toks):
            for e in range(b * per, (b + 1) * per):
                row_copy(idx_ref, row0 + k, half, tok, e).start(priority=e % 2)

    def pair(half, tok0, idx_next, next_row0):
        toks = (tok0, tok0 + 1)
        rows = [half * tt + t for t in toks]
        for t, row in zip(toks, rows):
            for e in range(N_PICK):
                row_copy(idx_cur, row, half, t, e).wait()
        words = [bufs[half][t] for t in toks]
        xts = [xn_ref[row:row + 1, :].astype(BF16) for row in rows]
        gates = [gate_ref[row:row + 1, :] for row in rows]
        hrows = [h_ref[row:row + 1, :] for row in rows]
        refill = lambda b: burst(idx_next, next_row0, 1 - half, toks, b)
        refill(0)
        us = [lax.bitcast_convert_type(w << 16, F32).astype(BF16) for w in words]
        refill(1)
        hds = [lax.dot_general(x, u, (((1,), (1,)), ((), ())), preferred_element_type=F32)
               for x, u in zip(xts, us)]
        vs = [lax.bitcast_convert_type(w & jnp.uint32(0xFFFF0000), F32).astype(BF16) for w in words]
        refill(2)
        hds = [0.5 * h * (1.0 + lax.erf(h * (2.0 ** -0.5))) * g for h, g in zip(hds, gates)]
        outs = [jnp.dot(h.astype(BF16), v, preferred_element_type=F32) for h, v in zip(hds, vs)]
        refill(3)
        for row, hr, out in zip(rows, hrows, outs):
            o_ref[row:row + 1, :] = hr + gt2 * out

    @pl.when(i == 0)
    def _():
        for tok in range(tt):
            for e in range(N_PICK):
                row_copy(idx_cur, tok, 0, tok, e).start(priority=e % 2)

    for tok0 in range(0, tt, 2):
        pair(0, tok0, idx_cur, tt + tok0)
    for tok0 in range(0, tt, 2):
        pair(1, tok0, idx_nxt, tok0)

    @pl.when(i == n - 1)
    def _():
        for tok in range(tt):
            for e in range(N_PICK):
                row_copy(idx_nxt, tok, 0, tok, e).wait()


def _peer(idx, gates, xn2, h1, mod3, uv_words, tokens_per_seq):
    t, d = h1.shape
    tt = GATHER_TOKENS
    blk = 2 * tt
    nsteps = t // blk
    row = lambda i: (i, 0)
    buf = pltpu.VMEM((tt, N_PICK, d), jnp.uint32)
    return pl.pallas_call(
        _peer_kernel,
        grid=(nsteps,),
        in_specs=[pl.BlockSpec((blk, N_PICK), row, memory_space=pltpu.SMEM),
                  pl.BlockSpec((blk, N_PICK), lambda i: (jnp.minimum(i + 1, nsteps - 1), 0), memory_space=pltpu.SMEM),
                  pl.BlockSpec((blk, N_PICK), row), pl.BlockSpec((blk, d), row), pl.BlockSpec((blk, d), row),
                  pl.BlockSpec((1, 1, mod3.shape[2]), lambda i: ((i * blk) // tokens_per_seq, 0, 0)),
                  pl.BlockSpec(memory_space=pl.ANY)],
        out_specs=pl.BlockSpec((blk, d), row),
        out_shape=jax.ShapeDtypeStruct((t, d), F32),
        scratch_shapes=[buf, buf, pltpu.SemaphoreType.DMA((tt,)), pltpu.SemaphoreType.DMA((tt,))],
        compiler_params=_cparams("arbitrary"),
        name="peer_experts",
    )(idx, idx, gates, xn2, h1, mod3, uv_words)


def _pack_uv(u, v):
    ub = lax.bitcast_convert_type(u.astype(BF16), jnp.uint16).astype(jnp.uint32)
    vb = lax.bitcast_convert_type(v.astype(BF16), jnp.uint16).astype(jnp.uint32)
    return (ub | (vb << 16)).reshape(u.shape[0], 1, u.shape[1])


def _rope_inv128():
    half = ROPE_DIM // 2
    inv = ROPE_THETA ** (-(jnp.arange(half, dtype=F32) * 2.0) / ROPE_DIM)
    lane = np.arange(128) % HEAD_DIM
    out = jnp.where(lane < ROPE_DIM, inv[lane % half], 0.0)
    return out.reshape(1, 128)


def kernel(x, c, positions, w_ada, b_ada, norm1_g, w_in, q_norm_g, k_norm_g, rwkv_mu, rwkv_w0, rwkv_w2, rwkv_a0, rwkv_a2, rwkv_g2, rwkv_k_k, rwkv_k_a, rwkv_r_k, rwkv_ln_g, rwkv_ln_b, w_proj_moba, w_proj_rwkv, w_out, norm2_g, peer_wq, peer_subkeys, peer_u, peer_v):
    bsz, seq, d = x.shape
    t = bsz * seq
    tps = seq // TOKEN_TILE
    l = 0
    x2 = x.reshape(t, d)
    mod3 = _adaln(c, w_ada[l], b_ada[l]).reshape(bsz, 1, 6 * d)
    w_in_b = w_in[l].astype(BF16)
    g1 = norm1_g[l].reshape(1, d)
    qkv = _inproj(x2, mod3, g1, w_in_b[:, :3 * WIDTH], tps)
    qb, kb, vb, km = _qkprep(qkv, positions.reshape(t, 1), jnp.tile(q_norm_g[l], N_HEADS).reshape(1, WIDTH),
                             jnp.tile(k_norm_g[l], N_HEADS).reshape(1, WIDTH), _rope_inv128())
    o_moba = _moba(qb, kb, vb, km, bsz, seq)
    params = dict(rwkv_mu=rwkv_mu, rwkv_w0=rwkv_w0, rwkv_w2=rwkv_w2, rwkv_a0=rwkv_a0, rwkv_a2=rwkv_a2,
                  rwkv_g2=rwkv_g2, rwkv_k_k=rwkv_k_k, rwkv_k_a=rwkv_k_a, rwkv_r_k=rwkv_r_k.reshape(1, WIDTH),
                  rwkv_ln_g=rwkv_ln_g, rwkv_ln_b=rwkv_ln_b)
    zrw = _inproj(x2, mod3, g1, w_in_b[:, 3 * WIDTH:3 * WIDTH + SHIFT_WIDTH], tps)
    o_rwkv = _rwkv_branch(zrw, params, l, bsz, seq)
    gl = _inproj(x2, mod3, g1, w_in_b[:, 3 * WIDTH + SHIFT_WIDTH:], tps)
    h1, xn2, q = _merge(x2, o_moba.reshape(t, WIDTH), o_rwkv, gl, mod3, norm2_g[l].reshape(1, d),
                        w_proj_moba[l].astype(BF16), w_proj_rwkv[l].astype(BF16), w_out[l].astype(BF16),
                        peer_wq[l].astype(BF16), tps)
    idx, gates = _peer_topk(q, peer_subkeys[l].reshape(2 * PEER_HEADS, PEER_NKEYS, PEER_HALF))
    out = _peer(idx, gates, xn2, h1, mod3, _pack_uv(peer_u[l], peer_v[l]), seq)
    return out.reshape(bsz, seq, d)
```

```python
import functools

import numpy as np
import jax
import jax.numpy as jnp
from jax import lax
from jax.experimental import pallas as pl
from jax.experimental.pallas import tpu as pltpu

F32 = jnp.float32
BF16 = jnp.bfloat16

HEAD_DIM = 64
N_HEADS = 8
WIDTH = N_HEADS * HEAD_DIM
MOBA_BLOCK = 256
MOBA_TOPK = 3
ROPE_THETA = 500000.0
ROPE_DIM = HEAD_DIM // 4
LORA_W = 64
LORA_A = 64
LORA_G = 128
SHIFT_WIDTH = 3 * WIDTH + LORA_W + LORA_A + LORA_G
GN_EPS = 64e-5
RMS_EPS = 1e-6
NEG_INF = -1e30
PEER_HEADS = 8
PEER_NKEYS = 128
PEER_HALF = 128
PEER_TOPK = 16

TOKEN_TILE = 256
V7X_VMEM_LIMIT = 48 * 1024 * 1024


def _cparams(*sem, flags=None):
    return pltpu.CompilerParams(dimension_semantics=sem, vmem_limit_bytes=V7X_VMEM_LIMIT, flags=flags)


def _dot(a, b):
    return jnp.dot(a.astype(BF16), b.astype(BF16), preferred_element_type=F32)


def _dot_t(a, b):
    return lax.dot_general(a.astype(BF16), b.astype(BF16), (((1,), (1,)), ((), ())),
                           preferred_element_type=F32)


def _split(a):
    hi = a.astype(BF16)
    lo = (a - hi.astype(F32)).astype(BF16)
    return hi, lo


def _split3(a):
    hi = a.astype(BF16)
    r1 = a - hi.astype(F32)
    mid = r1.astype(BF16)
    return hi, mid, (r1 - mid.astype(F32)).astype(BF16)


def _dot3(a, b):
    ah, al = _split(a)
    bh, bl = _split(b)
    return (jnp.dot(ah, bh, preferred_element_type=F32) + jnp.dot(ah, bl, preferred_element_type=F32)
            + jnp.dot(al, bh, preferred_element_type=F32))


def _dot2(a, b_exact):
    ah, al = _split(a)
    return jnp.dot(ah, b_exact, preferred_element_type=F32) + jnp.dot(al, b_exact, preferred_element_type=F32)


def _head_ones(n):
    r = lax.broadcasted_iota(jnp.int32, (n, n), 0) >> 6
    c = lax.broadcasted_iota(jnp.int32, (n, n), 1) >> 6
    return (r == c).astype(BF16)


def _sigmoid(x):
    return 1.0 / (1.0 + jnp.exp(-x))


def _ada_kernel(c_ref, w_ref, b_ref, o_ref):
    c = c_ref[...]
    o_ref[...] = _dot3(c * _sigmoid(c), w_ref[...]) + b_ref[...]


def _adaln(c, w_ada, b_ada):
    bsz, d = c.shape
    n = w_ada.shape[1]
    nb = 1024
    return pl.pallas_call(
        _ada_kernel,
        grid=(n // nb,),
        in_specs=[pl.BlockSpec((bsz, d), lambda j: (0, 0)),
                  pl.BlockSpec((d, nb), lambda j: (0, j)),
                  pl.BlockSpec((1, nb), lambda j: (0, j))],
        out_specs=pl.BlockSpec((bsz, nb), lambda j: (0, j)),
        out_shape=jax.ShapeDtypeStruct((bsz, n), F32),
        compiler_params=_cparams("arbitrary"),
        name="adaln_mod",
    )(c, w_ada, b_ada.reshape(1, n))


def _modnorm(x, g, mod, d, which):
    ms = jnp.mean(x * x, axis=-1, keepdims=True)
    y = x * lax.rsqrt(ms + RMS_EPS) * g
    sh = mod[:, (3 * which) * d:(3 * which + 1) * d]
    sc = mod[:, (3 * which + 1) * d:(3 * which + 2) * d]
    return y * (1.0 + sc) + sh


def _inproj_kernel(x_ref, mod_ref, g_ref, w_ref, o_ref):
    d = x_ref.shape[1]
    xn = _modnorm(x_ref[...], g_ref[...], mod_ref[0], d, 0)
    o_ref[...] = jnp.dot(xn.astype(BF16), w_ref[...], preferred_element_type=F32)


def _inproj(x2, mod3, g, w_bf16, tps):
    t, d = x2.shape
    n = w_bf16.shape[1]
    tm = TOKEN_TILE
    return pl.pallas_call(
        _inproj_kernel,
        grid=(t // tm,),
        in_specs=[pl.BlockSpec((tm, d), lambda i: (i, 0)),
                  pl.BlockSpec((1, 1, mod3.shape[2]), lambda i: (i // tps, 0, 0)),
                  pl.BlockSpec((1, d), lambda i: (0, 0)),
                  pl.BlockSpec((d, n), lambda i: (0, 0))],
        out_specs=pl.BlockSpec((tm, n), lambda i: (i, 0)),
        out_shape=jax.ShapeDtypeStruct((t, n), F32),
        compiler_params=_cparams("parallel"),
        name="in_proj",
    )(x2, mod3, g, w_bf16)


def _qkprep_kernel(qkv_ref, pos_ref, gq_ref, gk_ref, inv_ref, qo_ref, ko_ref, vo_ref, km_ref):
    w = WIDTH
    ones = _head_ones(w)
    ang = pos_ref[...].astype(F32) * inv_ref[...]
    cos = jnp.tile(jnp.cos(ang), (1, w // 128))
    sin = jnp.tile(jnp.sin(ang), (1, w // 128))
    dd = lax.broadcasted_iota(jnp.int32, (1, w), 1) & (HEAD_DIM - 1)
    half = ROPE_DIM // 2
    s_lo = jnp.where(dd < half, -1.0, 0.0) * sin
    s_hi = jnp.where((dd >= half) & (dd < ROPE_DIM), 1.0, 0.0) * sin

    def norm_rope(xh, g):
        ssq = _dot2(xh * xh, ones)
        y = xh * lax.rsqrt(ssq * (1.0 / HEAD_DIM) + RMS_EPS) * g
        return y * cos + pltpu.roll(y, w - half, 1) * s_lo + pltpu.roll(y, half, 1) * s_hi

    q = norm_rope(qkv_ref[:, 0:w], gq_ref[...])
    k = norm_rope(qkv_ref[:, w:2 * w], gk_ref[...])
    qo_ref[...] = (q * (HEAD_DIM ** -0.5)).astype(BF16)
    ko_ref[...] = k.astype(BF16)
    vo_ref[...] = qkv_ref[:, 2 * w:3 * w].astype(BF16)
    km_ref[0] = jnp.mean(k, axis=0, keepdims=True)


def _qkprep(qkv, pos2, gq, gk, inv128):
    t = qkv.shape[0]
    tm = MOBA_BLOCK
    w = WIDTH
    row = lambda i: (i, 0)
    fixed = lambda i: (0, 0)
    return pl.pallas_call(
        _qkprep_kernel,
        grid=(t // tm,),
        in_specs=[pl.BlockSpec((tm, 3 * w), row), pl.BlockSpec((tm, 1), row),
                  pl.BlockSpec((1, w), fixed), pl.BlockSpec((1, w), fixed), pl.BlockSpec((1, 128), fixed)],
        out_specs=[pl.BlockSpec((tm, w), row), pl.BlockSpec((tm, w), row), pl.BlockSpec((tm, w), row),
                   pl.BlockSpec((1, 1, w), lambda i: (i, 0, 0))],
        out_shape=[jax.ShapeDtypeStruct((t, w), BF16), jax.ShapeDtypeStruct((t, w), BF16),
                   jax.ShapeDtypeStruct((t, w), BF16), jax.ShapeDtypeStruct((t // tm, 1, w), F32)],
        compiler_params=_cparams("parallel"),
        name="moba_qk_prep",
    )(qkv, pos2, gq, gk, inv128)


def _moba_kernel(q_ref, k_ref, v_ref, km_ref, o_ref):
    blk = MOBA_BLOCK
    i = pl.program_id(2)
    q = q_ref[0]
    km = km_ref[0]
    nb = km.shape[0]
    lane = lax.broadcasted_iota(jnp.int32, (blk, 128), 1)
    col = lax.broadcasted_iota(jnp.int32, (blk, nb), 1)
    qpos = lax.broadcasted_iota(jnp.int32, (blk, blk), 0)
    kpos = lax.broadcasted_iota(jnp.int32, (blk, blk), 1)
    row0 = pl.multiple_of(i * blk, blk)
    k_own = k_ref[0, pl.ds(row0, blk), :]
    v_own = v_ref[0, pl.ds(row0, blk), :]

    qs, biases, carry0 = [], [], []
    for hh in range(2):
        qh = jnp.where((lane >> 6) == hh, q, jnp.zeros_like(q))
        g = jnp.where(col < i, _dot_t(qh, km), -jnp.inf)
        sel = jnp.zeros((blk, nb), jnp.bool_)
        for _ in range(MOBA_TOPK):
            m = jnp.max(g, axis=-1, keepdims=True)
            idx = jnp.min(jnp.where(g == m, col, nb), axis=-1, keepdims=True)
            pick = (col == idx) & (g > -jnp.inf)
            sel = sel | pick
            g = jnp.where(pick, -jnp.inf, g)
        biases.append(jnp.where(sel, 0.0, NEG_INF))
        s = jnp.where(kpos <= qpos, _dot_t(qh, k_own), NEG_INF)
        m0 = jnp.max(s, axis=-1, keepdims=True)
        p = jnp.exp(s - m0)
        carry0 += [m0, jnp.sum(p, axis=-1, keepdims=True), _dot(p, v_own)]
        qs.append(qh)

    def body(jj, carry):
        r0 = pl.multiple_of(jj * (2 * blk), 2 * blk)
        kj = k_ref[0, pl.ds(r0, 2 * blk), :]
        vj = v_ref[0, pl.ds(r0, 2 * blk), :]
        hu = [(hh, u) for hh in range(2) for u in range(2)]
        raw = [_dot_t(qs[hh], kj[u * blk:(u + 1) * blk]) for hh, u in hu]
        bj = [jnp.sum(jnp.where(col == 2 * jj + u, biases[hh], 0.0), axis=-1, keepdims=True) for hh, u in hu]
        ss = [a + b for a, b in zip(raw, bj)]
        mx = [jnp.max(a, axis=-1, keepdims=True) for a in ss]
        m_new = [jnp.maximum(carry[3 * hh], jnp.maximum(mx[2 * hh], mx[2 * hh + 1])) for hh in range(2)]
        alpha = [jnp.exp(carry[3 * hh] - m_new[hh]) for hh in range(2)]
        ps = [jnp.exp(a - m_new[hh]) for a, (hh, u) in zip(ss, hu)]
        pcat = [jnp.concatenate([ps[2 * hh], ps[2 * hh + 1]], axis=1) for hh in range(2)]
        lsum = [jnp.sum(pc, axis=-1, keepdims=True) for pc in pcat]
        pv = [_dot(pc, vj) for pc in pcat]
        out = []
        for hh in range(2):
            out += [m_new[hh], alpha[hh] * carry[3 * hh + 1] + lsum[hh], alpha[hh] * carry[3 * hh + 2] + pv[hh]]
        return tuple(out)

    fin = lax.fori_loop(0, (i + 1) // 2, body, tuple(carry0))
    o0 = fin[2] / fin[1]
    o1 = fin[5] / fin[4]
    o_ref[0] = jnp.where(lane < HEAD_DIM, o0, o1).astype(o_ref.dtype)


def _moba(qb, kb, vb, km, bsz, seq):
    w = WIDTH
    nb = seq // MOBA_BLOCK
    q3 = qb.reshape(bsz, seq, w)
    k3 = kb.reshape(bsz, seq, w)
    v3 = vb.reshape(bsz, seq, w)
    km3 = km.reshape(bsz, nb, w)
    return pl.pallas_call(
        _moba_kernel,
        grid=(bsz, w // 128, nb),
        in_specs=[pl.BlockSpec((1, MOBA_BLOCK, 128), lambda b, p, i: (b, i, p)),
                  pl.BlockSpec((1, seq, 128), lambda b, p, i: (b, 0, p)),
                  pl.BlockSpec((1, seq, 128), lambda b, p, i: (b, 0, p)),
                  pl.BlockSpec((1, nb, 128), lambda b, p, i: (b, 0, p))],
        out_specs=pl.BlockSpec((1, MOBA_BLOCK, 128), lambda b, p, i: (b, i, p)),
        out_shape=jax.ShapeDtypeStruct((bsz, seq, w), BF16),
        compiler_params=_cparams("parallel", "parallel", "arbitrary"),
        name="moba_attention",
    )(q3, k3, v3, km3)


def _rwkvprep_kernel(z_ref, zp_ref, mu_ref, w0_ref, w2_ref, a0_ref, a2_ref, g2_ref, kk_ref, ka_ref, rk_ref,
                     r_o, lw_o, k_o, v_o, kn_o, b_o, bonus_o, g_o, *, tps):
    i = pl.program_id(0)
    w = WIDTH
    z = z_ref[...]
    tm = z.shape[0]
    prev_last = jnp.where((i % tps) == 0, 0.0, zp_ref[7:8, :])
    rows = lax.broadcasted_iota(jnp.int32, (tm, 1), 0)
    prev = jnp.where(rows == 0, prev_last, pltpu.roll(z, 1, 0))
    zs = z + (prev - z) * mu_ref[...]
    zr, zk, zv = zs[:, 0:w], zs[:, w:2 * w], zs[:, 2 * w:3 * w]
    zwa = zs[:, 3 * w:3 * w + 128]
    zg = zs[:, 3 * w + 128:3 * w + 256]
    nw = -(w0_ref[...] + _dot3(jnp.tanh(zwa), w2_ref[...]))
    softplus = jnp.maximum(nw, 0.0) + jnp.log(1.0 + jnp.exp(-jnp.abs(nw)))
    lw_o[...] = -jnp.exp(-softplus - 0.5)
    a = _sigmoid(a0_ref[...] + _dot3(zwa, a2_ref[...]))
    g_o[...] = _dot3(_sigmoid(zg), g2_ref[...])
    ones = _head_ones(w)
    kn = zk * kk_ref[...]
    kn = kn / jnp.maximum(jnp.sqrt(_dot2(kn * kn, ones)), 1e-12)
    k = zk * (1.0 + (a - 1.0) * ka_ref[...])
    r_o[...] = zr
    k_o[...] = k
    v_o[...] = zv
    kn_o[...] = kn
    b_o[...] = kn * a
    bonus_o[...] = _dot2(zr * k * rk_ref[...], ones) * zv


def _rwkvprep(zrw, tps, mu, w0, w2aug, a0, a2aug, g2, k_k, k_a, r_k):
    t = zrw.shape[0]
    tm = TOKEN_TILE
    w = WIDTH
    row = lambda i: (i, 0)
    fixed = lambda i: (0, 0)
    vec = pl.BlockSpec((1, w), fixed)
    outs = [jax.ShapeDtypeStruct((t, w), F32)] * 8
    return pl.pallas_call(
        functools.partial(_rwkvprep_kernel, tps=tps),
        grid=(t // tm,),
        in_specs=[pl.BlockSpec((tm, SHIFT_WIDTH), row),
                  pl.BlockSpec((8, SHIFT_WIDTH), lambda i: (jnp.maximum(i * (tm // 8) - 1, 0), 0)),
                  pl.BlockSpec((1, SHIFT_WIDTH), fixed),
                  vec, pl.BlockSpec((128, w), fixed), vec, pl.BlockSpec((128, w), fixed),
                  pl.BlockSpec((LORA_G, w), fixed), vec, vec, vec],
        out_specs=[pl.BlockSpec((tm, w), row)] * 8,
        out_shape=outs,
        compiler_params=_cparams("parallel"),
        name="rwkv_prep",
    )(zrw, zrw, mu, w0, w2aug, a0, a2aug, g2, k_k, k_a, r_k)


RWKV_CHUNK = 32


SCAN_GROUP = 4
SCAN_CHUNKS_PER_STEP = 4


def _scan_chunks(probs):
    cl, gw = probs[0][0].shape
    hc = SCAN_GROUP * cl
    bdot = lambda a, b: jnp.dot(a, b, preferred_element_type=F32)
    tn = lambda a, b: lax.dot_general(a, b, (((0,), (0,)), ((), ())), preferred_element_type=F32)
    tri = (lax.broadcasted_iota(jnp.int32, (cl, cl), 0) >= lax.broadcasted_iota(jnp.int32, (cl, cl), 1)).astype(BF16)
    hmask = (lax.broadcasted_iota(jnp.int32, (hc, gw), 0) // cl) == (lax.broadcasted_iota(jnp.int32, (hc, gw), 1) >> 6)
    stack = lambda a: jnp.where(hmask, jnp.tile(a, (SCAN_GROUP, 1)), 0.0).astype(BF16)
    unstack = lambda a: functools.reduce(lambda p, q: p + q, [a[h * cl:(h + 1) * cl] for h in range(SCAN_GROUP)])
    ri = lax.broadcasted_iota(jnp.int32, (hc, hc), 0)
    ci = lax.broadcasted_iota(jnp.int32, (hc, hc), 1)
    strict = ri > ci
    incl = ri >= ci
    eye = jnp.where(ri == ci, 1.0, 0.0)
    di = lax.broadcasted_iota(jnp.int32, (gw, gw), 0) == lax.broadcasted_iota(jnp.int32, (gw, gw), 1)

    cs2 = [bdot(tri, jnp.concatenate(_split(p[1]), axis=1)) for p in probs]
    ops = []
    for (r, lw, k, v, kn, b), c2 in zip(probs, cs2):
        cs = c2[:, :gw] + c2[:, gw:]
        cs_end = cs[cl - 1:cl, :]
        inv = jnp.exp(-cs)
        dec_end = jnp.exp(cs_end - cs)
        ops.append(dict(khs=stack(kn * jnp.exp(cs - lw)),
                        rhs=stack(r * jnp.exp(cs)),
                        bts=stack(b * inv), kts=stack(k * inv), vs=stack(v),
                        btes=stack(b * dec_end), ktes=stack(k * dec_end), gend=jnp.exp(cs_end)))
    grams = [lax.dot_general(jnp.concatenate([o['khs'], o['rhs']], axis=0),
                             jnp.concatenate([o['bts'], o['kts']], axis=0),
                             (((1,), (1,)), ((), ())), preferred_element_type=F32) for o in ops]
    for o, g in zip(ops, grams):
        lb = jnp.where(strict, g[:hc, :hc], 0.0)
        o['lkak'] = jnp.concatenate([jnp.where(strict, g[:hc, hc:], 0.0),
                                     jnp.where(incl, g[hc:, hc:], 0.0)], axis=0).astype(BF16)
        o['ab'] = jnp.where(incl, g[hc:, :hc], 0.0).astype(BF16)
        o['tinv'] = eye - lb
        o['pw'] = (-lb).astype(BF16)
    sq = [bdot(o['pw'], o['pw']) for o in ops]
    for o, q in zip(ops, sq):
        o['pw'] = q.astype(BF16)
    levels = int(np.log2(cl)) - 1
    for lvl in range(levels):
        both = [bdot(jnp.concatenate([o['pw'], o['tinv'].astype(BF16)], axis=0), o['pw']) for o in ops]
        for o, bo in zip(ops, both):
            o['tinv'] = o['tinv'] + bo[hc:]
            o['pw'] = bo[:hc].astype(BF16)
    lkak = [bdot(o['lkak'], o['vs']) for o in ops]
    wu = [bdot(o['tinv'].astype(BF16), jnp.concatenate([o['khs'], lv[:hc].astype(BF16)], axis=1))
          for o, lv in zip(ops, lkak)]
    wub = [jnp.concatenate([w[:, :gw], -w[:, gw:]], axis=1).astype(BF16) for w in wu]
    abw = [bdot(o['ab'], w) for o, w in zip(ops, wub)]
    gmat = [tn(o['btes'], w[:, :gw]) for o, w in zip(ops, wub)]
    hv = [tn(jnp.concatenate([o['btes'], o['ktes']], axis=0), jnp.concatenate([w[:, gw:], o['vs']], axis=0))
          for o, w in zip(ops, wub)]
    out = []
    for o, aw, lv, gm, h_v in zip(ops, abw, lkak, gmat, hv):
        qt = unstack(o['rhs'].astype(F32) - aw[:, :gw])
        yv = unstack(aw[:, gw:] + lv[hc:])
        out.append((qt, yv, jnp.where(di, o['gend'], 0.0) - gm, h_v))
    return out


def _scan_kernel(r_ref, lw_ref, k_ref, v_ref, kn_ref, b_ref, bonus_ref, g_ref, lng_ref, lnb_ref, o_ref, h_ref):
    gw = SCAN_GROUP * HEAD_DIM
    cl = RWKV_CHUNK
    ngrp = N_HEADS // SCAN_GROUP

    @pl.when(pl.program_id(1) == 0)
    def _():
        h_ref[...] = jnp.zeros_like(h_ref)

    where = [(grp, c) for grp in range(ngrp) for c in range(SCAN_CHUNKS_PER_STEP)]
    sl = lambda grp, c: (slice(c * cl, (c + 1) * cl), slice(grp * gw, (grp + 1) * gw))
    parts = _scan_chunks([tuple(ref[sl(grp, c)] for ref in (r_ref, lw_ref, k_ref, v_ref, kn_ref, b_ref))
                          for grp, c in where])
    ones = _head_ones(gw)
    bdot = lambda a, b: jnp.dot(a, b, preferred_element_type=F32)
    for grp in range(ngrp):
        h = h_ref[grp]
        for c in range(SCAN_CHUNKS_PER_STEP):
            rows, ln = sl(grp, c)
            qt, yv, abar, h_v = parts[where.index((grp, c))]
            hh, hl = _split(h)
            y = bdot(qt.astype(BF16), hh) + yv
            ah, al = _split(abar)
            ahl = bdot(jnp.concatenate([ah, al], axis=0), hh)
            h = ahl[:gw] + ahl[gw:] + bdot(ah, hl) + h_v
            mean = _dot2(y, ones) * (1.0 / HEAD_DIM)
            yc = y - mean
            var = _dot2(yc * yc, ones) * (1.0 / HEAD_DIM)
            yn = yc * lax.rsqrt(var + GN_EPS) * lng_ref[:, ln] + lnb_ref[:, ln] + bonus_ref[rows, ln]
            o_ref[rows, ln] = (yn * g_ref[rows, ln]).astype(o_ref.dtype)
        h_ref[grp] = h


def _rwkv_scan(r, lw, k, v, kn, b, bonus, g, ln_g, ln_b, bsz, seq):
    t, w = r.shape
    cl = RWKV_CHUNK * SCAN_CHUNKS_PER_STEP
    cps = seq // cl
    row = pl.BlockSpec((cl, w), lambda bi, c: (bi * cps + c, 0))
    vec = pl.BlockSpec((1, w), lambda bi, c: (0, 0))
    return pl.pallas_call(
        _scan_kernel,
        grid=(bsz, cps),
        in_specs=[row] * 8 + [vec, vec],
        out_specs=row,
        out_shape=jax.ShapeDtypeStruct((t, w), BF16),
        scratch_shapes=[pltpu.VMEM((N_HEADS // SCAN_GROUP, SCAN_GROUP * HEAD_DIM, SCAN_GROUP * HEAD_DIM), F32)],
        compiler_params=_cparams("parallel", "arbitrary"),
        name="rwkv_scan",
    )(r, lw, k, v, kn, b, bonus, g, ln_g, ln_b)


def _rwkv_branch(zrw, p, l, bsz, seq):
    w = WIDTH
    vec = lambda a: a.reshape(1, -1)
    zpad = lambda a, top: jnp.concatenate([a, jnp.zeros_like(a)] if top else [jnp.zeros_like(a), a], axis=0)
    outs = _rwkvprep(zrw, seq // TOKEN_TILE, vec(p['rwkv_mu'][l]), vec(p['rwkv_w0'][l]),
                     zpad(p['rwkv_w2'][l], True), vec(p['rwkv_a0'][l]), zpad(p['rwkv_a2'][l], False),
                     p['rwkv_g2'][l], vec(p['rwkv_k_k'][l]), vec(p['rwkv_k_a'][l]), vec(p['rwkv_r_k'][l]))
    return _rwkv_scan(*outs, vec(p['rwkv_ln_g'][l]), vec(p['rwkv_ln_b'][l]), bsz, seq)


def _merge_kernel(x_ref, om_ref, or_ref, gl_ref, mod_ref, g2_ref, wpm_ref, wpr_ref, wo_ref, wq_ref,
                  h_o, xn_o, q_o):
    d = x_ref.shape[1]
    mod = mod_ref[0]
    pm = jnp.dot(om_ref[...], wpm_ref[...], preferred_element_type=F32)
    pr = jnp.dot(or_ref[...], wpr_ref[...], preferred_element_type=F32)
    gl = gl_ref[...]
    mixed = _sigmoid(gl[:, :d]) * pm + _sigmoid(gl[:, d:]) * pr
    h = x_ref[...] + mod[:, 2 * d:3 * d] * _dot(mixed, wo_ref[...])
    h_o[...] = h
    xn2 = _modnorm(h, g2_ref[...], mod, d, 1)
    xn_o[...] = xn2
    q_o[...] = _dot(xn2, wq_ref[...])


def _merge(x2, o_moba, o_rwkv, gl, mod3, g2, wpm, wpr, wo, wq, tps):
    t, d = x2.shape
    tm = TOKEN_TILE
    w = WIDTH
    nq = wq.shape[1]
    row = lambda i: (i, 0)
    fixed = lambda i: (0, 0)
    return pl.pallas_call(
        _merge_kernel,
        grid=(t // tm,),
        in_specs=[pl.BlockSpec((tm, d), row), pl.BlockSpec((tm, w), row), pl.BlockSpec((tm, w), row),
                  pl.BlockSpec((tm, 2 * d), row),
                  pl.BlockSpec((1, 1, mod3.shape[2]), lambda i: (i // tps, 0, 0)),
                  pl.BlockSpec((1, d), fixed), pl.BlockSpec((w, d), fixed), pl.BlockSpec((w, d), fixed),
                  pl.BlockSpec((d, d), fixed), pl.BlockSpec((d, nq), fixed)],
        out_specs=[pl.BlockSpec((tm, d), row), pl.BlockSpec((tm, d), row), pl.BlockSpec((tm, nq), row)],
        out_shape=[jax.ShapeDtypeStruct((t, d), F32), jax.ShapeDtypeStruct((t, d), F32),
                   jax.ShapeDtypeStruct((t, nq), F32)],
        compiler_params=_cparams("parallel"),
        name="merge_proj",
    )(x2, o_moba, o_rwkv, gl, mod3, g2, wpm, wpr, wo, wq)


PEER_TILE = 128


def _top16_rows(problems):
    tm = problems[0][0].shape[1]
    rank = lax.broadcasted_iota(jnp.int32, (PEER_TOPK, tm), 0)
    keys = [k for _, k in problems]
    big = jnp.int32(1 << 30)

    def rnd(r, carry):
        out = []
        hit = rank == r
        for (s, vals, idxs), key in zip(carry, keys):
            m = jnp.max(s, axis=0, keepdims=True)
            ix = jnp.min(jnp.where(s == m, key, big), axis=0, keepdims=True)
            out.append((jnp.where(key == ix, -jnp.inf, s), jnp.where(hit, m, vals),
                        jnp.where(hit, ix.astype(F32), idxs)))
        return tuple(out)

    z = jnp.zeros((PEER_TOPK, tm), F32)
    fin = lax.fori_loop(0, PEER_TOPK, rnd, tuple((s, z, z) for s, _ in problems))
    return [(vals, idxs) for _, vals, idxs in fin]


def _peer_topk_kernel(q_ref, sk_ref, idx_o, gate_o):
    tm = q_ref.shape[0]
    kk = PEER_TOPK
    nk = PEER_NKEYS
    rank = lax.broadcasted_iota(jnp.int32, (kk, tm), 0)
    pos_key = lax.broadcasted_iota(jnp.int32, (nk, tm), 0)
    row8 = lax.broadcasted_iota(jnp.int32, (8, tm), 0)
    tdot = lambda a, b: lax.dot_general(a, b, (((1,), (1,)), ((), ())), preferred_element_type=F32)
    cand_key = jnp.concatenate([rank] + [row8 + a * kk for a in range(1, 8)] + [(row8 + 8) * kk], axis=0)

    tops = []
    for h in range(PEER_HEADS):
        probs = []
        for p in range(2):
            hp = 2 * h + p
            qh, ql = _split(q_ref[:, hp * PEER_HALF:(hp + 1) * PEER_HALF])
            kh, kl = _split(sk_ref[hp])
            probs.append((tdot(kh, qh) + tdot(kh, ql) + tdot(kl, qh), pos_key))
        tops.append(_top16_rows(probs))

    idx_rows, gate_rows = [], []
    for h0 in range(0, PEER_HEADS, 2):
        probs = []
        for h in (h0, h0 + 1):
            (ts0, _), (ts1, _) = tops[h]
            cand = jnp.concatenate([ts0[0:1] + ts1] + [ts0[a:a + 1] + ts1[0:8] for a in range(1, 8)]
                                   + [ts0[8:16] + ts1[0:1]], axis=0)
            probs.append((cand, cand_key))
        for h, (best, pos) in zip((h0, h0 + 1), _top16_rows(probs)):
            (_, ti0), (_, ti1) = tops[h]
            pos = pos.astype(jnp.int32)
            ids = jnp.zeros((kk, tm), F32)
            for r in range(kk):
                pr = pos[r:r + 1, :]
                i0 = jnp.sum(jnp.where(rank == (pr >> 4), ti0, 0.0), axis=0, keepdims=True)
                i1 = jnp.sum(jnp.where(rank == (pr & (kk - 1)), ti1, 0.0), axis=0, keepdims=True)
                ids = jnp.where(rank == r, i0 * float(nk) + i1, ids)
            e = jnp.exp(best - jnp.max(best, axis=0, keepdims=True))
            gate_rows.append(e / jnp.sum(e, axis=0, keepdims=True))
            idx_rows.append(ids)
    idx_o[...] = jnp.concatenate(idx_rows, axis=0).T.astype(jnp.int32)
    gate_o[...] = jnp.concatenate(gate_rows, axis=0).T


def _peer_topk(q, subkeys16):
    t, nq = q.shape
    tm = PEER_TILE
    return pl.pallas_call(
        _peer_topk_kernel,
        grid=(t // tm,),
        in_specs=[pl.BlockSpec((tm, nq), lambda i: (i, 0)),
                  pl.BlockSpec(subkeys16.shape, lambda i: (0, 0, 0))],
        out_specs=[pl.BlockSpec((tm, 128), lambda i: (i, 0)), pl.BlockSpec((tm, 128), lambda i: (i, 0))],
        out_shape=[jax.ShapeDtypeStruct((t, 128), jnp.int32), jax.ShapeDtypeStruct((t, 128), F32)],
        compiler_params=_cparams("parallel"),
        name="peer_topk",
    )(q, subkeys16)


GATHER_TOKENS = 16
N_PICK = PEER_HEADS * PEER_TOPK
N_BURST = 4


def _peer_kernel(idx_cur, idx_nxt, gate_ref, xn_ref, h_ref, mod_ref, uv_hbm, o_ref, buf_a, buf_b, sem_a, sem_b):
    i = pl.program_id(0)
    n = pl.num_programs(0)
    tt = GATHER_TOKENS
    d = h_ref.shape[1]
    gt2 = mod_ref[0][:, 5 * d:6 * d]
    bufs = (buf_a, buf_b)
    sems = (sem_a, sem_b)
    per = N_PICK // N_BURST

    def row_copy(idx_ref, row, half, tok, e):
        return pltpu.make_async_copy(uv_hbm.at[idx_ref[row, e]], bufs[half].at[tok, pl.ds(e * 8, 8), :],
                                     sems[half].at[tok])

    def burst(idx_ref, row0, half, toks, b):
        for k, tok in enumerate(toks):
            for e in range(b * per, (b + 1) * per):
                row_copy(idx_ref, row0 + k, half, tok, e).start(priority=e % 2)

    def pair(half, tok0, idx_next, next_row0):
        toks = (tok0, tok0 + 1)
        rows = [half * tt + t for t in toks]
        for t, row in zip(toks, rows):
            for e in range(N_PICK):
                row_copy(idx_cur, row, half, t, e).wait()
        nch = d // 128
        words = [[bufs[half][t, pl.ds(c, N_PICK, stride=nch), :] for c in range(nch)] for t in toks]
        xts = [xn_ref[row:row + 1, :].astype(BF16) for row in rows]
        gates = [gate_ref[row:row + 1, :] for row in rows]
        hrows = [h_ref[row:row + 1, :] for row in rows]
        refill = lambda b: burst(idx_next, next_row0, 1 - half, toks, b)
        refill(0)
        us = [[lax.bitcast_convert_type(w << 16, F32).astype(BF16) for w in ws] for ws in words]
        refill(1)
        tdot = lambda a, b: lax.dot_general(a, b, (((1,), (1,)), ((), ())), preferred_element_type=F32)
        hds = [functools.reduce(lambda p, q: p + q, [tdot(x[:, c * 128:(c + 1) * 128], u[c]) for c in range(nch)])
               for x, u in zip(xts, us)]
        vs = [[lax.bitcast_convert_type(w & jnp.uint32(0xFFFF0000), F32).astype(BF16) for w in ws] for ws in words]
        refill(2)
        hds = [(0.5 * h * (1.0 + lax.erf(h * (2.0 ** -0.5))) * g).astype(BF16) for h, g in zip(hds, gates)]
        outs = [jnp.concatenate([jnp.dot(h, v[c], preferred_element_type=F32) for c in range(nch)], axis=1)
                for h, v in zip(hds, vs)]
        refill(3)
        for row, hr, out in zip(rows, hrows, outs):
            o_ref[row:row + 1, :] = hr + gt2 * out

    @pl.when(i == 0)
    def _():
        for tok in range(tt):
            for e in range(N_PICK):
                row_copy(idx_cur, tok, 0, tok, e).start(priority=e % 2)

    for tok0 in range(0, tt, 2):
        pair(0, tok0, idx_cur, tt + tok0)
    for tok0 in range(0, tt, 2):
        pair(1, tok0, idx_nxt, tok0)

    @pl.when(i == n - 1)
    def _():
        for tok in range(tt):
            for e in range(N_PICK):
                row_copy(idx_nxt, tok, 0, tok, e).wait()


def _peer(idx, gates, xn2, h1, mod3, uv_words, tokens_per_seq):
    t, d = h1.shape
    tt = GATHER_TOKENS
    blk = 2 * tt
    nsteps = t // blk
    row = lambda i: (i, 0)
    buf = pltpu.VMEM((tt, N_PICK * (d // 128), 128), jnp.uint32)
    return pl.pallas_call(
        _peer_kernel,
        grid=(nsteps,),
        in_specs=[pl.BlockSpec((blk, N_PICK), row, memory_space=pltpu.SMEM),
                  pl.BlockSpec((blk, N_PICK), lambda i: (jnp.minimum(i + 1, nsteps - 1), 0), memory_space=pltpu.SMEM),
                  pl.BlockSpec((blk, N_PICK), row), pl.BlockSpec((blk, d), row), pl.BlockSpec((blk, d), row),
                  pl.BlockSpec((1, 1, mod3.shape[2]), lambda i: ((i * blk) // tokens_per_seq, 0, 0)),
                  pl.BlockSpec(memory_space=pl.ANY)],
        out_specs=pl.BlockSpec((blk, d), row),
        out_shape=jax.ShapeDtypeStruct((t, d), F32),
        scratch_shapes=[buf, buf, pltpu.SemaphoreType.DMA((tt,)), pltpu.SemaphoreType.DMA((tt,))],
        compiler_params=_cparams("arbitrary"),
        name="peer_experts",
    )(idx, idx, gates, xn2, h1, mod3, uv_words)


def _pack_uv(u, v):
    ub = lax.bitcast_convert_type(u.astype(BF16), jnp.uint16).astype(jnp.uint32)
    vb = lax.bitcast_convert_type(v.astype(BF16), jnp.uint16).astype(jnp.uint32)
    return (ub | (vb << 16)).reshape(u.shape[0], u.shape[1] // 128, 128)


def _rope_inv128():
    half = ROPE_DIM // 2
    inv = ROPE_THETA ** (-(jnp.arange(half, dtype=F32) * 2.0) / ROPE_DIM)
    lane = np.arange(128) % HEAD_DIM
    out = jnp.where(lane < ROPE_DIM, inv[lane % half], 0.0)
    return out.reshape(1, 128)


def kernel(x, c, positions, w_ada, b_ada, norm1_g, w_in, q_norm_g, k_norm_g, rwkv_mu, rwkv_w0, rwkv_w2, rwkv_a0, rwkv_a2, rwkv_g2, rwkv_k_k, rwkv_k_a, rwkv_r_k, rwkv_ln_g, rwkv_ln_b, w_proj_moba, w_proj_rwkv, w_out, norm2_g, peer_wq, peer_subkeys, peer_u, peer_v):
    bsz, seq, d = x.shape
    t = bsz * seq
    tps = seq // TOKEN_TILE
    l = 0
    x2 = x.reshape(t, d)
    mod3 = _adaln(c, w_ada[l], b_ada[l]).reshape(bsz, 1, 6 * d)
    w_in_b = w_in[l].astype(BF16)
    g1 = norm1_g[l].reshape(1, d)
    qkv = _inproj(x2, mod3, g1, w_in_b[:, :3 * WIDTH], tps)
    qb, kb, vb, km = _qkprep(qkv, positions.reshape(t, 1), jnp.tile(q_norm_g[l], N_HEADS).reshape(1, WIDTH),
                             jnp.tile(k_norm_g[l], N_HEADS).reshape(1, WIDTH), _rope_inv128())
    o_moba = _moba(qb, kb, vb, km, bsz, seq)
    params = dict(rwkv_mu=rwkv_mu, rwkv_w0=rwkv_w0, rwkv_w2=rwkv_w2, rwkv_a0=rwkv_a0, rwkv_a2=rwkv_a2,
                  rwkv_g2=rwkv_g2, rwkv_k_k=rwkv_k_k, rwkv_k_a=rwkv_k_a, rwkv_r_k=rwkv_r_k.reshape(1, WIDTH),
                  rwkv_ln_g=rwkv_ln_g, rwkv_ln_b=rwkv_ln_b)
    zrw = _inproj(x2, mod3, g1, w_in_b[:, 3 * WIDTH:3 * WIDTH + SHIFT_WIDTH], tps)
    o_rwkv = _rwkv_branch(zrw, params, l, bsz, seq)
    gl = _inproj(x2, mod3, g1, w_in_b[:, 3 * WIDTH + SHIFT_WIDTH:], tps)
    h1, xn2, q = _merge(x2, o_moba.reshape(t, WIDTH), o_rwkv, gl, mod3, norm2_g[l].reshape(1, d),
                        w_proj_moba[l].astype(BF16), w_proj_rwkv[l].astype(BF16), w_out[l].astype(BF16),
                        peer_wq[l].astype(BF16), tps)
    idx, gates = _peer_topk(q, peer_subkeys[l].reshape(2 * PEER_HEADS, PEER_NKEYS, PEER_HALF))
    out = _peer(idx, gates, xn2, h1, mod3, _pack_uv(peer_u[l], peer_v[l]), seq)
    return out.reshape(bsz, seq, d)
```

```python
import functools

import numpy as np
import jax
import jax.numpy as jnp
from jax import lax
from jax.experimental import pallas as pl
from jax.experimental.pallas import tpu as pltpu

F32 = jnp.float32
BF16 = jnp.bfloat16

HEAD_DIM = 64
N_HEADS = 8
WIDTH = N_HEADS * HEAD_DIM
MOBA_BLOCK = 256
MOBA_TOPK = 3
ROPE_THETA = 500000.0
ROPE_DIM = HEAD_DIM // 4
LORA_W = 64
LORA_A = 64
LORA_G = 128
SHIFT_WIDTH = 3 * WIDTH + LORA_W + LORA_A + LORA_G
GN_EPS = 64e-5
RMS_EPS = 1e-6
NEG_INF = -1e30
PEER_HEADS = 8
PEER_NKEYS = 128
PEER_HALF = 128
PEER_TOPK = 16

TOKEN_TILE = 256
V7X_VMEM_LIMIT = 48 * 1024 * 1024


def _cparams(*sem, flags=None):
    return pltpu.CompilerParams(dimension_semantics=sem, vmem_limit_bytes=V7X_VMEM_LIMIT, flags=flags)


def _dot(a, b):
    return jnp.dot(a.astype(BF16), b.astype(BF16), preferred_element_type=F32)


def _dot_t(a, b):
    return lax.dot_general(a.astype(BF16), b.astype(BF16), (((1,), (1,)), ((), ())),
                           preferred_element_type=F32)


def _split(a):
    hi = a.astype(BF16)
    lo = (a - hi.astype(F32)).astype(BF16)
    return hi, lo


def _split3(a):
    hi = a.astype(BF16)
    r1 = a - hi.astype(F32)
    mid = r1.astype(BF16)
    return hi, mid, (r1 - mid.astype(F32)).astype(BF16)


def _dot3(a, b):
    ah, al = _split(a)
    bh, bl = _split(b)
    return (jnp.dot(ah, bh, preferred_element_type=F32) + jnp.dot(ah, bl, preferred_element_type=F32)
            + jnp.dot(al, bh, preferred_element_type=F32))


def _dot2(a, b_exact):
    ah, al = _split(a)
    return jnp.dot(ah, b_exact, preferred_element_type=F32) + jnp.dot(al, b_exact, preferred_element_type=F32)


def _head_ones(n):
    r = lax.broadcasted_iota(jnp.int32, (n, n), 0) >> 6
    c = lax.broadcasted_iota(jnp.int32, (n, n), 1) >> 6
    return (r == c).astype(BF16)


def _sigmoid(x):
    return 1.0 / (1.0 + jnp.exp(-x))


def _ada_kernel(c_ref, w_ref, b_ref, o_ref):
    c = c_ref[...]
    o_ref[...] = _dot3(c * _sigmoid(c), w_ref[...]) + b_ref[...]


def _adaln(c, w_ada, b_ada):
    bsz, d = c.shape
    n = w_ada.shape[1]
    nb = 1024
    return pl.pallas_call(
        _ada_kernel,
        grid=(n // nb,),
        in_specs=[pl.BlockSpec((bsz, d), lambda j: (0, 0)),
                  pl.BlockSpec((d, nb), lambda j: (0, j)),
                  pl.BlockSpec((1, nb), lambda j: (0, j))],
        out_specs=pl.BlockSpec((bsz, nb), lambda j: (0, j)),
        out_shape=jax.ShapeDtypeStruct((bsz, n), F32),
        compiler_params=_cparams("arbitrary"),
        name="adaln_mod",
    )(c, w_ada, b_ada.reshape(1, n))


def _modnorm(x, g, mod, d, which):
    ms = jnp.mean(x * x, axis=-1, keepdims=True)
    y = x * lax.rsqrt(ms + RMS_EPS) * g
    sh = mod[:, (3 * which) * d:(3 * which + 1) * d]
    sc = mod[:, (3 * which + 1) * d:(3 * which + 2) * d]
    return y * (1.0 + sc) + sh


def _inproj_kernel(x_ref, mod_ref, g_ref, w_ref, o_ref):
    d = x_ref.shape[1]
    xn = _modnorm(x_ref[...], g_ref[...], mod_ref[0], d, 0)
    o_ref[...] = jnp.dot(xn.astype(BF16), w_ref[...], preferred_element_type=F32)


def _inproj(x2, mod3, g, w_bf16, tps):
    t, d = x2.shape
    n = w_bf16.shape[1]
    tm = TOKEN_TILE
    return pl.pallas_call(
        _inproj_kernel,
        grid=(t // tm,),
        in_specs=[pl.BlockSpec((tm, d), lambda i: (i, 0)),
                  pl.BlockSpec((1, 1, mod3.shape[2]), lambda i: (i // tps, 0, 0)),
                  pl.BlockSpec((1, d), lambda i: (0, 0)),
                  pl.BlockSpec((d, n), lambda i: (0, 0))],
        out_specs=pl.BlockSpec((tm, n), lambda i: (i, 0)),
        out_shape=jax.ShapeDtypeStruct((t, n), F32),
        compiler_params=_cparams("parallel"),
        name="in_proj",
    )(x2, mod3, g, w_bf16)


def _qkprep_kernel(qkv_ref, pos_ref, gq_ref, gk_ref, inv_ref, qo_ref, ko_ref, vo_ref, km_ref):
    w = WIDTH
    ones = _head_ones(w)
    ang = pos_ref[...].astype(F32) * inv_ref[...]
    cos = jnp.tile(jnp.cos(ang), (1, w // 128))
    sin = jnp.tile(jnp.sin(ang), (1, w // 128))
    dd = lax.broadcasted_iota(jnp.int32, (1, w), 1) & (HEAD_DIM - 1)
    half = ROPE_DIM // 2
    s_lo = jnp.where(dd < half, -1.0, 0.0) * sin
    s_hi = jnp.where((dd >= half) & (dd < ROPE_DIM), 1.0, 0.0) * sin

    def norm_rope(xh, g):
        ssq = _dot2(xh * xh, ones)
        y = xh * lax.rsqrt(ssq * (1.0 / HEAD_DIM) + RMS_EPS) * g
        return y * cos + pltpu.roll(y, w - half, 1) * s_lo + pltpu.roll(y, half, 1) * s_hi

    q = norm_rope(qkv_ref[:, 0:w], gq_ref[...])
    k = norm_rope(qkv_ref[:, w:2 * w], gk_ref[...])
    qo_ref[...] = (q * (HEAD_DIM ** -0.5)).astype(BF16)
    ko_ref[...] = k.astype(BF16)
    vo_ref[...] = qkv_ref[:, 2 * w:3 * w].astype(BF16)
    km_ref[0] = jnp.mean(k, axis=0, keepdims=True)


def _qkprep(qkv, pos2, gq, gk, inv128):
    t = qkv.shape[0]
    tm = MOBA_BLOCK
    w = WIDTH
    row = lambda i: (i, 0)
    fixed = lambda i: (0, 0)
    return pl.pallas_call(
        _qkprep_kernel,
        grid=(t // tm,),
        in_specs=[pl.BlockSpec((tm, 3 * w), row), pl.BlockSpec((tm, 1), row),
                  pl.BlockSpec((1, w), fixed), pl.BlockSpec((1, w), fixed), pl.BlockSpec((1, 128), fixed)],
        out_specs=[pl.BlockSpec((tm, w), row), pl.BlockSpec((tm, w), row), pl.BlockSpec((tm, w), row),
                   pl.BlockSpec((1, 1, w), lambda i: (i, 0, 0))],
        out_shape=[jax.ShapeDtypeStruct((t, w), BF16), jax.ShapeDtypeStruct((t, w), BF16),
                   jax.ShapeDtypeStruct((t, w), BF16), jax.ShapeDtypeStruct((t // tm, 1, w), F32)],
        compiler_params=_cparams("parallel"),
        name="moba_qk_prep",
    )(qkv, pos2, gq, gk, inv128)


def _moba_kernel(q_ref, k_ref, v_ref, km_ref, o_ref):
    blk = MOBA_BLOCK
    i = pl.program_id(2)
    q = q_ref[0]
    km = km_ref[0]
    nb = km.shape[0]
    lane = lax.broadcasted_iota(jnp.int32, (blk, 128), 1)
    col = lax.broadcasted_iota(jnp.int32, (blk, nb), 1)
    qpos = lax.broadcasted_iota(jnp.int32, (blk, blk), 0)
    kpos = lax.broadcasted_iota(jnp.int32, (blk, blk), 1)
    row0 = pl.multiple_of(i * blk, blk)
    k_own = k_ref[0, pl.ds(row0, blk), :]
    v_own = v_ref[0, pl.ds(row0, blk), :]

    heads = range(2)
    in_head = [(lane >> 6) == hh for hh in heads]
    qs = [jnp.where(m, q, jnp.zeros_like(q)) for m in in_head]
    ones_other = lambda v, hh: jnp.where(in_head[hh], v, jnp.ones_like(v))
    gs = [jnp.where(col < i, _dot_t(qh, km), -jnp.inf) for qh in qs]
    sels = [jnp.zeros((blk, nb), jnp.bool_) for _ in heads]
    for _ in range(MOBA_TOPK):
        ms = [jnp.max(g, axis=-1, keepdims=True) for g in gs]
        idxs = [jnp.min(jnp.where(g == m, col, nb), axis=-1, keepdims=True) for g, m in zip(gs, ms)]
        picks = [(col == ix) & (g > -jnp.inf) for g, ix in zip(gs, idxs)]
        sels = [sl | pk for sl, pk in zip(sels, picks)]
        gs = [jnp.where(pk, -jnp.inf, g) for g, pk in zip(gs, picks)]
    biases = [jnp.where(sl, 0.0, NEG_INF) for sl in sels]
    s0 = [jnp.where(kpos <= qpos, _dot_t(qh, k_own), NEG_INF) for qh in qs]
    m0 = [jnp.max(a, axis=-1, keepdims=True) for a in s0]
    p0 = [jnp.exp(a - m) for a, m in zip(s0, m0)]
    acc0 = [_dot(p, ones_other(v_own, hh)) for hh, p in zip(heads, p0)]
    carry0 = [m0[0], acc0[0], m0[1], acc0[1]]

    def body(jj, carry):
        r0 = pl.multiple_of(jj * (2 * blk), 2 * blk)
        kj = k_ref[0, pl.ds(r0, 2 * blk), :]
        vj = v_ref[0, pl.ds(r0, 2 * blk), :]
        hu = [(hh, u) for hh in heads for u in range(2)]
        raw = [_dot_t(qs[hh], kj[u * blk:(u + 1) * blk]) for hh, u in hu]
        bj = [jnp.sum(jnp.where(col == 2 * jj + u, biases[hh], 0.0), axis=-1, keepdims=True) for hh, u in hu]
        ss = [a + b for a, b in zip(raw, bj)]
        mx = [jnp.max(a, axis=-1, keepdims=True) for a in ss]
        m_new = [jnp.maximum(carry[2 * hh], jnp.maximum(mx[2 * hh], mx[2 * hh + 1])) for hh in heads]
        alpha = [jnp.exp(carry[2 * hh] - m_new[hh]) for hh in heads]
        ps = [jnp.exp(a - m_new[hh]) for a, (hh, u) in zip(ss, hu)]
        pcat = [jnp.concatenate([ps[2 * hh], ps[2 * hh + 1]], axis=1) for hh in heads]
        lane2 = lax.broadcasted_iota(jnp.int32, (2 * blk, 128), 1)
        pv = [_dot(pc, jnp.where((lane2 >> 6) == hh, vj, jnp.ones_like(vj))) for hh, pc in zip(heads, pcat)]
        out = []
        for hh in heads:
            out += [m_new[hh], alpha[hh] * carry[2 * hh + 1] + pv[hh]]
        return tuple(out)

    fin = lax.fori_loop(0, (i + 1) // 2, body, tuple(carry0))
    outs = [fin[2 * hh + 1] / pltpu.roll(fin[2 * hh + 1], HEAD_DIM, 1) for hh in heads]
    o_ref[0] = jnp.where(lane < HEAD_DIM, outs[0], outs[1]).astype(o_ref.dtype)


def _moba(qb, kb, vb, km, bsz, seq):
    w = WIDTH
    nb = seq // MOBA_BLOCK
    q3 = qb.reshape(bsz, seq, w)
    k3 = kb.reshape(bsz, seq, w)
    v3 = vb.reshape(bsz, seq, w)
    km3 = km.reshape(bsz, nb, w)
    return pl.pallas_call(
        _moba_kernel,
        grid=(bsz, w // 128, nb),
        in_specs=[pl.BlockSpec((1, MOBA_BLOCK, 128), lambda b, p, i: (b, i, p)),
                  pl.BlockSpec((1, seq, 128), lambda b, p, i: (b, 0, p)),
                  pl.BlockSpec((1, seq, 128), lambda b, p, i: (b, 0, p)),
                  pl.BlockSpec((1, nb, 128), lambda b, p, i: (b, 0, p))],
        out_specs=pl.BlockSpec((1, MOBA_BLOCK, 128), lambda b, p, i: (b, i, p)),
        out_shape=jax.ShapeDtypeStruct((bsz, seq, w), BF16),
        compiler_params=_cparams("parallel", "parallel", "arbitrary"),
        name="moba_attention",
    )(q3, k3, v3, km3)


def _rwkvprep_kernel(z_ref, zp_ref, mu_ref, w0_ref, w2_ref, a0_ref, a2_ref, g2_ref, kk_ref, ka_ref, rk_ref,
                     r_o, lw_o, k_o, v_o, kn_o, b_o, bonus_o, g_o, *, tps):
    i = pl.program_id(0)
    w = WIDTH
    z = z_ref[...]
    tm = z.shape[0]
    prev_last = jnp.where((i % tps) == 0, 0.0, zp_ref[7:8, :])
    rows = lax.broadcasted_iota(jnp.int32, (tm, 1), 0)
    prev = jnp.where(rows == 0, prev_last, pltpu.roll(z, 1, 0))
    zs = z + (prev - z) * mu_ref[...]
    zr, zk, zv = zs[:, 0:w], zs[:, w:2 * w], zs[:, 2 * w:3 * w]
    zwa = zs[:, 3 * w:3 * w + 128]
    zg = zs[:, 3 * w + 128:3 * w + 256]
    nw = -(w0_ref[...] + _dot3(jnp.tanh(zwa), w2_ref[...]))
    softplus = jnp.maximum(nw, 0.0) + jnp.log(1.0 + jnp.exp(-jnp.abs(nw)))
    lw_o[...] = -jnp.exp(-softplus - 0.5)
    a = _sigmoid(a0_ref[...] + _dot3(zwa, a2_ref[...]))
    g_o[...] = _dot3(_sigmoid(zg), g2_ref[...])
    ones = _head_ones(w)
    kn = zk * kk_ref[...]
    kn = kn / jnp.maximum(jnp.sqrt(_dot2(kn * kn, ones)), 1e-12)
    k = zk * (1.0 + (a - 1.0) * ka_ref[...])
    r_o[...] = zr
    k_o[...] = k
    v_o[...] = zv
    kn_o[...] = kn
    b_o[...] = kn * a
    bonus_o[...] = _dot2(zr * k * rk_ref[...], ones) * zv


def _rwkvprep(zrw, tps, mu, w0, w2aug, a0, a2aug, g2, k_k, k_a, r_k):
    t = zrw.shape[0]
    tm = TOKEN_TILE
    w = WIDTH
    row = lambda i: (i, 0)
    fixed = lambda i: (0, 0)
    vec = pl.BlockSpec((1, w), fixed)
    outs = [jax.ShapeDtypeStruct((t, w), F32)] * 8
    return pl.pallas_call(
        functools.partial(_rwkvprep_kernel, tps=tps),
        grid=(t // tm,),
        in_specs=[pl.BlockSpec((tm, SHIFT_WIDTH), row),
                  pl.BlockSpec((8, SHIFT_WIDTH), lambda i: (jnp.maximum(i * (tm // 8) - 1, 0), 0)),
                  pl.BlockSpec((1, SHIFT_WIDTH), fixed),
                  vec, pl.BlockSpec((128, w), fixed), vec, pl.BlockSpec((128, w), fixed),
                  pl.BlockSpec((LORA_G, w), fixed), vec, vec, vec],
        out_specs=[pl.BlockSpec((tm, w), row)] * 8,
        out_shape=outs,
        compiler_params=_cparams("parallel"),
        name="rwkv_prep",
    )(zrw, zrw, mu, w0, w2aug, a0, a2aug, g2, k_k, k_a, r_k)


RWKV_CHUNK = 32


SCAN_GROUP = 4
SCAN_CHUNKS_PER_STEP = 8


def _scan_chunks(probs):
    cl, gw = probs[0][0].shape
    hc = SCAN_GROUP * cl
    bdot = lambda a, b: jnp.dot(a, b, preferred_element_type=F32)
    tn = lambda a, b: lax.dot_general(a, b, (((0,), (0,)), ((), ())), preferred_element_type=F32)
    tri = (lax.broadcasted_iota(jnp.int32, (cl, cl), 0) >= lax.broadcasted_iota(jnp.int32, (cl, cl), 1)).astype(BF16)
    hmask = (lax.broadcasted_iota(jnp.int32, (hc, gw), 0) // cl) == (lax.broadcasted_iota(jnp.int32, (hc, gw), 1) >> 6)
    stack = lambda a: jnp.where(hmask, jnp.tile(a, (SCAN_GROUP, 1)), 0.0).astype(BF16)
    unstack = lambda a: functools.reduce(lambda p, q: p + q, [a[h * cl:(h + 1) * cl] for h in range(SCAN_GROUP)])
    ri = lax.broadcasted_iota(jnp.int32, (hc, hc), 0)
    ci = lax.broadcasted_iota(jnp.int32, (hc, hc), 1)
    strict = ri > ci
    incl = ri >= ci
    eye = jnp.where(ri == ci, 1.0, 0.0)
    di = lax.broadcasted_iota(jnp.int32, (gw, gw), 0) == lax.broadcasted_iota(jnp.int32, (gw, gw), 1)

    cs2 = [bdot(tri, jnp.concatenate(_split(p[1]), axis=1)) for p in probs]
    ops = []
    for (r, lw, k, v, kn, b), c2 in zip(probs, cs2):
        cs = c2[:, :gw] + c2[:, gw:]
        cs_end = cs[cl - 1:cl, :]
        inv = jnp.exp(-cs)
        dec_end = jnp.exp(cs_end - cs)
        ops.append(dict(khs=stack(kn * jnp.exp(cs - lw)),
                        rhs=stack(r * jnp.exp(cs)),
                        bts=stack(b * inv), kts=stack(k * inv), vs=stack(v),
                        btes=stack(b * dec_end), ktes=stack(k * dec_end), gend=jnp.exp(cs_end)))
    grams = [lax.dot_general(jnp.concatenate([o['khs'], o['rhs']], axis=0),
                             jnp.concatenate([o['bts'], o['kts']], axis=0),
                             (((1,), (1,)), ((), ())), preferred_element_type=F32) for o in ops]
    for o, g in zip(ops, grams):
        lb = jnp.where(strict, g[:hc, :hc], 0.0)
        o['lkak'] = jnp.concatenate([jnp.where(strict, g[:hc, hc:], 0.0),
                                     jnp.where(incl, g[hc:, hc:], 0.0)], axis=0).astype(BF16)
        o['ab'] = jnp.where(incl, g[hc:, :hc], 0.0).astype(BF16)
        o['tinv'] = eye - lb
        o['pw'] = (-lb).astype(BF16)
    sq = [bdot(o['pw'], o['pw']) for o in ops]
    for o, q in zip(ops, sq):
        o['pw'] = q.astype(BF16)
    levels = int(np.log2(cl)) - 1
    for lvl in range(levels):
        both = [bdot(jnp.concatenate([o['pw'], o['tinv'].astype(BF16)], axis=0), o['pw']) for o in ops]
        for o, bo in zip(ops, both):
            o['tinv'] = o['tinv'] + bo[hc:]
            o['pw'] = bo[:hc].astype(BF16)
    lkak = [bdot(o['lkak'], o['vs']) for o in ops]
    wu = [bdot(o['tinv'].astype(BF16), jnp.concatenate([o['khs'], lv[:hc].astype(BF16)], axis=1))
          for o, lv in zip(ops, lkak)]
    wub = [jnp.concatenate([w[:, :gw], -w[:, gw:]], axis=1).astype(BF16) for w in wu]
    abw = [bdot(o['ab'], w) for o, w in zip(ops, wub)]
    gmat = [tn(o['btes'], w[:, :gw]) for o, w in zip(ops, wub)]
    hv = [tn(jnp.concatenate([o['btes'], o['ktes']], axis=0), jnp.concatenate([w[:, gw:], o['vs']], axis=0))
          for o, w in zip(ops, wub)]
    out = []
    for o, aw, lv, gm, h_v in zip(ops, abw, lkak, gmat, hv):
        qt = unstack(o['rhs'].astype(F32) - aw[:, :gw])
        yv = unstack(aw[:, gw:] + lv[hc:])
        out.append((qt, yv, jnp.where(di, o['gend'], 0.0) - gm, h_v))
    return out


def _scan_kernel(r_ref, lw_ref, k_ref, v_ref, kn_ref, b_ref, bonus_ref, g_ref, lng_ref, lnb_ref, o_ref, h_ref):
    gw = SCAN_GROUP * HEAD_DIM
    cl = RWKV_CHUNK
    ngrp = N_HEADS // SCAN_GROUP

    @pl.when(pl.program_id(1) == 0)
    def _():
        h_ref[...] = jnp.zeros_like(h_ref)

    where = [(grp, c) for grp in range(ngrp) for c in range(SCAN_CHUNKS_PER_STEP)]
    sl = lambda grp, c: (slice(c * cl, (c + 1) * cl), slice(grp * gw, (grp + 1) * gw))
    parts = _scan_chunks([tuple(ref[sl(grp, c)] for ref in (r_ref, lw_ref, k_ref, v_ref, kn_ref, b_ref))
                          for grp, c in where])
    ones = _head_ones(gw)
    bdot = lambda a, b: jnp.dot(a, b, preferred_element_type=F32)
    for grp in range(ngrp):
        h = h_ref[grp]
        for c in range(SCAN_CHUNKS_PER_STEP):
            rows, ln = sl(grp, c)
            qt, yv, abar, h_v = parts[where.index((grp, c))]
            hh, hl = _split(h)
            y = bdot(qt.astype(BF16), hh) + yv
            ah, al = _split(abar)
            ahl = bdot(jnp.concatenate([ah, al], axis=0), hh)
            h = ahl[:gw] + ahl[gw:] + bdot(ah, hl) + h_v
            mean = _dot2(y, ones) * (1.0 / HEAD_DIM)
            yc = y - mean
            var = _dot2(yc * yc, ones) * (1.0 / HEAD_DIM)
            yn = yc * lax.rsqrt(var + GN_EPS) * lng_ref[:, ln] + lnb_ref[:, ln] + bonus_ref[rows, ln]
            o_ref[rows, ln] = (yn * g_ref[rows, ln]).astype(o_ref.dtype)
        h_ref[grp] = h


def _rwkv_scan(r, lw, k, v, kn, b, bonus, g, ln_g, ln_b, bsz, seq):
    t, w = r.shape
    cl = RWKV_CHUNK * SCAN_CHUNKS_PER_STEP
    cps = seq // cl
    row = pl.BlockSpec((cl, w), lambda bi, c: (bi * cps + c, 0))
    vec = pl.BlockSpec((1, w), lambda bi, c: (0, 0))
    return pl.pallas_call(
        _scan_kernel,
        grid=(bsz, cps),
        in_specs=[row] * 8 + [vec, vec],
        out_specs=row,
        out_shape=jax.ShapeDtypeStruct((t, w), BF16),
        scratch_shapes=[pltpu.VMEM((N_HEADS // SCAN_GROUP, SCAN_GROUP * HEAD_DIM, SCAN_GROUP * HEAD_DIM), F32)],
        compiler_params=_cparams("parallel", "arbitrary"),
        name="rwkv_scan",
    )(r, lw, k, v, kn, b, bonus, g, ln_g, ln_b)


def _rwkv_branch(zrw, p, l, bsz, seq):
    w = WIDTH
    vec = lambda a: a.reshape(1, -1)
    zpad = lambda a, top: jnp.concatenate([a, jnp.zeros_like(a)] if top else [jnp.zeros_like(a), a], axis=0)
    outs = _rwkvprep(zrw, seq // TOKEN_TILE, vec(p['rwkv_mu'][l]), vec(p['rwkv_w0'][l]),
                     zpad(p['rwkv_w2'][l], True), vec(p['rwkv_a0'][l]), zpad(p['rwkv_a2'][l], False),
                     p['rwkv_g2'][l], vec(p['rwkv_k_k'][l]), vec(p['rwkv_k_a'][l]), vec(p['rwkv_r_k'][l]))
    return _rwkv_scan(*outs, vec(p['rwkv_ln_g'][l]), vec(p['rwkv_ln_b'][l]), bsz, seq)


def _merge_kernel(x_ref, om_ref, or_ref, gl_ref, mod_ref, g2_ref, wpm_ref, wpr_ref, wo_ref, wq_ref,
                  h_o, xn_o, q_o):
    d = x_ref.shape[1]
    mod = mod_ref[0]
    pm = jnp.dot(om_ref[...], wpm_ref[...], preferred_element_type=F32)
    pr = jnp.dot(or_ref[...], wpr_ref[...], preferred_element_type=F32)
    gl = gl_ref[...]
    mixed = _sigmoid(gl[:, :d]) * pm + _sigmoid(gl[:, d:]) * pr
    h = x_ref[...] + mod[:, 2 * d:3 * d] * _dot(mixed, wo_ref[...])
    h_o[...] = h
    xn2 = _modnorm(h, g2_ref[...], mod, d, 1)
    xn_o[...] = xn2
    q_o[...] = _dot(xn2, wq_ref[...])


def _merge(x2, o_moba, o_rwkv, gl, mod3, g2, wpm, wpr, wo, wq, tps):
    t, d = x2.shape
    tm = TOKEN_TILE
    w = WIDTH
    nq = wq.shape[1]
    row = lambda i: (i, 0)
    fixed = lambda i: (0, 0)
    return pl.pallas_call(
        _merge_kernel,
        grid=(t // tm,),
        in_specs=[pl.BlockSpec((tm, d), row), pl.BlockSpec((tm, w), row), pl.BlockSpec((tm, w), row),
                  pl.BlockSpec((tm, 2 * d), row),
                  pl.BlockSpec((1, 1, mod3.shape[2]), lambda i: (i // tps, 0, 0)),
                  pl.BlockSpec((1, d), fixed), pl.BlockSpec((w, d), fixed), pl.BlockSpec((w, d), fixed),
                  pl.BlockSpec((d, d), fixed), pl.BlockSpec((d, nq), fixed)],
        out_specs=[pl.BlockSpec((tm, d), row), pl.BlockSpec((tm, d), row), pl.BlockSpec((tm, nq), row)],
        out_shape=[jax.ShapeDtypeStruct((t, d), F32), jax.ShapeDtypeStruct((t, d), F32),
                   jax.ShapeDtypeStruct((t, nq), F32)],
        compiler_params=_cparams("parallel"),
        name="merge_proj",
    )(x2, o_moba, o_rwkv, gl, mod3, g2, wpm, wpr, wo, wq)


PEER_TILE = 128


def _top16_rows(problems):
    tm = problems[0][0].shape[1]
    rank = lax.broadcasted_iota(jnp.int32, (PEER_TOPK, tm), 0)
    keys = [k for _, k in problems]
    big = jnp.int32(1 << 30)

    def rnd(r, carry):
        out = []
        hit = rank == r
        for (s, vals, idxs), key in zip(carry, keys):
            m = jnp.max(s, axis=0, keepdims=True)
            ix = jnp.min(jnp.where(s == m, key, big), axis=0, keepdims=True)
            out.append((jnp.where(key == ix, -jnp.inf, s), jnp.where(hit, m, vals),
                        jnp.where(hit, ix.astype(F32), idxs)))
        return tuple(out)

    z = jnp.zeros((PEER_TOPK, tm), F32)
    fin = lax.fori_loop(0, PEER_TOPK, rnd, tuple((s, z, z) for s, _ in problems))
    return [(vals, idxs) for _, vals, idxs in fin]


def _peer_topk_kernel(q_ref, sk_ref, idx_o, gate_o):
    tm = q_ref.shape[0]
    kk = PEER_TOPK
    nk = PEER_NKEYS
    rank = lax.broadcasted_iota(jnp.int32, (kk, tm), 0)
    pos_key = lax.broadcasted_iota(jnp.int32, (nk, tm), 0)
    row8 = lax.broadcasted_iota(jnp.int32, (8, tm), 0)
    tdot = lambda a, b: lax.dot_general(a, b, (((1,), (1,)), ((), ())), preferred_element_type=F32)
    cand_key = jnp.concatenate([rank] + [row8 + a * kk for a in range(1, 8)] + [(row8 + 8) * kk], axis=0)

    tops = []
    for h0 in range(0, PEER_HEADS, 2):
        probs = []
        for hp in range(2 * h0, 2 * h0 + 4):
            qh, ql = _split(q_ref[:, hp * PEER_HALF:(hp + 1) * PEER_HALF])
            kh, kl = _split(sk_ref[hp])
            probs.append((tdot(kh, qh) + tdot(kh, ql) + tdot(kl, qh), pos_key))
        res = _top16_rows(probs)
        tops += [res[0:2], res[2:4]]

    idx_rows, gate_rows = [], []
    for h0 in range(0, PEER_HEADS, 2):
        probs = []
        for h in (h0, h0 + 1):
            (ts0, _), (ts1, _) = tops[h]
            cand = jnp.concatenate([ts0[0:1] + ts1] + [ts0[a:a + 1] + ts1[0:8] for a in range(1, 8)]
                                   + [ts0[8:16] + ts1[0:1]], axis=0)
            probs.append((cand, cand_key))
        for h, (best, pos) in zip((h0, h0 + 1), _top16_rows(probs)):
            (_, ti0), (_, ti1) = tops[h]
            pos = pos.astype(jnp.int32)
            ids = jnp.zeros((kk, tm), F32)
            for r in range(kk):
                pr = pos[r:r + 1, :]
                i0 = jnp.sum(jnp.where(rank == (pr >> 4), ti0, 0.0), axis=0, keepdims=True)
                i1 = jnp.sum(jnp.where(rank == (pr & (kk - 1)), ti1, 0.0), axis=0, keepdims=True)
                ids = jnp.where(rank == r, i0 * float(nk) + i1, ids)
            e = jnp.exp(best - jnp.max(best, axis=0, keepdims=True))
            gate_rows.append(e / jnp.sum(e, axis=0, keepdims=True))
            idx_rows.append(ids)
    idx_o[...] = jnp.concatenate(idx_rows, axis=0).T.astype(jnp.int32)
    gate_o[...] = jnp.concatenate(gate_rows, axis=0).T


def _peer_topk(q, subkeys16):
    t, nq = q.shape
    tm = PEER_TILE
    return pl.pallas_call(
        _peer_topk_kernel,
        grid=(t // tm,),
        in_specs=[pl.BlockSpec((tm, nq), lambda i: (i, 0)),
                  pl.BlockSpec(subkeys16.shape, lambda i: (0, 0, 0))],
        out_specs=[pl.BlockSpec((tm, 128), lambda i: (i, 0)), pl.BlockSpec((tm, 128), lambda i: (i, 0))],
        out_shape=[jax.ShapeDtypeStruct((t, 128), jnp.int32), jax.ShapeDtypeStruct((t, 128), F32)],
        compiler_params=_cparams("parallel"),
        name="peer_topk",
    )(q, subkeys16)


GATHER_TOKENS = 16
N_PICK = PEER_HEADS * PEER_TOPK
N_BURST = 4


def _peer_kernel(idx_cur, idx_nxt, gate_ref, xn_ref, h_ref, mod_ref, uv_hbm, o_ref, buf_a, buf_b, sem_a, sem_b):
    i = pl.program_id(0)
    n = pl.num_programs(0)
    tt = GATHER_TOKENS
    d = h_ref.shape[1]
    gt2 = mod_ref[0][:, 5 * d:6 * d]
    bufs = (buf_a, buf_b)
    sems = (sem_a, sem_b)
    per = N_PICK // N_BURST

    def row_copy(idx_ref, row, half, tok, e):
        return pltpu.make_async_copy(uv_hbm.at[idx_ref[row, e]], bufs[half].at[tok, pl.ds(e * 8, 8), :],
                                     sems[half].at[tok])

    def burst(idx_ref, row0, half, toks, b):
        for k, tok in enumerate(toks):
            for e in range(b * per, (b + 1) * per):
                row_copy(idx_ref, row0 + k, half, tok, e).start(priority=e % 2)

    def pair(half, tok0, idx_next, next_row0):
        toks = (tok0, tok0 + 1)
        rows = [half * tt + t for t in toks]
        for t, row in zip(toks, rows):
            for e in range(N_PICK):
                row_copy(idx_cur, row, half, t, e).wait()
        nch = d // 128
        words = [[bufs[half][t, pl.ds(c, N_PICK, stride=nch), :] for c in range(nch)] for t in toks]
        xts = [xn_ref[row:row + 1, :].astype(BF16) for row in rows]
        gates = [gate_ref[row:row + 1, :] for row in rows]
        hrows = [h_ref[row:row + 1, :] for row in rows]
        refill = lambda b: burst(idx_next, next_row0, 1 - half, toks, b)
        refill(0)
        us = [[lax.bitcast_convert_type(w << 16, F32).astype(BF16) for w in ws] for ws in words]
        refill(1)
        tdot = lambda a, b: lax.dot_general(a, b, (((1,), (1,)), ((), ())), preferred_element_type=F32)
        hds = [functools.reduce(lambda p, q: p + q, [tdot(x[:, c * 128:(c + 1) * 128], u[c]) for c in range(nch)])
               for x, u in zip(xts, us)]
        vs = [[lax.bitcast_convert_type(w & jnp.uint32(0xFFFF0000), F32).astype(BF16) for w in ws] for ws in words]
        refill(2)
        hds = [(0.5 * h * (1.0 + lax.erf(h * (2.0 ** -0.5))) * g).astype(BF16) for h, g in zip(hds, gates)]
        outs = [jnp.concatenate([jnp.dot(h, v[c], preferred_element_type=F32) for c in range(nch)], axis=1)
                for h, v in zip(hds, vs)]
        refill(3)
        for row, hr, out in zip(rows, hrows, outs):
            o_ref[row:row + 1, :] = hr + gt2 * out

    @pl.when(i == 0)
    def _():
        for tok in range(tt):
            for e in range(N_PICK):
                row_copy(idx_cur, tok, 0, tok, e).start(priority=e % 2)

    for tok0 in range(0, tt, 2):
        pair(0, tok0, idx_cur, tt + tok0)
    for tok0 in range(0, tt, 2):
        pair(1, tok0, idx_nxt, tok0)

    @pl.when(i == n - 1)
    def _():
        for tok in range(tt):
            for e in range(N_PICK):
                row_copy(idx_nxt, tok, 0, tok, e).wait()


def _peer(idx, gates, xn2, h1, mod3, uv_words, tokens_per_seq):
    t, d = h1.shape
    tt = GATHER_TOKENS
    blk = 2 * tt
    nsteps = t // blk
    row = lambda i: (i, 0)
    buf = pltpu.VMEM((tt, N_PICK * (d // 128), 128), jnp.uint32)
    return pl.pallas_call(
        _peer_kernel,
        grid=(nsteps,),
        in_specs=[pl.BlockSpec((blk, N_PICK), row, memory_space=pltpu.SMEM),
                  pl.BlockSpec((blk, N_PICK), lambda i: (jnp.minimum(i + 1, nsteps - 1), 0), memory_space=pltpu.SMEM),
                  pl.BlockSpec((blk, N_PICK), row), pl.BlockSpec((blk, d), row), pl.BlockSpec((blk, d), row),
                  pl.BlockSpec((1, 1, mod3.shape[2]), lambda i: ((i * blk) // tokens_per_seq, 0, 0)),
                  pl.BlockSpec(memory_space=pl.ANY)],
        out_specs=pl.BlockSpec((blk, d), row),
        out_shape=jax.ShapeDtypeStruct((t, d), F32),
        scratch_shapes=[buf, buf, pltpu.SemaphoreType.DMA((tt,)), pltpu.SemaphoreType.DMA((tt,))],
        compiler_params=_cparams("arbitrary"),
        name="peer_experts",
    )(idx, idx, gates, xn2, h1, mod3, uv_words)


def _pack_uv(u, v):
    ub = lax.bitcast_convert_type(u.astype(BF16), jnp.uint16).astype(jnp.uint32)
    vb = lax.bitcast_convert_type(v.astype(BF16), jnp.uint16).astype(jnp.uint32)
    return (ub | (vb << 16)).reshape(u.shape[0], u.shape[1] // 128, 128)


def _rope_inv128():
    half = ROPE_DIM // 2
    inv = ROPE_THETA ** (-(jnp.arange(half, dtype=F32) * 2.0) / ROPE_DIM)
    lane = np.arange(128) % HEAD_DIM
    out = jnp.where(lane < ROPE_DIM, inv[lane % half], 0.0)
    return out.reshape(1, 128)


def kernel(x, c, positions, w_ada, b_ada, norm1_g, w_in, q_norm_g, k_norm_g, rwkv_mu, rwkv_w0, rwkv_w2, rwkv_a0, rwkv_a2, rwkv_g2, rwkv_k_k, rwkv_k_a, rwkv_r_k, rwkv_ln_g, rwkv_ln_b, w_proj_moba, w_proj_rwkv, w_out, norm2_g, peer_wq, peer_subkeys, peer_u, peer_v):
    bsz, seq, d = x.shape
    t = bsz * seq
    tps = seq // TOKEN_TILE
    l = 0
    x2 = x.reshape(t, d)
    mod3 = _adaln(c, w_ada[l], b_ada[l]).reshape(bsz, 1, 6 * d)
    w_in_b = w_in[l].astype(BF16)
    g1 = norm1_g[l].reshape(1, d)
    qkv = _inproj(x2, mod3, g1, w_in_b[:, :3 * WIDTH], tps)
    qb, kb, vb, km = _qkprep(qkv, positions.reshape(t, 1), jnp.tile(q_norm_g[l], N_HEADS).reshape(1, WIDTH),
                             jnp.tile(k_norm_g[l], N_HEADS).reshape(1, WIDTH), _rope_inv128())
    o_moba = _moba(qb, kb, vb, km, bsz, seq)
    params = dict(rwkv_mu=rwkv_mu, rwkv_w0=rwkv_w0, rwkv_w2=rwkv_w2, rwkv_a0=rwkv_a0, rwkv_a2=rwkv_a2,
                  rwkv_g2=rwkv_g2, rwkv_k_k=rwkv_k_k, rwkv_k_a=rwkv_k_a, rwkv_r_k=rwkv_r_k.reshape(1, WIDTH),
                  rwkv_ln_g=rwkv_ln_g, rwkv_ln_b=rwkv_ln_b)
    zrw = _inproj(x2, mod3, g1, w_in_b[:, 3 * WIDTH:3 * WIDTH + SHIFT_WIDTH], tps)
    o_rwkv = _rwkv_branch(zrw, params, l, bsz, seq)
    gl = _inproj(x2, mod3, g1, w_in_b[:, 3 * WIDTH + SHIFT_WIDTH:], tps)
    h1, xn2, q = _merge(x2, o_moba.reshape(t, WIDTH), o_rwkv, gl, mod3, norm2_g[l].reshape(1, d),
                        w_proj_moba[l].astype(BF16), w_proj_rwkv[l].astype(BF16), w_out[l].astype(BF16),
                        peer_wq[l].astype(BF16), tps)
    idx, gates = _peer_topk(q, peer_subkeys[l].reshape(2 * PEER_HEADS, PEER_NKEYS, PEER_HALF))
    out = _peer(idx, gates, xn2, h1, mod3, _pack_uv(peer_u[l], peer_v[l]), seq)
    return out.reshape(bsz, seq, d)
```

```python
import functools

import numpy as np
import jax
import jax.numpy as jnp
from jax import lax
from jax.experimental import pallas as pl
from jax.experimental.pallas import tpu as pltpu

F32 = jnp.float32
BF16 = jnp.bfloat16

HEAD_DIM = 64
N_HEADS = 8
WIDTH = N_HEADS * HEAD_DIM
MOBA_BLOCK = 256
MOBA_TOPK = 3
ROPE_THETA = 500000.0
ROPE_DIM = HEAD_DIM // 4
LORA_W = 64
LORA_A = 64
LORA_G = 128
SHIFT_WIDTH = 3 * WIDTH + LORA_W + LORA_A + LORA_G
GN_EPS = 64e-5
RMS_EPS = 1e-6
NEG_INF = -1e30
LOG2_E = 1.4426950408889634
PEER_HEADS = 8
PEER_NKEYS = 128
PEER_HALF = 128
PEER_TOPK = 16

TOKEN_TILE = 256
V7X_VMEM_LIMIT = 48 * 1024 * 1024


def _cparams(*sem, flags=None):
    return pltpu.CompilerParams(dimension_semantics=sem, vmem_limit_bytes=V7X_VMEM_LIMIT, flags=flags)


def _dot(a, b):
    return jnp.dot(a.astype(BF16), b.astype(BF16), preferred_element_type=F32)


def _dot_t(a, b):
    return lax.dot_general(a.astype(BF16), b.astype(BF16), (((1,), (1,)), ((), ())),
                           preferred_element_type=F32)


def _split(a):
    hi = a.astype(BF16)
    lo = (a - hi.astype(F32)).astype(BF16)
    return hi, lo


def _split3(a):
    hi = a.astype(BF16)
    r1 = a - hi.astype(F32)
    mid = r1.astype(BF16)
    return hi, mid, (r1 - mid.astype(F32)).astype(BF16)


def _dot3(a, b):
    ah, al = _split(a)
    bh, bl = _split(b)
    return (jnp.dot(ah, bh, preferred_element_type=F32) + jnp.dot(ah, bl, preferred_element_type=F32)
            + jnp.dot(al, bh, preferred_element_type=F32))


def _dot2(a, b_exact):
    ah, al = _split(a)
    return jnp.dot(ah, b_exact, preferred_element_type=F32) + jnp.dot(al, b_exact, preferred_element_type=F32)


def _head_ones(n):
    r = lax.broadcasted_iota(jnp.int32, (n, n), 0) >> 6
    c = lax.broadcasted_iota(jnp.int32, (n, n), 1) >> 6
    return (r == c).astype(BF16)


def _sigmoid(x):
    return 1.0 / (1.0 + jnp.exp(-x))


def _ada_kernel(c_ref, w_ref, b_ref, o_ref):
    c = c_ref[...]
    o_ref[...] = _dot3(c * _sigmoid(c), w_ref[...]) + b_ref[...]


def _adaln(c, w_ada, b_ada):
    bsz, d = c.shape
    n = w_ada.shape[1]
    nb = 1024
    return pl.pallas_call(
        _ada_kernel,
        grid=(n // nb,),
        in_specs=[pl.BlockSpec((bsz, d), lambda j: (0, 0)),
                  pl.BlockSpec((d, nb), lambda j: (0, j)),
                  pl.BlockSpec((1, nb), lambda j: (0, j))],
        out_specs=pl.BlockSpec((bsz, nb), lambda j: (0, j)),
        out_shape=jax.ShapeDtypeStruct((bsz, n), F32),
        compiler_params=_cparams("arbitrary"),
        name="adaln_mod",
    )(c, w_ada, b_ada.reshape(1, n))


def _modnorm(x, g, mod, d, which):
    ms = jnp.mean(x * x, axis=-1, keepdims=True)
    y = x * lax.rsqrt(ms + RMS_EPS) * g
    sh = mod[:, (3 * which) * d:(3 * which + 1) * d]
    sc = mod[:, (3 * which + 1) * d:(3 * which + 2) * d]
    return y * (1.0 + sc) + sh


def _inproj_kernel(x_ref, mod_ref, g_ref, w_ref, o_ref):
    d = x_ref.shape[1]
    xn = _modnorm(x_ref[...], g_ref[...], mod_ref[0], d, 0)
    o_ref[...] = jnp.dot(xn.astype(BF16), w_ref[...], preferred_element_type=F32)


def _inproj(x2, mod3, g, w_bf16, tps):
    t, d = x2.shape
    n = w_bf16.shape[1]
    tm = TOKEN_TILE
    return pl.pallas_call(
        _inproj_kernel,
        grid=(t // tm,),
        in_specs=[pl.BlockSpec((tm, d), lambda i: (i, 0)),
                  pl.BlockSpec((1, 1, mod3.shape[2]), lambda i: (i // tps, 0, 0)),
                  pl.BlockSpec((1, d), lambda i: (0, 0)),
                  pl.BlockSpec((d, n), lambda i: (0, 0))],
        out_specs=pl.BlockSpec((tm, n), lambda i: (i, 0)),
        out_shape=jax.ShapeDtypeStruct((t, n), F32),
        compiler_params=_cparams("parallel"),
        name="in_proj",
    )(x2, mod3, g, w_bf16)


def _qkprep_kernel(qkv_ref, pos_ref, gq_ref, gk_ref, inv_ref, qo_ref, ko_ref, vo_ref, km_ref):
    w = WIDTH
    ones = _head_ones(w)
    ang = pos_ref[...].astype(F32) * inv_ref[...]
    cos = jnp.tile(jnp.cos(ang), (1, w // 128))
    sin = jnp.tile(jnp.sin(ang), (1, w // 128))
    dd = lax.broadcasted_iota(jnp.int32, (1, w), 1) & (HEAD_DIM - 1)
    half = ROPE_DIM // 2
    s_lo = jnp.where(dd < half, -1.0, 0.0) * sin
    s_hi = jnp.where((dd >= half) & (dd < ROPE_DIM), 1.0, 0.0) * sin

    def norm_rope(xh, g):
        ssq = _dot2(xh * xh, ones)
        y = xh * lax.rsqrt(ssq * (1.0 / HEAD_DIM) + RMS_EPS) * g
        return y * cos + pltpu.roll(y, w - half, 1) * s_lo + pltpu.roll(y, half, 1) * s_hi

    q = norm_rope(qkv_ref[:, 0:w], gq_ref[...])
    k = norm_rope(qkv_ref[:, w:2 * w], gk_ref[...])
    qo_ref[...] = (q * (HEAD_DIM ** -0.5 * LOG2_E)).astype(BF16)
    ko_ref[...] = k.astype(BF16)
    vo_ref[...] = qkv_ref[:, 2 * w:3 * w].astype(BF16)
    km_ref[0] = jnp.mean(k, axis=0, keepdims=True)


def _qkprep(qkv, pos2, gq, gk, inv128):
    t = qkv.shape[0]
    tm = MOBA_BLOCK
    w = WIDTH
    row = lambda i: (i, 0)
    fixed = lambda i: (0, 0)
    return pl.pallas_call(
        _qkprep_kernel,
        grid=(t // tm,),
        in_specs=[pl.BlockSpec((tm, 3 * w), row), pl.BlockSpec((tm, 1), row),
                  pl.BlockSpec((1, w), fixed), pl.BlockSpec((1, w), fixed), pl.BlockSpec((1, 128), fixed)],
        out_specs=[pl.BlockSpec((tm, w), row), pl.BlockSpec((tm, w), row), pl.BlockSpec((tm, w), row),
                   pl.BlockSpec((1, 1, w), lambda i: (i, 0, 0))],
        out_shape=[jax.ShapeDtypeStruct((t, w), BF16), jax.ShapeDtypeStruct((t, w), BF16),
                   jax.ShapeDtypeStruct((t, w), BF16), jax.ShapeDtypeStruct((t // tm, 1, w), F32)],
        compiler_params=_cparams("parallel"),
        name="moba_qk_prep",
    )(qkv, pos2, gq, gk, inv128)


def _moba_kernel(q_ref, k_ref, v_ref, km_ref, o_ref):
    blk = MOBA_BLOCK
    i = pl.program_id(2)
    q = q_ref[0]
    km = km_ref[0]
    nbp = km.shape[0]
    lane = lax.broadcasted_iota(jnp.int32, (blk, 128), 1)
    col = lax.broadcasted_iota(jnp.int32, (blk, nbp), 1)
    brow = lax.broadcasted_iota(jnp.int32, (nbp, blk), 0)
    qpos = lax.broadcasted_iota(jnp.int32, (blk, blk), 0)
    kpos = lax.broadcasted_iota(jnp.int32, (blk, blk), 1)
    row0 = pl.multiple_of(i * blk, blk)
    k_own = k_ref[0, pl.ds(row0, blk), :]
    v_own = v_ref[0, pl.ds(row0, blk), :]

    heads = range(2)
    in_head = [(lane >> 6) == hh for hh in heads]
    qs = [jnp.where(m, q, jnp.zeros_like(q)) for m in in_head]
    ones_other = lambda v, hh: jnp.where(in_head[hh], v, jnp.ones_like(v))
    gs = [jnp.where(brow < i, _dot_t(km, qh), -jnp.inf) for qh in qs]
    sels = [jnp.zeros((nbp, blk), jnp.bool_) for _ in heads]
    for _ in range(MOBA_TOPK):
        ms = [jnp.max(g, axis=0, keepdims=True) for g in gs]
        idxs = [jnp.min(jnp.where(g == m, brow, nbp), axis=0, keepdims=True) for g, m in zip(gs, ms)]
        picks = [(brow == ix) & (g > -jnp.inf) for g, ix in zip(gs, idxs)]
        sels = [sl | pk for sl, pk in zip(sels, picks)]
        gs = [jnp.where(pk, -jnp.inf, g) for g, pk in zip(gs, picks)]
    biases = [jnp.where(sl, 0.0, NEG_INF).T for sl in sels]
    s0 = [jnp.where(kpos <= qpos, _dot_t(qh, k_own), NEG_INF) for qh in qs]
    m0 = [jnp.max(a, axis=-1, keepdims=True) for a in s0]
    p0 = [jnp.exp2(a - m) for a, m in zip(s0, m0)]
    acc0 = [_dot(p, ones_other(v_own, hh)) for hh, p in zip(heads, p0)]
    carry0 = [m0[0], acc0[0], m0[1], acc0[1]]

    def body(jj, carry):
        r0 = pl.multiple_of(jj * (2 * blk), 2 * blk)
        kj = k_ref[0, pl.ds(r0, 2 * blk), :]
        vj = v_ref[0, pl.ds(r0, 2 * blk), :]
        hu = [(hh, u) for hh in heads for u in range(2)]
        raw = [_dot_t(qs[hh], kj[u * blk:(u + 1) * blk]) for hh, u in hu]
        bj = [jnp.sum(jnp.where(col == 2 * jj + u, biases[hh], 0.0), axis=-1, keepdims=True) for hh, u in hu]
        ss = [a + b for a, b in zip(raw, bj)]
        mx = [jnp.max(a, axis=-1, keepdims=True) for a in ss]
        m_new = [jnp.maximum(carry[2 * hh], jnp.maximum(mx[2 * hh], mx[2 * hh + 1])) for hh in heads]
        alpha = [jnp.exp2(carry[2 * hh] - m_new[hh]) for hh in heads]
        ps = [jnp.exp2(a - m_new[hh]) for a, (hh, u) in zip(ss, hu)]
        pcat = [jnp.concatenate([ps[2 * hh], ps[2 * hh + 1]], axis=1) for hh in heads]
        lane2 = lax.broadcasted_iota(jnp.int32, (2 * blk, 128), 1)
        pv = [_dot(pc, jnp.where((lane2 >> 6) == hh, vj, jnp.ones_like(vj))) for hh, pc in zip(heads, pcat)]
        out = []
        for hh in heads:
            out += [m_new[hh], alpha[hh] * carry[2 * hh + 1] + pv[hh]]
        return tuple(out)

    fin = lax.fori_loop(0, (i + 1) // 2, body, tuple(carry0))
    outs = [fin[2 * hh + 1] / pltpu.roll(fin[2 * hh + 1], HEAD_DIM, 1) for hh in heads]
    o_ref[0] = jnp.where(lane < HEAD_DIM, outs[0], outs[1]).astype(o_ref.dtype)


def _moba(qb, kb, vb, km, bsz, seq):
    w = WIDTH
    nb = seq // MOBA_BLOCK
    q3 = qb.reshape(bsz, seq, w)
    k3 = kb.reshape(bsz, seq, w)
    v3 = vb.reshape(bsz, seq, w)
    nbp = 128
    assert nb <= nbp
    km3 = jnp.pad(km.reshape(bsz, nb, w), ((0, 0), (0, nbp - nb), (0, 0)))
    return pl.pallas_call(
        _moba_kernel,
        grid=(bsz, w // 128, nb),
        in_specs=[pl.BlockSpec((1, MOBA_BLOCK, 128), lambda b, p, i: (b, i, p)),
                  pl.BlockSpec((1, seq, 128), lambda b, p, i: (b, 0, p)),
                  pl.BlockSpec((1, seq, 128), lambda b, p, i: (b, 0, p)),
                  pl.BlockSpec((1, nbp, 128), lambda b, p, i: (b, 0, p))],
        out_specs=pl.BlockSpec((1, MOBA_BLOCK, 128), lambda b, p, i: (b, i, p)),
        out_shape=jax.ShapeDtypeStruct((bsz, seq, w), BF16),
        compiler_params=_cparams("parallel", "parallel", "arbitrary"),
        name="moba_attention",
    )(q3, k3, v3, km3)


def _rwkvprep_kernel(z_ref, zp_ref, mu_ref, w0_ref, w2_ref, a0_ref, a2_ref, g2_ref, kk_ref, ka_ref, rk_ref,
                     r_o, lw_o, k_o, v_o, kn_o, b_o, bonus_o, g_o, *, tps):
    i = pl.program_id(0)
    w = WIDTH
    z = z_ref[...]
    tm = z.shape[0]
    prev_last = jnp.where((i % tps) == 0, 0.0, zp_ref[7:8, :])
    rows = lax.broadcasted_iota(jnp.int32, (tm, 1), 0)
    prev = jnp.where(rows == 0, prev_last, pltpu.roll(z, 1, 0))
    zs = z + (prev - z) * mu_ref[...]
    zr, zk, zv = zs[:, 0:w], zs[:, w:2 * w], zs[:, 2 * w:3 * w]
    zwa = zs[:, 3 * w:3 * w + 128]
    zg = zs[:, 3 * w + 128:3 * w + 256]
    nw = -(w0_ref[...] + _dot3(jnp.tanh(zwa), w2_ref[...]))
    softplus = jnp.maximum(nw, 0.0) + jnp.log(1.0 + jnp.exp(-jnp.abs(nw)))
    lw_o[...] = -jnp.exp(-softplus - 0.5)
    a = _sigmoid(a0_ref[...] + _dot3(zwa, a2_ref[...]))
    g_o[...] = _dot3(_sigmoid(zg), g2_ref[...])
    ones = _head_ones(w)
    kn = zk * kk_ref[...]
    kn = kn / jnp.maximum(jnp.sqrt(_dot2(kn * kn, ones)), 1e-12)
    k = zk * (1.0 + (a - 1.0) * ka_ref[...])
    r_o[...] = zr
    k_o[...] = k
    v_o[...] = zv
    kn_o[...] = kn
    b_o[...] = kn * a
    bonus_o[...] = _dot2(zr * k * rk_ref[...], ones) * zv


def _rwkvprep(zrw, tps, mu, w0, w2aug, a0, a2aug, g2, k_k, k_a, r_k):
    t = zrw.shape[0]
    tm = TOKEN_TILE
    w = WIDTH
    row = lambda i: (i, 0)
    fixed = lambda i: (0, 0)
    vec = pl.BlockSpec((1, w), fixed)
    outs = [jax.ShapeDtypeStruct((t, w), F32)] * 8
    return pl.pallas_call(
        functools.partial(_rwkvprep_kernel, tps=tps),
        grid=(t // tm,),
        in_specs=[pl.BlockSpec((tm, SHIFT_WIDTH), row),
                  pl.BlockSpec((8, SHIFT_WIDTH), lambda i: (jnp.maximum(i * (tm // 8) - 1, 0), 0)),
                  pl.BlockSpec((1, SHIFT_WIDTH), fixed),
                  vec, pl.BlockSpec((128, w), fixed), vec, pl.BlockSpec((128, w), fixed),
                  pl.BlockSpec((LORA_G, w), fixed), vec, vec, vec],
        out_specs=[pl.BlockSpec((tm, w), row)] * 8,
        out_shape=outs,
        compiler_params=_cparams("parallel"),
        name="rwkv_prep",
    )(zrw, zrw, mu, w0, w2aug, a0, a2aug, g2, k_k, k_a, r_k)


RWKV_CHUNK = 32


SCAN_GROUP = 4
SCAN_CHUNKS_PER_STEP = 8


def _scan_chunks(probs):
    cl, gw = probs[0][0].shape
    hc = SCAN_GROUP * cl
    bdot = lambda a, b: jnp.dot(a, b, preferred_element_type=F32)
    tn = lambda a, b: lax.dot_general(a, b, (((0,), (0,)), ((), ())), preferred_element_type=F32)
    tri = (lax.broadcasted_iota(jnp.int32, (cl, cl), 0) >= lax.broadcasted_iota(jnp.int32, (cl, cl), 1)).astype(BF16)
    hmask = (lax.broadcasted_iota(jnp.int32, (hc, gw), 0) // cl) == (lax.broadcasted_iota(jnp.int32, (hc, gw), 1) >> 6)
    stack = lambda a: jnp.where(hmask, jnp.tile(a, (SCAN_GROUP, 1)), 0.0).astype(BF16)
    unstack = lambda a: functools.reduce(lambda p, q: p + q, [a[h * cl:(h + 1) * cl] for h in range(SCAN_GROUP)])
    ri = lax.broadcasted_iota(jnp.int32, (hc, hc), 0)
    ci = lax.broadcasted_iota(jnp.int32, (hc, hc), 1)
    strict = ri > ci
    incl = ri >= ci
    eye = jnp.where(ri == ci, 1.0, 0.0)
    di = lax.broadcasted_iota(jnp.int32, (gw, gw), 0) == lax.broadcasted_iota(jnp.int32, (gw, gw), 1)

    cs2 = [bdot(tri, jnp.concatenate(_split(p[1]), axis=1)) for p in probs]
    ops = []
    for (r, lw, k, v, kn, b), c2 in zip(probs, cs2):
        cs = c2[:, :gw] + c2[:, gw:]
        cs_end = cs[cl - 1:cl, :]
        inv = jnp.exp(-cs)
        dec_end = jnp.exp(cs_end - cs)
        ops.append(dict(khs=stack(kn * jnp.exp(cs - lw)),
                        rhs=stack(r * jnp.exp(cs)),
                        bts=stack(b * inv), kts=stack(k * inv), vs=stack(v),
                        btes=stack(b * dec_end), ktes=stack(k * dec_end), gend=jnp.exp(cs_end)))
    grams = [lax.dot_general(jnp.concatenate([o['khs'], o['rhs']], axis=0),
                             jnp.concatenate([o['bts'], o['kts']], axis=0),
                             (((1,), (1,)), ((), ())), preferred_element_type=F32) for o in ops]
    for o, g in zip(ops, grams):
        lb = jnp.where(strict, g[:hc, :hc], 0.0)
        o['lkak'] = jnp.concatenate([jnp.where(strict, g[:hc, hc:], 0.0),
                                     jnp.where(incl, g[hc:, hc:], 0.0)], axis=0).astype(BF16)
        o['ab'] = jnp.where(incl, g[hc:, :hc], 0.0).astype(BF16)
        o['tinv'] = eye - lb
        o['pw'] = (-lb).astype(BF16)
    sq = [bdot(o['pw'], o['pw']) for o in ops]
    for o, q in zip(ops, sq):
        o['pw'] = q.astype(BF16)
    levels = int(np.log2(cl)) - 1
    for lvl in range(levels):
        both = [bdot(jnp.concatenate([o['pw'], o['tinv'].astype(BF16)], axis=0), o['pw']) for o in ops]
        for o, bo in zip(ops, both):
            o['tinv'] = o['tinv'] + bo[hc:]
            o['pw'] = bo[:hc].astype(BF16)
    lkak = [bdot(o['lkak'], o['vs']) for o in ops]
    wu = [bdot(o['tinv'].astype(BF16), jnp.concatenate([o['khs'], lv[:hc].astype(BF16)], axis=1))
          for o, lv in zip(ops, lkak)]
    wub = [jnp.concatenate([w[:, :gw], -w[:, gw:]], axis=1).astype(BF16) for w in wu]
    abw = [bdot(o['ab'], w) for o, w in zip(ops, wub)]
    gmat = [tn(o['btes'], w[:, :gw]) for o, w in zip(ops, wub)]
    hv = [tn(jnp.concatenate([o['btes'], o['ktes']], axis=0), jnp.concatenate([w[:, gw:], o['vs']], axis=0))
          for o, w in zip(ops, wub)]
    out = []
    for o, aw, lv, gm, h_v in zip(ops, abw, lkak, gmat, hv):
        qt = unstack(o['rhs'].astype(F32) - aw[:, :gw])
        yv = unstack(aw[:, gw:] + lv[hc:])
        out.append((qt, yv, jnp.where(di, o['gend'], 0.0) - gm, h_v))
    return out


def _scan_kernel(r_ref, lw_ref, k_ref, v_ref, kn_ref, b_ref, bonus_ref, g_ref, lng_ref, lnb_ref, o_ref, h_ref):
    gw = SCAN_GROUP * HEAD_DIM
    cl = RWKV_CHUNK
    ngrp = N_HEADS // SCAN_GROUP

    @pl.when(pl.program_id(1) == 0)
    def _():
        h_ref[...] = jnp.zeros_like(h_ref)

    where = [(grp, c) for grp in range(ngrp) for c in range(SCAN_CHUNKS_PER_STEP)]
    sl = lambda grp, c: (slice(c * cl, (c + 1) * cl), slice(grp * gw, (grp + 1) * gw))
    parts = _scan_chunks([tuple(ref[sl(grp, c)] for ref in (r_ref, lw_ref, k_ref, v_ref, kn_ref, b_ref))
                          for grp, c in where])
    ones = _head_ones(gw)
    bdot = lambda a, b: jnp.dot(a, b, preferred_element_type=F32)
    for grp in range(ngrp):
        h = h_ref[grp]
        for c in range(SCAN_CHUNKS_PER_STEP):
            rows, ln = sl(grp, c)
            qt, yv, abar, h_v = parts[where.index((grp, c))]
            hh, hl = _split(h)
            y = bdot(qt.astype(BF16), hh) + yv
            ah, al = _split(abar)
            ahl = bdot(jnp.concatenate([ah, al], axis=0), hh)
            h = ahl[:gw] + ahl[gw:] + bdot(ah, hl) + h_v
            mean = _dot2(y, ones) * (1.0 / HEAD_DIM)
            yc = y - mean
            var = _dot2(yc * yc, ones) * (1.0 / HEAD_DIM)
            yn = yc * lax.rsqrt(var + GN_EPS) * lng_ref[:, ln] + lnb_ref[:, ln] + bonus_ref[rows, ln]
            o_ref[rows, ln] = (yn * g_ref[rows, ln]).astype(o_ref.dtype)
        h_ref[grp] = h


def _rwkv_scan(r, lw, k, v, kn, b, bonus, g, ln_g, ln_b, bsz, seq):
    t, w = r.shape
    cl = RWKV_CHUNK * SCAN_CHUNKS_PER_STEP
    cps = seq // cl
    row = pl.BlockSpec((cl, w), lambda bi, c: (bi * cps + c, 0))
    vec = pl.BlockSpec((1, w), lambda bi, c: (0, 0))
    return pl.pallas_call(
        _scan_kernel,
        grid=(bsz, cps),
        in_specs=[row] * 8 + [vec, vec],
        out_specs=row,
        out_shape=jax.ShapeDtypeStruct((t, w), BF16),
        scratch_shapes=[pltpu.VMEM((N_HEADS // SCAN_GROUP, SCAN_GROUP * HEAD_DIM, SCAN_GROUP * HEAD_DIM), F32)],
        compiler_params=_cparams("parallel", "arbitrary"),
        name="rwkv_scan",
    )(r, lw, k, v, kn, b, bonus, g, ln_g, ln_b)


def _rwkv_branch(zrw, p, l, bsz, seq):
    w = WIDTH
    vec = lambda a: a.reshape(1, -1)
    zpad = lambda a, top: jnp.concatenate([a, jnp.zeros_like(a)] if top else [jnp.zeros_like(a), a], axis=0)
    outs = _rwkvprep(zrw, seq // TOKEN_TILE, vec(p['rwkv_mu'][l]), vec(p['rwkv_w0'][l]),
                     zpad(p['rwkv_w2'][l], True), vec(p['rwkv_a0'][l]), zpad(p['rwkv_a2'][l], False),
                     p['rwkv_g2'][l], vec(p['rwkv_k_k'][l]), vec(p['rwkv_k_a'][l]), vec(p['rwkv_r_k'][l]))
    return _rwkv_scan(*outs, vec(p['rwkv_ln_g'][l]), vec(p['rwkv_ln_b'][l]), bsz, seq)


def _merge_kernel(x_ref, om_ref, or_ref, gl_ref, mod_ref, g2_ref, wpm_ref, wpr_ref, wo_ref, wq_ref,
                  h_o, xn_o, q_o):
    d = x_ref.shape[1]
    mod = mod_ref[0]
    pm = jnp.dot(om_ref[...], wpm_ref[...], preferred_element_type=F32)
    pr = jnp.dot(or_ref[...], wpr_ref[...], preferred_element_type=F32)
    gl = gl_ref[...]
    mixed = _sigmoid(gl[:, :d]) * pm + _sigmoid(gl[:, d:]) * pr
    h = x_ref[...] + mod[:, 2 * d:3 * d] * _dot(mixed, wo_ref[...])
    h_o[...] = h
    xn2 = _modnorm(h, g2_ref[...], mod, d, 1)
    xn_o[...] = xn2
    q_o[...] = _dot(xn2, wq_ref[...])


def _merge(x2, o_moba, o_rwkv, gl, mod3, g2, wpm, wpr, wo, wq, tps):
    t, d = x2.shape
    tm = TOKEN_TILE
    w = WIDTH
    nq = wq.shape[1]
    row = lambda i: (i, 0)
    fixed = lambda i: (0, 0)
    return pl.pallas_call(
        _merge_kernel,
        grid=(t // tm,),
        in_specs=[pl.BlockSpec((tm, d), row), pl.BlockSpec((tm, w), row), pl.BlockSpec((tm, w), row),
                  pl.BlockSpec((tm, 2 * d), row),
                  pl.BlockSpec((1, 1, mod3.shape[2]), lambda i: (i // tps, 0, 0)),
                  pl.BlockSpec((1, d), fixed), pl.BlockSpec((w, d), fixed), pl.BlockSpec((w, d), fixed),
                  pl.BlockSpec((d, d), fixed), pl.BlockSpec((d, nq), fixed)],
        out_specs=[pl.BlockSpec((tm, d), row), pl.BlockSpec((tm, d), row), pl.BlockSpec((tm, nq), row)],
        out_shape=[jax.ShapeDtypeStruct((t, d), F32), jax.ShapeDtypeStruct((t, d), F32),
                   jax.ShapeDtypeStruct((t, nq), F32)],
        compiler_params=_cparams("parallel"),
        name="merge_proj",
    )(x2, o_moba, o_rwkv, gl, mod3, g2, wpm, wpr, wo, wq)


PEER_TILE = 128


def _top16_rows(problems):
    tm = problems[0][0].shape[1]
    rank = lax.broadcasted_iota(jnp.int32, (PEER_TOPK, tm), 0)
    keys = [k for _, k in problems]
    big = jnp.int32(1 << 30)

    def rnd(r, carry):
        out = []
        hit = rank == r
        for (s, vals, idxs), key in zip(carry, keys):
            m = jnp.max(s, axis=0, keepdims=True)
            ix = jnp.min(jnp.where(s == m, key, big), axis=0, keepdims=True)
            out.append((jnp.where(key == ix, -jnp.inf, s), jnp.where(hit, m, vals),
                        jnp.where(hit, ix.astype(F32), idxs)))
        return tuple(out)

    z = jnp.zeros((PEER_TOPK, tm), F32)
    fin = lax.fori_loop(0, PEER_TOPK, rnd, tuple((s, z, z) for s, _ in problems))
    return [(vals, idxs) for _, vals, idxs in fin]


def _peer_topk_kernel(q_ref, sk_ref, idx_o, gate_o):
    tm = q_ref.shape[0]
    kk = PEER_TOPK
    nk = PEER_NKEYS
    rank = lax.broadcasted_iota(jnp.int32, (kk, tm), 0)
    pos_key = lax.broadcasted_iota(jnp.int32, (nk, tm), 0)
    row8 = lax.broadcasted_iota(jnp.int32, (8, tm), 0)
    tdot = lambda a, b: lax.dot_general(a, b, (((1,), (1,)), ((), ())), preferred_element_type=F32)
    cand_key = jnp.concatenate([rank] + [row8 + a * kk for a in range(1, 8)] + [(row8 + 8) * kk], axis=0)

    tops = []
    for h0 in range(0, PEER_HEADS, 2):
        probs = []
        for hp in range(2 * h0, 2 * h0 + 4):
            qh, ql = _split(q_ref[:, hp * PEER_HALF:(hp + 1) * PEER_HALF])
            kh, kl = _split(sk_ref[hp])
            probs.append((tdot(kh, qh) + tdot(kh, ql) + tdot(kl, qh), pos_key))
        res = _top16_rows(probs)
        tops += [res[0:2], res[2:4]]

    idx_rows, gate_rows = [], []
    for h0 in range(0, PEER_HEADS, 2):
        probs = []
        for h in (h0, h0 + 1):
            (ts0, _), (ts1, _) = tops[h]
            cand = jnp.concatenate([ts0[0:1] + ts1] + [ts0[a:a + 1] + ts1[0:8] for a in range(1, 8)]
                                   + [ts0[8:16] + ts1[0:1]], axis=0)
            probs.append((cand, cand_key))
        for h, (best, pos) in zip((h0, h0 + 1), _top16_rows(probs)):
            (_, ti0), (_, ti1) = tops[h]
            pos = pos.astype(jnp.int32)
            ids = jnp.zeros((kk, tm), F32)
            for r in range(kk):
                pr = pos[r:r + 1, :]
                i0 = jnp.sum(jnp.where(rank == (pr >> 4), ti0, 0.0), axis=0, keepdims=True)
                i1 = jnp.sum(jnp.where(rank == (pr & (kk - 1)), ti1, 0.0), axis=0, keepdims=True)
                ids = jnp.where(rank == r, i0 * float(nk) + i1, ids)
            e = jnp.exp(best - jnp.max(best, axis=0, keepdims=True))
            gate_rows.append(e / jnp.sum(e, axis=0, keepdims=True))
            idx_rows.append(ids)
    idx_o[...] = jnp.concatenate(idx_rows, axis=0).T.astype(jnp.int32)
    gate_o[...] = jnp.concatenate(gate_rows, axis=0).T


def _peer_topk(q, subkeys16):
    t, nq = q.shape
    tm = PEER_TILE
    return pl.pallas_call(
        _peer_topk_kernel,
        grid=(t // tm,),
        in_specs=[pl.BlockSpec((tm, nq), lambda i: (i, 0)),
                  pl.BlockSpec(subkeys16.shape, lambda i: (0, 0, 0))],
        out_specs=[pl.BlockSpec((tm, 128), lambda i: (i, 0)), pl.BlockSpec((tm, 128), lambda i: (i, 0))],
        out_shape=[jax.ShapeDtypeStruct((t, 128), jnp.int32), jax.ShapeDtypeStruct((t, 128), F32)],
        compiler_params=_cparams("parallel"),
        name="peer_topk",
    )(q, subkeys16)


GATHER_TOKENS = 16
N_PICK = PEER_HEADS * PEER_TOPK
N_BURST = 4


def _peer_kernel(idx_cur, idx_nxt, gate_ref, xn_ref, h_ref, mod_ref, uv_hbm, o_ref, buf_a, buf_b, sem_a, sem_b):
    i = pl.program_id(0)
    n = pl.num_programs(0)
    tt = GATHER_TOKENS
    d = h_ref.shape[1]
    gt2 = mod_ref[0][:, 5 * d:6 * d]
    bufs = (buf_a, buf_b)
    sems = (sem_a, sem_b)
    per = N_PICK // N_BURST

    def row_copy(idx_ref, row, half, tok, e):
        return pltpu.make_async_copy(uv_hbm.at[idx_ref[row, e]], bufs[half].at[tok, pl.ds(e * 8, 8), :],
                                     sems[half].at[tok])

    def burst(idx_ref, row0, half, toks, b):
        for k, tok in enumerate(toks):
            for e in range(b * per, (b + 1) * per):
                row_copy(idx_ref, row0 + k, half, tok, e).start(priority=e % 2)

    def pair(half, tok0, idx_next, next_row0):
        toks = (tok0, tok0 + 1)
        rows = [half * tt + t for t in toks]
        for t, row in zip(toks, rows):
            for e in range(N_PICK):
                row_copy(idx_cur, row, half, t, e).wait()
        nch = d // 128
        words = [[bufs[half][t, pl.ds(c, N_PICK, stride=nch), :] for c in range(nch)] for t in toks]
        xts = [xn_ref[row:row + 1, :].astype(BF16) for row in rows]
        gates = [gate_ref[row:row + 1, :] for row in rows]
        hrows = [h_ref[row:row + 1, :] for row in rows]
        refill = lambda b: burst(idx_next, next_row0, 1 - half, toks, b)
        refill(0)
        us = [[lax.bitcast_convert_type(w << 16, F32).astype(BF16) for w in ws] for ws in words]
        refill(1)
        tdot = lambda a, b: lax.dot_general(a, b, (((1,), (1,)), ((), ())), preferred_element_type=F32)
        hds = [functools.reduce(lambda p, q: p + q, [tdot(x[:, c * 128:(c + 1) * 128], u[c]) for c in range(nch)])
               for x, u in zip(xts, us)]
        vs = [[lax.bitcast_convert_type(w & jnp.uint32(0xFFFF0000), F32).astype(BF16) for w in ws] for ws in words]
        refill(2)
        hds = [(0.5 * h * (1.0 + lax.erf(h * (2.0 ** -0.5))) * g).astype(BF16) for h, g in zip(hds, gates)]
        outs = [jnp.concatenate([jnp.dot(h, v[c], preferred_element_type=F32) for c in range(nch)], axis=1)
                for h, v in zip(hds, vs)]
        refill(3)
        for row, hr, out in zip(rows, hrows, outs):
            o_ref[row:row + 1, :] = hr + gt2 * out

    @pl.when(i == 0)
    def _():
        for tok in range(tt):
            for e in range(N_PICK):
                row_copy(idx_cur, tok, 0, tok, e).start(priority=e % 2)

    for tok0 in range(0, tt, 2):
        pair(0, tok0, idx_cur, tt + tok0)
    for tok0 in range(0, tt, 2):
        pair(1, tok0, idx_nxt, tok0)

    @pl.when(i == n - 1)
    def _():
        for tok in range(tt):
            for e in range(N_PICK):
                row_copy(idx_nxt, tok, 0, tok, e).wait()


def _peer(idx, gates, xn2, h1, mod3, uv_words, tokens_per_seq):
    t, d = h1.shape
    tt = GATHER_TOKENS
    blk = 2 * tt
    nsteps = t // blk
    row = lambda i: (i, 0)
    buf = pltpu.VMEM((tt, N_PICK * (d // 128), 128), jnp.uint32)
    return pl.pallas_call(
        _peer_kernel,
        grid=(nsteps,),
        in_specs=[pl.BlockSpec((blk, N_PICK), row, memory_space=pltpu.SMEM),
                  pl.BlockSpec((blk, N_PICK), lambda i: (jnp.minimum(i + 1, nsteps - 1), 0), memory_space=pltpu.SMEM),
                  pl.BlockSpec((blk, N_PICK), row), pl.BlockSpec((blk, d), row), pl.BlockSpec((blk, d), row),
                  pl.BlockSpec((1, 1, mod3.shape[2]), lambda i: ((i * blk) // tokens_per_seq, 0, 0)),
                  pl.BlockSpec(memory_space=pl.ANY)],
        out_specs=pl.BlockSpec((blk, d), row),
        out_shape=jax.ShapeDtypeStruct((t, d), F32),
        scratch_shapes=[buf, buf, pltpu.SemaphoreType.DMA((tt,)), pltpu.SemaphoreType.DMA((tt,))],
        compiler_params=_cparams("arbitrary"),
        name="peer_experts",
    )(idx, idx, gates, xn2, h1, mod3, uv_words)


def _pack_uv(u, v):
    ub = lax.bitcast_convert_type(u.astype(BF16), jnp.uint16).astype(jnp.uint32)
    vb = lax.bitcast_convert_type(v.astype(BF16), jnp.uint16).astype(jnp.uint32)
    return (ub | (vb << 16)).reshape(u.shape[0], u.shape[1] // 128, 128)


def _rope_inv128():
    half = ROPE_DIM // 2
    inv = ROPE_THETA ** (-(jnp.arange(half, dtype=F32) * 2.0) / ROPE_DIM)
    lane = np.arange(128) % HEAD_DIM
    out = jnp.where(lane < ROPE_DIM, inv[lane % half], 0.0)
    return out.reshape(1, 128)


def kernel(x, c, positions, w_ada, b_ada, norm1_g, w_in, q_norm_g, k_norm_g, rwkv_mu, rwkv_w0, rwkv_w2, rwkv_a0, rwkv_a2, rwkv_g2, rwkv_k_k, rwkv_k_a, rwkv_r_k, rwkv_ln_g, rwkv_ln_b, w_proj_moba, w_proj_rwkv, w_out, norm2_g, peer_wq, peer_subkeys, peer_u, peer_v):
    bsz, seq, d = x.shape
    t = bsz * seq
    tps = seq // TOKEN_TILE
    l = 0
    x2 = x.reshape(t, d)
    mod3 = _adaln(c, w_ada[l], b_ada[l]).reshape(bsz, 1, 6 * d)
    w_in_b = w_in[l].astype(BF16)
    g1 = norm1_g[l].reshape(1, d)
    qkv = _inproj(x2, mod3, g1, w_in_b[:, :3 * WIDTH], tps)
    qb, kb, vb, km = _qkprep(qkv, positions.reshape(t, 1), jnp.tile(q_norm_g[l], N_HEADS).reshape(1, WIDTH),
                             jnp.tile(k_norm_g[l], N_HEADS).reshape(1, WIDTH), _rope_inv128())
    o_moba = _moba(qb, kb, vb, km, bsz, seq)
    params = dict(rwkv_mu=rwkv_mu, rwkv_w0=rwkv_w0, rwkv_w2=rwkv_w2, rwkv_a0=rwkv_a0, rwkv_a2=rwkv_a2,
                  rwkv_g2=rwkv_g2, rwkv_k_k=rwkv_k_k, rwkv_k_a=rwkv_k_a, rwkv_r_k=rwkv_r_k.reshape(1, WIDTH),
                  rwkv_ln_g=rwkv_ln_g, rwkv_ln_b=rwkv_ln_b)
    zrw = _inproj(x2, mod3, g1, w_in_b[:, 3 * WIDTH:3 * WIDTH + SHIFT_WIDTH], tps)
    o_rwkv = _rwkv_branch(zrw, params, l, bsz, seq)
    gl = _inproj(x2, mod3, g1, w_in_b[:, 3 * WIDTH + SHIFT_WIDTH:], tps)
    h1, xn2, q = _merge(x2, o_moba.reshape(t, WIDTH), o_rwkv, gl, mod3, norm2_g[l].reshape(1, d),
                        w_proj_moba[l].astype(BF16), w_proj_rwkv[l].astype(BF16), w_out[l].astype(BF16),
                        peer_wq[l].astype(BF16), tps)
    idx, gates = _peer_topk(q, peer_subkeys[l].reshape(2 * PEER_HEADS, PEER_NKEYS, PEER_HALF))
    out = _peer(idx, gates, xn2, h1, mod3, _pack_uv(peer_u[l], peer_v[l]), seq)
    return out.reshape(bsz, seq, d)
```

```python
import functools

import numpy as np
import jax
import jax.numpy as jnp
from jax import lax
from jax.experimental import pallas as pl
from jax.experimental.pallas import tpu as pltpu

F32 = jnp.float32
BF16 = jnp.bfloat16

HEAD_DIM = 64
N_HEADS = 8
WIDTH = N_HEADS * HEAD_DIM
MOBA_BLOCK = 256
MOBA_TOPK = 3
ROPE_THETA = 500000.0
ROPE_DIM = HEAD_DIM // 4
LORA_W = 64
LORA_A = 64
LORA_G = 128
SHIFT_WIDTH = 3 * WIDTH + LORA_W + LORA_A + LORA_G
GN_EPS = 64e-5
RMS_EPS = 1e-6
NEG_INF = -1e30
LOG2_E = 1.4426950408889634
PEER_HEADS = 8
PEER_NKEYS = 128
PEER_HALF = 128
PEER_TOPK = 16

TOKEN_TILE = 256
V7X_VMEM_LIMIT = 48 * 1024 * 1024


def _cparams(*sem, flags=None):
    return pltpu.CompilerParams(dimension_semantics=sem, vmem_limit_bytes=V7X_VMEM_LIMIT, flags=flags)


def _dot(a, b):
    return jnp.dot(a.astype(BF16), b.astype(BF16), preferred_element_type=F32)


def _dot_t(a, b):
    return lax.dot_general(a.astype(BF16), b.astype(BF16), (((1,), (1,)), ((), ())),
                           preferred_element_type=F32)


def _split(a):
    hi = a.astype(BF16)
    lo = (a - hi.astype(F32)).astype(BF16)
    return hi, lo


def _dot3(a, b):
    ah, al = _split(a)
    bh, bl = _split(b)
    return (jnp.dot(ah, bh, preferred_element_type=F32) + jnp.dot(ah, bl, preferred_element_type=F32)
            + jnp.dot(al, bh, preferred_element_type=F32))


def _dot2(a, b_exact):
    ah, al = _split(a)
    return jnp.dot(ah, b_exact, preferred_element_type=F32) + jnp.dot(al, b_exact, preferred_element_type=F32)


def _head_ones(n):
    r = lax.broadcasted_iota(jnp.int32, (n, n), 0) >> 6
    c = lax.broadcasted_iota(jnp.int32, (n, n), 1) >> 6
    return (r == c).astype(BF16)


def _sigmoid(x):
    return 1.0 / (1.0 + jnp.exp(-x))


def _ada_kernel(c_ref, w_ref, b_ref, o_ref):
    c = c_ref[...]
    o_ref[...] = _dot3(c * _sigmoid(c), w_ref[...]) + b_ref[...]


def _adaln(c, w_ada, b_ada):
    bsz, d = c.shape
    n = w_ada.shape[1]
    nb = 1024
    return pl.pallas_call(
        _ada_kernel,
        grid=(n // nb,),
        in_specs=[pl.BlockSpec((bsz, d), lambda j: (0, 0)),
                  pl.BlockSpec((d, nb), lambda j: (0, j)),
                  pl.BlockSpec((1, nb), lambda j: (0, j))],
        out_specs=pl.BlockSpec((bsz, nb), lambda j: (0, j)),
        out_shape=jax.ShapeDtypeStruct((bsz, n), F32),
        compiler_params=_cparams("arbitrary"),
        name="adaln_mod",
    )(c, w_ada, b_ada.reshape(1, n))


def _modnorm(x, g, mod, d, which):
    ms = jnp.mean(x * x, axis=-1, keepdims=True)
    y = x * lax.rsqrt(ms + RMS_EPS) * g
    sh = mod[:, (3 * which) * d:(3 * which + 1) * d]
    sc = mod[:, (3 * which + 1) * d:(3 * which + 2) * d]
    return y * (1.0 + sc) + sh


def _inproj_kernel(x_ref, mod_ref, g_ref, w_ref, o_ref):
    d = x_ref.shape[1]
    xn = _modnorm(x_ref[...], g_ref[...], mod_ref[0], d, 0)
    o_ref[...] = jnp.dot(xn.astype(BF16), w_ref[...], preferred_element_type=F32)


def _inproj(x2, mod3, g, w_bf16, tps):
    t, d = x2.shape
    n = w_bf16.shape[1]
    tm = TOKEN_TILE
    return pl.pallas_call(
        _inproj_kernel,
        grid=(t // tm,),
        in_specs=[pl.BlockSpec((tm, d), lambda i: (i, 0)),
                  pl.BlockSpec((1, 1, mod3.shape[2]), lambda i: (i // tps, 0, 0)),
                  pl.BlockSpec((1, d), lambda i: (0, 0)),
                  pl.BlockSpec((d, n), lambda i: (0, 0))],
        out_specs=pl.BlockSpec((tm, n), lambda i: (i, 0)),
        out_shape=jax.ShapeDtypeStruct((t, n), F32),
        compiler_params=_cparams("parallel"),
        name="in_proj",
    )(x2, mod3, g, w_bf16)


def _qkprep_kernel(qkv_ref, pos_ref, gq_ref, gk_ref, inv_ref, qo_ref, ko_ref, vo_ref, km_ref):
    w = WIDTH
    ones = _head_ones(w)
    ang = pos_ref[...].astype(F32) * inv_ref[...]
    cos = jnp.tile(jnp.cos(ang), (1, w // 128))
    sin = jnp.tile(jnp.sin(ang), (1, w // 128))
    dd = lax.broadcasted_iota(jnp.int32, (1, w), 1) & (HEAD_DIM - 1)
    half = ROPE_DIM // 2
    s_lo = jnp.where(dd < half, -1.0, 0.0) * sin
    s_hi = jnp.where((dd >= half) & (dd < ROPE_DIM), 1.0, 0.0) * sin

    def norm_rope(xh, g):
        ssq = _dot2(xh * xh, ones)
        y = xh * lax.rsqrt(ssq * (1.0 / HEAD_DIM) + RMS_EPS) * g
        return y * cos + pltpu.roll(y, w - half, 1) * s_lo + pltpu.roll(y, half, 1) * s_hi

    q = norm_rope(qkv_ref[:, 0:w], gq_ref[...])
    k = norm_rope(qkv_ref[:, w:2 * w], gk_ref[...])
    qo_ref[...] = (q * (HEAD_DIM ** -0.5 * LOG2_E)).astype(BF16)
    ko_ref[...] = k.astype(BF16)
    vo_ref[...] = qkv_ref[:, 2 * w:3 * w].astype(BF16)
    km_ref[0] = jnp.mean(k, axis=0, keepdims=True)


def _qkprep(qkv, pos2, gq, gk, inv128):
    t = qkv.shape[0]
    tm = MOBA_BLOCK
    w = WIDTH
    row = lambda i: (i, 0)
    fixed = lambda i: (0, 0)
    return pl.pallas_call(
        _qkprep_kernel,
        grid=(t // tm,),
        in_specs=[pl.BlockSpec((tm, 3 * w), row), pl.BlockSpec((tm, 1), row),
                  pl.BlockSpec((1, w), fixed), pl.BlockSpec((1, w), fixed), pl.BlockSpec((1, 128), fixed)],
        out_specs=[pl.BlockSpec((tm, w), row), pl.BlockSpec((tm, w), row), pl.BlockSpec((tm, w), row),
                   pl.BlockSpec((1, 1, w), lambda i: (i, 0, 0))],
        out_shape=[jax.ShapeDtypeStruct((t, w), BF16), jax.ShapeDtypeStruct((t, w), BF16),
                   jax.ShapeDtypeStruct((t, w), BF16), jax.ShapeDtypeStruct((t // tm, 1, w), F32)],
        compiler_params=_cparams("parallel"),
        name="moba_qk_prep",
    )(qkv, pos2, gq, gk, inv128)


def _moba_kernel(q_ref, k_ref, v_ref, km_ref, o_ref):
    blk = MOBA_BLOCK
    i = pl.program_id(2)
    q = q_ref[0]
    km = km_ref[0]
    nbp = km.shape[0]
    lane = lax.broadcasted_iota(jnp.int32, (blk, 128), 1)
    col = lax.broadcasted_iota(jnp.int32, (blk, nbp), 1)
    brow = lax.broadcasted_iota(jnp.int32, (nbp, blk), 0)
    qpos = lax.broadcasted_iota(jnp.int32, (blk, blk), 0)
    kpos = lax.broadcasted_iota(jnp.int32, (blk, blk), 1)
    row0 = pl.multiple_of(i * blk, blk)
    k_own = k_ref[0, pl.ds(row0, blk), :]
    v_own = v_ref[0, pl.ds(row0, blk), :]

    heads = range(2)
    in_head = [(lane >> 6) == hh for hh in heads]
    qs = [jnp.where(m, q, jnp.zeros_like(q)) for m in in_head]
    ones_other = lambda v, hh: jnp.where(in_head[hh], v, jnp.ones_like(v))
    gs = [jnp.where(brow < i, _dot_t(km, qh), -jnp.inf) for qh in qs]
    sels = [jnp.zeros((nbp, blk), jnp.bool_) for _ in heads]
    for _ in range(MOBA_TOPK):
        ms = [jnp.max(g, axis=0, keepdims=True) for g in gs]
        idxs = [jnp.min(jnp.where(g == m, brow, nbp), axis=0, keepdims=True) for g, m in zip(gs, ms)]
        picks = [(brow == ix) & (g > -jnp.inf) for g, ix in zip(gs, idxs)]
        sels = [sl | pk for sl, pk in zip(sels, picks)]
        gs = [jnp.where(pk, -jnp.inf, g) for g, pk in zip(gs, picks)]
    biases = [jnp.where(sl, 0.0, NEG_INF).T for sl in sels]
    s0 = [jnp.where(kpos <= qpos, _dot_t(qh, k_own), NEG_INF) for qh in qs]
    m0 = [jnp.max(a, axis=-1, keepdims=True) for a in s0]
    p0 = [jnp.exp2(a - m) for a, m in zip(s0, m0)]
    acc0 = [_dot(p, ones_other(v_own, hh)) for hh, p in zip(heads, p0)]
    carry0 = [m0[0], acc0[0], m0[1], acc0[1]]

    def body(jj, carry):
        r0 = pl.multiple_of(jj * (2 * blk), 2 * blk)
        kj = k_ref[0, pl.ds(r0, 2 * blk), :]
        vj = v_ref[0, pl.ds(r0, 2 * blk), :]
        hu = [(hh, u) for hh in heads for u in range(2)]
        raw = [_dot_t(qs[hh], kj[u * blk:(u + 1) * blk]) for hh, u in hu]
        bj = [jnp.sum(jnp.where(col == 2 * jj + u, biases[hh], 0.0), axis=-1, keepdims=True) for hh, u in hu]
        ss = [a + b for a, b in zip(raw, bj)]
        mx = [jnp.max(a, axis=-1, keepdims=True) for a in ss]
        m_new = [jnp.maximum(carry[2 * hh], jnp.maximum(mx[2 * hh], mx[2 * hh + 1])) for hh in heads]
        alpha = [jnp.exp2(carry[2 * hh] - m_new[hh]) for hh in heads]
        ps = [jnp.exp2(a - m_new[hh]) for a, (hh, u) in zip(ss, hu)]
        pcat = [jnp.concatenate([ps[2 * hh], ps[2 * hh + 1]], axis=1) for hh in heads]
        lane2 = lax.broadcasted_iota(jnp.int32, (2 * blk, 128), 1)
        pv = [_dot(pc, jnp.where((lane2 >> 6) == hh, vj, jnp.ones_like(vj))) for hh, pc in zip(heads, pcat)]
        out = []
        for hh in heads:
            out += [m_new[hh], alpha[hh] * carry[2 * hh + 1] + pv[hh]]
        return tuple(out)

    fin = lax.fori_loop(0, (i + 1) // 2, body, tuple(carry0))
    outs = [fin[2 * hh + 1] / pltpu.roll(fin[2 * hh + 1], HEAD_DIM, 1) for hh in heads]
    o_ref[0] = jnp.where(lane < HEAD_DIM, outs[0], outs[1]).astype(o_ref.dtype)


def _moba(qb, kb, vb, km, bsz, seq):
    w = WIDTH
    nb = seq // MOBA_BLOCK
    q3 = qb.reshape(bsz, seq, w)
    k3 = kb.reshape(bsz, seq, w)
    v3 = vb.reshape(bsz, seq, w)
    nbp = 128
    assert nb <= nbp
    km3 = jnp.pad(km.reshape(bsz, nb, w), ((0, 0), (0, nbp - nb), (0, 0)))
    return pl.pallas_call(
        _moba_kernel,
        grid=(bsz, w // 128, nb),
        in_specs=[pl.BlockSpec((1, MOBA_BLOCK, 128), lambda b, p, i: (b, i, p)),
                  pl.BlockSpec((1, seq, 128), lambda b, p, i: (b, 0, p)),
                  pl.BlockSpec((1, seq, 128), lambda b, p, i: (b, 0, p)),
                  pl.BlockSpec((1, nbp, 128), lambda b, p, i: (b, 0, p))],
        out_specs=pl.BlockSpec((1, MOBA_BLOCK, 128), lambda b, p, i: (b, i, p)),
        out_shape=jax.ShapeDtypeStruct((bsz, seq, w), BF16),
        compiler_params=_cparams("parallel", "parallel", "arbitrary"),
        name="moba_attention",
    )(q3, k3, v3, km3)


def _rwkvprep_kernel(z_ref, zp_ref, mu_ref, w0_ref, w2_ref, a0_ref, a2_ref, g2_ref, kk_ref, ka_ref, rk_ref,
                     r_o, lw_o, k_o, v_o, kn_o, b_o, bonus_o, g_o, *, tps):
    i = pl.program_id(0)
    w = WIDTH
    z = z_ref[...]
    tm = z.shape[0]
    prev_last = jnp.where((i % tps) == 0, 0.0, zp_ref[7:8, :])
    rows = lax.broadcasted_iota(jnp.int32, (tm, 1), 0)
    prev = jnp.where(rows == 0, prev_last, pltpu.roll(z, 1, 0))
    zs = z + (prev - z) * mu_ref[...]
    zr, zk, zv = zs[:, 0:w], zs[:, w:2 * w], zs[:, 2 * w:3 * w]
    zwa = zs[:, 3 * w:3 * w + 128]
    zg = zs[:, 3 * w + 128:3 * w + 256]
    nw = -(w0_ref[...] + _dot3(jnp.tanh(zwa), w2_ref[...]))
    softplus = jnp.maximum(nw, 0.0) + jnp.log(1.0 + jnp.exp(-jnp.abs(nw)))
    lw_o[...] = -jnp.exp(-softplus - 0.5)
    a = _sigmoid(a0_ref[...] + _dot3(zwa, a2_ref[...]))
    g_o[...] = _dot3(_sigmoid(zg), g2_ref[...])
    ones = _head_ones(w)
    kn = zk * kk_ref[...]
    kn = kn / jnp.maximum(jnp.sqrt(_dot2(kn * kn, ones)), 1e-12)
    k = zk * (1.0 + (a - 1.0) * ka_ref[...])
    r_o[...] = zr
    k_o[...] = k
    v_o[...] = zv
    kn_o[...] = kn
    b_o[...] = kn * a
    bonus_o[...] = _dot2(zr * k * rk_ref[...], ones) * zv


def _rwkvprep(zrw, tps, mu, w0, w2aug, a0, a2aug, g2, k_k, k_a, r_k):
    t = zrw.shape[0]
    tm = TOKEN_TILE
    w = WIDTH
    row = lambda i: (i, 0)
    fixed = lambda i: (0, 0)
    vec = pl.BlockSpec((1, w), fixed)
    outs = [jax.ShapeDtypeStruct((t, w), F32)] * 8
    return pl.pallas_call(
        functools.partial(_rwkvprep_kernel, tps=tps),
        grid=(t // tm,),
        in_specs=[pl.BlockSpec((tm, SHIFT_WIDTH), row),
                  pl.BlockSpec((8, SHIFT_WIDTH), lambda i: (jnp.maximum(i * (tm // 8) - 1, 0), 0)),
                  pl.BlockSpec((1, SHIFT_WIDTH), fixed),
                  vec, pl.BlockSpec((128, w), fixed), vec, pl.BlockSpec((128, w), fixed),
                  pl.BlockSpec((LORA_G, w), fixed), vec, vec, vec],
        out_specs=[pl.BlockSpec((tm, w), row)] * 8,
        out_shape=outs,
        compiler_params=_cparams("parallel"),
        name="rwkv_prep",
    )(zrw, zrw, mu, w0, w2aug, a0, a2aug, g2, k_k, k_a, r_k)


RWKV_CHUNK = 32


SCAN_GROUP = 4
SCAN_CHUNKS_PER_STEP = 8


def _scan_chunks(probs):
    cl, gw = probs[0][0].shape
    hc = SCAN_GROUP * cl
    bdot = lambda a, b: jnp.dot(a, b, preferred_element_type=F32)
    tn = lambda a, b: lax.dot_general(a, b, (((0,), (0,)), ((), ())), preferred_element_type=F32)
    tri = (lax.broadcasted_iota(jnp.int32, (cl, cl), 0) >= lax.broadcasted_iota(jnp.int32, (cl, cl), 1)).astype(BF16)
    hmask = (lax.broadcasted_iota(jnp.int32, (hc, gw), 0) // cl) == (lax.broadcasted_iota(jnp.int32, (hc, gw), 1) >> 6)
    stack = lambda a: jnp.where(hmask, jnp.tile(a, (SCAN_GROUP, 1)), 0.0).astype(BF16)
    unstack = lambda a: functools.reduce(lambda p, q: p + q, [a[h * cl:(h + 1) * cl] for h in range(SCAN_GROUP)])
    ri = lax.broadcasted_iota(jnp.int32, (hc, hc), 0)
    ci = lax.broadcasted_iota(jnp.int32, (hc, hc), 1)
    strict = ri > ci
    incl = ri >= ci
    eye = jnp.where(ri == ci, 1.0, 0.0)
    di = lax.broadcasted_iota(jnp.int32, (gw, gw), 0) == lax.broadcasted_iota(jnp.int32, (gw, gw), 1)

    cs2 = [bdot(tri, jnp.concatenate(_split(p[1]), axis=1)) for p in probs]
    ops = []
    for (r, lw, k, v, kn, b), c2 in zip(probs, cs2):
        cs = c2[:, :gw] + c2[:, gw:]
        cs_end = cs[cl - 1:cl, :]
        inv = jnp.exp(-cs)
        dec_end = jnp.exp(cs_end - cs)
        ops.append(dict(khs=stack(kn * jnp.exp(cs - lw)),
                        rhs=stack(r * jnp.exp(cs)),
                        bts=stack(b * inv), kts=stack(k * inv), vs=stack(v),
                        btes=stack(b * dec_end), ktes=stack(k * dec_end), gend=jnp.exp(cs_end)))
    grams = [lax.dot_general(jnp.concatenate([o['khs'], o['rhs']], axis=0),
                             jnp.concatenate([o['bts'], o['kts']], axis=0),
                             (((1,), (1,)), ((), ())), preferred_element_type=F32) for o in ops]
    for o, g in zip(ops, grams):
        lb = jnp.where(strict, g[:hc, :hc], 0.0)
        o['lkak'] = jnp.concatenate([jnp.where(strict, g[:hc, hc:], 0.0),
                                     jnp.where(incl, g[hc:, hc:], 0.0)], axis=0).astype(BF16)
        o['ab'] = jnp.where(incl, g[hc:, :hc], 0.0).astype(BF16)
        o['tinv'] = eye - lb
        o['pw'] = (-lb).astype(BF16)
    sq = [bdot(o['pw'], o['pw']) for o in ops]
    for o, q in zip(ops, sq):
        o['pw'] = q.astype(BF16)
    levels = int(np.log2(cl)) - 1
    for lvl in range(levels):
        both = [bdot(jnp.concatenate([o['pw'], o['tinv'].astype(BF16)], axis=0), o['pw']) for o in ops]
        for o, bo in zip(ops, both):
            o['tinv'] = o['tinv'] + bo[hc:]
            o['pw'] = bo[:hc].astype(BF16)
    lkak = [bdot(o['lkak'], o['vs']) for o in ops]
    wu = [bdot(o['tinv'].astype(BF16), jnp.concatenate([o['khs'], lv[:hc].astype(BF16)], axis=1))
          for o, lv in zip(ops, lkak)]
    wub = [jnp.concatenate([w[:, :gw], -w[:, gw:]], axis=1).astype(BF16) for w in wu]
    abw = [bdot(o['ab'], w) for o, w in zip(ops, wub)]
    gmat = [tn(o['btes'], w[:, :gw]) for o, w in zip(ops, wub)]
    hv = [tn(jnp.concatenate([o['btes'], o['ktes']], axis=0), jnp.concatenate([w[:, gw:], o['vs']], axis=0))
          for o, w in zip(ops, wub)]
    out = []
    for o, aw, lv, gm, h_v in zip(ops, abw, lkak, gmat, hv):
        qt = unstack(o['rhs'].astype(F32) - aw[:, :gw])
        yv = unstack(aw[:, gw:] + lv[hc:])
        out.append((qt, yv, jnp.where(di, o['gend'], 0.0) - gm, h_v))
    return out


def _scan_kernel(r_ref, lw_ref, k_ref, v_ref, kn_ref, b_ref, bonus_ref, g_ref, lng_ref, lnb_ref, o_ref, h_ref):
    gw = SCAN_GROUP * HEAD_DIM
    cl = RWKV_CHUNK
    ngrp = N_HEADS // SCAN_GROUP

    @pl.when(pl.program_id(1) == 0)
    def _():
        h_ref[...] = jnp.zeros_like(h_ref)

    where = [(grp, c) for grp in range(ngrp) for c in range(SCAN_CHUNKS_PER_STEP)]
    sl = lambda grp, c: (slice(c * cl, (c + 1) * cl), slice(grp * gw, (grp + 1) * gw))
    parts = _scan_chunks([tuple(ref[sl(grp, c)] for ref in (r_ref, lw_ref, k_ref, v_ref, kn_ref, b_ref))
                          for grp, c in where])
    ones = _head_ones(gw)
    bdot = lambda a, b: jnp.dot(a, b, preferred_element_type=F32)
    for grp in range(ngrp):
        h = h_ref[grp]
        for c in range(SCAN_CHUNKS_PER_STEP):
            rows, ln = sl(grp, c)
            qt, yv, abar, h_v = parts[where.index((grp, c))]
            hh, hl = _split(h)
            y = bdot(qt.astype(BF16), hh) + yv
            ah, al = _split(abar)
            ahl = bdot(jnp.concatenate([ah, al], axis=0), hh)
            h = ahl[:gw] + ahl[gw:] + bdot(ah, hl) + h_v
            mean = _dot2(y, ones) * (1.0 / HEAD_DIM)
            yc = y - mean
            var = _dot2(yc * yc, ones) * (1.0 / HEAD_DIM)
            yn = yc * lax.rsqrt(var + GN_EPS) * lng_ref[:, ln] + lnb_ref[:, ln] + bonus_ref[rows, ln]
            o_ref[rows, ln] = (yn * g_ref[rows, ln]).astype(o_ref.dtype)
        h_ref[grp] = h


def _rwkv_scan(r, lw, k, v, kn, b, bonus, g, ln_g, ln_b, bsz, seq):
    t, w = r.shape
    cl = RWKV_CHUNK * SCAN_CHUNKS_PER_STEP
    cps = seq // cl
    row = pl.BlockSpec((cl, w), lambda bi, c: (bi * cps + c, 0))
    vec = pl.BlockSpec((1, w), lambda bi, c: (0, 0))
    return pl.pallas_call(
        _scan_kernel,
        grid=(bsz, cps),
        in_specs=[row] * 8 + [vec, vec],
        out_specs=row,
        out_shape=jax.ShapeDtypeStruct((t, w), BF16),
        scratch_shapes=[pltpu.VMEM((N_HEADS // SCAN_GROUP, SCAN_GROUP * HEAD_DIM, SCAN_GROUP * HEAD_DIM), F32)],
        compiler_params=_cparams("parallel", "arbitrary"),
        name="rwkv_scan",
    )(r, lw, k, v, kn, b, bonus, g, ln_g, ln_b)


def _rwkv_branch(zrw, p, l, bsz, seq):
    w = WIDTH
    vec = lambda a: a.reshape(1, -1)
    zpad = lambda a, top: jnp.concatenate([a, jnp.zeros_like(a)] if top else [jnp.zeros_like(a), a], axis=0)
    outs = _rwkvprep(zrw, seq // TOKEN_TILE, vec(p['rwkv_mu'][l]), vec(p['rwkv_w0'][l]),
                     zpad(p['rwkv_w2'][l], True), vec(p['rwkv_a0'][l]), zpad(p['rwkv_a2'][l], False),
                     p['rwkv_g2'][l], vec(p['rwkv_k_k'][l]), vec(p['rwkv_k_a'][l]), vec(p['rwkv_r_k'][l]))
    return _rwkv_scan(*outs, vec(p['rwkv_ln_g'][l]), vec(p['rwkv_ln_b'][l]), bsz, seq)


def _merge_kernel(x_ref, om_ref, or_ref, gl_ref, mod_ref, g2_ref, wpm_ref, wpr_ref, wo_ref, wq_ref,
                  h_o, xn_o, q_o):
    d = x_ref.shape[1]
    mod = mod_ref[0]
    pm = jnp.dot(om_ref[...], wpm_ref[...], preferred_element_type=F32)
    pr = jnp.dot(or_ref[...], wpr_ref[...], preferred_element_type=F32)
    gl = gl_ref[...]
    mixed = _sigmoid(gl[:, :d]) * pm + _sigmoid(gl[:, d:]) * pr
    h = x_ref[...] + mod[:, 2 * d:3 * d] * _dot(mixed, wo_ref[...])
    h_o[...] = h
    xn2 = _modnorm(h, g2_ref[...], mod, d, 1)
    xn_o[...] = xn2
    q_o[...] = _dot(xn2, wq_ref[...])


def _merge(x2, o_moba, o_rwkv, gl, mod3, g2, wpm, wpr, wo, wq, tps):
    t, d = x2.shape
    tm = TOKEN_TILE
    w = WIDTH
    nq = wq.shape[1]
    row = lambda i: (i, 0)
    fixed = lambda i: (0, 0)
    return pl.pallas_call(
        _merge_kernel,
        grid=(t // tm,),
        in_specs=[pl.BlockSpec((tm, d), row), pl.BlockSpec((tm, w), row), pl.BlockSpec((tm, w), row),
                  pl.BlockSpec((tm, 2 * d), row),
                  pl.BlockSpec((1, 1, mod3.shape[2]), lambda i: (i // tps, 0, 0)),
                  pl.BlockSpec((1, d), fixed), pl.BlockSpec((w, d), fixed), pl.BlockSpec((w, d), fixed),
                  pl.BlockSpec((d, d), fixed), pl.BlockSpec((d, nq), fixed)],
        out_specs=[pl.BlockSpec((tm, d), row), pl.BlockSpec((tm, d), row), pl.BlockSpec((tm, nq), row)],
        out_shape=[jax.ShapeDtypeStruct((t, d), F32), jax.ShapeDtypeStruct((t, d), F32),
                   jax.ShapeDtypeStruct((t, nq), F32)],
        compiler_params=_cparams("parallel"),
        name="merge_proj",
    )(x2, o_moba, o_rwkv, gl, mod3, g2, wpm, wpr, wo, wq)


PEER_TILE = 128


def _top16_rows(problems):
    tm = problems[0][0].shape[1]
    rank = lax.broadcasted_iota(jnp.int32, (PEER_TOPK, tm), 0)
    keys = [k for _, k in problems]
    big = jnp.int32(1 << 30)

    def rnd(r, carry):
        out = []
        hit = rank == r
        for (s, vals, idxs), key in zip(carry, keys):
            m = jnp.max(s, axis=0, keepdims=True)
            ix = jnp.min(jnp.where(s == m, key, big), axis=0, keepdims=True)
            out.append((jnp.where(key == ix, -jnp.inf, s), jnp.where(hit, m, vals),
                        jnp.where(hit, ix.astype(F32), idxs)))
        return tuple(out)

    z = jnp.zeros((PEER_TOPK, tm), F32)
    fin = lax.fori_loop(0, PEER_TOPK, rnd, tuple((s, z, z) for s, _ in problems))
    return [(vals, idxs) for _, vals, idxs in fin]


def _peer_topk_kernel(q_ref, sk_ref, idx_o, gate_o):
    tm = q_ref.shape[0]
    kk = PEER_TOPK
    nk = PEER_NKEYS
    rank = lax.broadcasted_iota(jnp.int32, (kk, tm), 0)
    pos_key = lax.broadcasted_iota(jnp.int32, (nk, tm), 0)
    row8 = lax.broadcasted_iota(jnp.int32, (8, tm), 0)
    tdot = lambda a, b: lax.dot_general(a, b, (((1,), (1,)), ((), ())), preferred_element_type=F32)
    cand_key = jnp.concatenate([rank] + [row8 + a * kk for a in range(1, 8)] + [(row8 + 8) * kk], axis=0)

    tops = []
    for h0 in range(0, PEER_HEADS, 2):
        probs = []
        for hp in range(2 * h0, 2 * h0 + 4):
            qh, ql = _split(q_ref[:, hp * PEER_HALF:(hp + 1) * PEER_HALF])
            kh, kl = _split(sk_ref[hp])
            probs.append((tdot(kh, qh) + tdot(kh, ql) + tdot(kl, qh), pos_key))
        res = _top16_rows(probs)
        tops += [res[0:2], res[2:4]]

    idx_rows, gate_rows = [], []
    for h0 in range(0, PEER_HEADS, 2):
        probs = []
        for h in (h0, h0 + 1):
            (ts0, _), (ts1, _) = tops[h]
            cand = jnp.concatenate([ts0[0:1] + ts1] + [ts0[a:a + 1] + ts1[0:8] for a in range(1, 8)]
                                   + [ts0[8:16] + ts1[0:1]], axis=0)
            probs.append((cand, cand_key))
        for h, (best, pos) in zip((h0, h0 + 1), _top16_rows(probs)):
            (_, ti0), (_, ti1) = tops[h]
            pos = pos.astype(jnp.int32)
            ids = jnp.zeros((kk, tm), F32)
            for r in range(kk):
                pr = pos[r:r + 1, :]
                i0 = jnp.sum(jnp.where(rank == (pr >> 4), ti0, 0.0), axis=0, keepdims=True)
                i1 = jnp.sum(jnp.where(rank == (pr & (kk - 1)), ti1, 0.0), axis=0, keepdims=True)
                ids = jnp.where(rank == r, i0 * float(nk) + i1, ids)
            e = jnp.exp(best - jnp.max(best, axis=0, keepdims=True))
            gate_rows.append(e / jnp.sum(e, axis=0, keepdims=True))
            idx_rows.append(ids)
    idx_o[...] = jnp.concatenate(idx_rows, axis=0).T.astype(jnp.int32)
    gate_o[...] = jnp.concatenate(gate_rows, axis=0).T


def _peer_topk(q, subkeys16):
    t, nq = q.shape
    tm = PEER_TILE
    return pl.pallas_call(
        _peer_topk_kernel,
        grid=(t // tm,),
        in_specs=[pl.BlockSpec((tm, nq), lambda i: (i, 0)),
                  pl.BlockSpec(subkeys16.shape, lambda i: (0, 0, 0))],
        out_specs=[pl.BlockSpec((tm, 128), lambda i: (i, 0)), pl.BlockSpec((tm, 128), lambda i: (i, 0))],
        out_shape=[jax.ShapeDtypeStruct((t, 128), jnp.int32), jax.ShapeDtypeStruct((t, 128), F32)],
        compiler_params=_cparams("parallel"),
        name="peer_topk",
    )(q, subkeys16)


GATHER_TOKENS = 16
N_PICK = PEER_HEADS * PEER_TOPK
N_BURST = 4


def _peer_kernel(idx_cur, idx_nxt, gate_ref, xn_ref, h_ref, mod_ref, uv_hbm, o_ref, buf_a, buf_b, sem_a, sem_b):
    i = pl.program_id(0)
    n = pl.num_programs(0)
    tt = GATHER_TOKENS
    d = h_ref.shape[1]
    gt2 = mod_ref[0][:, 5 * d:6 * d]
    bufs = (buf_a, buf_b)
    sems = (sem_a, sem_b)
    per = N_PICK // N_BURST

    def row_copy(idx_ref, row, half, tok, e):
        return pltpu.make_async_copy(uv_hbm.at[idx_ref[row, e]], bufs[half].at[tok, pl.ds(e * 8, 8), :],
                                     sems[half].at[tok])

    def burst(idx_ref, row0, half, toks, b):
        for k, tok in enumerate(toks):
            for e in range(b * per, (b + 1) * per):
                row_copy(idx_ref, row0 + k, half, tok, e).start(priority=e % 2)

    def pair(half, tok0, idx_next, next_row0):
        toks = (tok0, tok0 + 1)
        rows = [half * tt + t for t in toks]
        for t, row in zip(toks, rows):
            for e in range(N_PICK):
                row_copy(idx_cur, row, half, t, e).wait()
        nch = d // 128
        words = [[bufs[half][t, pl.ds(c, N_PICK, stride=nch), :] for c in range(nch)] for t in toks]
        xts = [xn_ref[row:row + 1, :].astype(BF16) for row in rows]
        gates = [gate_ref[row:row + 1, :] for row in rows]
        hrows = [h_ref[row:row + 1, :] for row in rows]
        refill = lambda b: burst(idx_next, next_row0, 1 - half, toks, b)
        refill(0)
        us = [[lax.bitcast_convert_type(w << 16, F32).astype(BF16) for w in ws] for ws in words]
        refill(1)
        tdot = lambda a, b: lax.dot_general(a, b, (((1,), (1,)), ((), ())), preferred_element_type=F32)
        hds = [functools.reduce(lambda p, q: p + q, [tdot(x[:, c * 128:(c + 1) * 128], u[c]) for c in range(nch)])
               for x, u in zip(xts, us)]
        vs = [[lax.bitcast_convert_type(w & jnp.uint32(0xFFFF0000), F32).astype(BF16) for w in ws] for ws in words]
        refill(2)
        hds = [(0.5 * h * (1.0 + lax.erf(h * (2.0 ** -0.5))) * g).astype(BF16) for h, g in zip(hds, gates)]
        outs = [jnp.concatenate([jnp.dot(h, v[c], preferred_element_type=F32) for c in range(nch)], axis=1)
                for h, v in zip(hds, vs)]
        refill(3)
        for row, hr, out in zip(rows, hrows, outs):
            o_ref[row:row + 1, :] = hr + gt2 * out

    @pl.when(i == 0)
    def _():
        for tok in range(tt):
            for e in range(N_PICK):
                row_copy(idx_cur, tok, 0, tok, e).start(priority=e % 2)

    for tok0 in range(0, tt, 2):
        pair(0, tok0, idx_cur, tt + tok0)
    for tok0 in range(0, tt, 2):
        pair(1, tok0, idx_nxt, tok0)

    @pl.when(i == n - 1)
    def _():
        for tok in range(tt):
            for e in range(N_PICK):
                row_copy(idx_nxt, tok, 0, tok, e).wait()


def _peer(idx, gates, xn2, h1, mod3, uv_words, tokens_per_seq):
    t, d = h1.shape
    tt = GATHER_TOKENS
    blk = 2 * tt
    nsteps = t // blk
    row = lambda i: (i, 0)
    buf = pltpu.VMEM((tt, N_PICK * (d // 128), 128), jnp.uint32)
    return pl.pallas_call(
        _peer_kernel,
        grid=(nsteps,),
        in_specs=[pl.BlockSpec((blk, N_PICK), row, memory_space=pltpu.SMEM),
                  pl.BlockSpec((blk, N_PICK), lambda i: (jnp.minimum(i + 1, nsteps - 1), 0), memory_space=pltpu.SMEM),
                  pl.BlockSpec((blk, N_PICK), row), pl.BlockSpec((blk, d), row), pl.BlockSpec((blk, d), row),
                  pl.BlockSpec((1, 1, mod3.shape[2]), lambda i: ((i * blk) // tokens_per_seq, 0, 0)),
                  pl.BlockSpec(memory_space=pl.ANY)],
        out_specs=pl.BlockSpec((blk, d), row),
        out_shape=jax.ShapeDtypeStruct((t, d), F32),
        scratch_shapes=[buf, buf, pltpu.SemaphoreType.DMA((tt,)), pltpu.SemaphoreType.DMA((tt,))],
        compiler_params=_cparams("arbitrary"),
        name="peer_experts",
    )(idx, idx, gates, xn2, h1, mod3, uv_words)


def _pack_uv(u, v):
    ub = lax.bitcast_convert_type(u.astype(BF16), jnp.uint16).astype(jnp.uint32)
    vb = lax.bitcast_convert_type(v.astype(BF16), jnp.uint16).astype(jnp.uint32)
    return (ub | (vb << 16)).reshape(u.shape[0], u.shape[1] // 128, 128)


def _rope_inv128():
    half = ROPE_DIM // 2
    inv = ROPE_THETA ** (-(jnp.arange(half, dtype=F32) * 2.0) / ROPE_DIM)
    lane = np.arange(128) % HEAD_DIM
    out = jnp.where(lane < ROPE_DIM, inv[lane % half], 0.0)
    return out.reshape(1, 128)


def kernel(x, c, positions, w_ada, b_ada, norm1_g, w_in, q_norm_g, k_norm_g, rwkv_mu, rwkv_w0, rwkv_w2, rwkv_a0, rwkv_a2, rwkv_g2, rwkv_k_k, rwkv_k_a, rwkv_r_k, rwkv_ln_g, rwkv_ln_b, w_proj_moba, w_proj_rwkv, w_out, norm2_g, peer_wq, peer_subkeys, peer_u, peer_v):
    bsz, seq, d = x.shape
    t = bsz * seq
    tps = seq // TOKEN_TILE
    l = 0
    x2 = x.reshape(t, d)
    mod3 = _adaln(c, w_ada[l], b_ada[l]).reshape(bsz, 1, 6 * d)
    w_in_b = w_in[l].astype(BF16)
    g1 = norm1_g[l].reshape(1, d)
    qkv = _inproj(x2, mod3, g1, w_in_b[:, :3 * WIDTH], tps)
    qb, kb, vb, km = _qkprep(qkv, positions.reshape(t, 1), jnp.tile(q_norm_g[l], N_HEADS).reshape(1, WIDTH),
                             jnp.tile(k_norm_g[l], N_HEADS).reshape(1, WIDTH), _rope_inv128())
    o_moba = _moba(qb, kb, vb, km, bsz, seq)
    params = dict(rwkv_mu=rwkv_mu, rwkv_w0=rwkv_w0, rwkv_w2=rwkv_w2, rwkv_a0=rwkv_a0, rwkv_a2=rwkv_a2,
                  rwkv_g2=rwkv_g2, rwkv_k_k=rwkv_k_k, rwkv_k_a=rwkv_k_a, rwkv_r_k=rwkv_r_k.reshape(1, WIDTH),
                  rwkv_ln_g=rwkv_ln_g, rwkv_ln_b=rwkv_ln_b)
    zrw = _inproj(x2, mod3, g1, w_in_b[:, 3 * WIDTH:3 * WIDTH + SHIFT_WIDTH], tps)
    o_rwkv = _rwkv_branch(zrw, params, l, bsz, seq)
    gl = _inproj(x2, mod3, g1, w_in_b[:, 3 * WIDTH + SHIFT_WIDTH:], tps)
    h1, xn2, q = _merge(x2, o_moba.reshape(t, WIDTH), o_rwkv, gl, mod3, norm2_g[l].reshape(1, d),
                        w_proj_moba[l].astype(BF16), w_proj_rwkv[l].astype(BF16), w_out[l].astype(BF16),
                        peer_wq[l].astype(BF16), tps)
    idx, gates = _peer_topk(q, peer_subkeys[l].reshape(2 * PEER_HEADS, PEER_NKEYS, PEER_HALF))
    out = _peer(idx, gates, xn2, h1, mod3, _pack_uv(peer_u[l], peer_v[l]), seq)
    return out.reshape(bsz, seq, d)
```

```python
import functools

import numpy as np
import jax
import jax.numpy as jnp
from jax import lax
from jax.experimental import pallas as pl
from jax.experimental.pallas import tpu as pltpu

F32 = jnp.float32
BF16 = jnp.bfloat16

HEAD_DIM = 64
N_HEADS = 8
WIDTH = N_HEADS * HEAD_DIM
MOBA_BLOCK = 256
MOBA_TOPK = 3
ROPE_THETA = 500000.0
ROPE_DIM = HEAD_DIM // 4
LORA_W = 64
LORA_A = 64
LORA_G = 128
SHIFT_WIDTH = 3 * WIDTH + LORA_W + LORA_A + LORA_G
GN_EPS = 64e-5
RMS_EPS = 1e-6
NEG_INF = -1e30
LOG2_E = 1.4426950408889634
PEER_HEADS = 8
PEER_NKEYS = 128
PEER_HALF = 128
PEER_TOPK = 16

TOKEN_TILE = 256
V7X_VMEM_LIMIT = 48 * 1024 * 1024


def _cparams(*sem, flags=None):
    return pltpu.CompilerParams(dimension_semantics=sem, vmem_limit_bytes=V7X_VMEM_LIMIT, flags=flags)


def _dot(a, b):
    return jnp.dot(a.astype(BF16), b.astype(BF16), preferred_element_type=F32)


def _dot_t(a, b):
    return lax.dot_general(a.astype(BF16), b.astype(BF16), (((1,), (1,)), ((), ())),
                           preferred_element_type=F32)


def _split(a):
    hi = a.astype(BF16)
    lo = (a - hi.astype(F32)).astype(BF16)
    return hi, lo


def _dot3(a, b):
    ah, al = _split(a)
    bh, bl = _split(b)
    return (jnp.dot(ah, bh, preferred_element_type=F32) + jnp.dot(ah, bl, preferred_element_type=F32)
            + jnp.dot(al, bh, preferred_element_type=F32))


def _dot2(a, b_exact):
    ah, al = _split(a)
    return jnp.dot(ah, b_exact, preferred_element_type=F32) + jnp.dot(al, b_exact, preferred_element_type=F32)


def _head_ones(n):
    r = lax.broadcasted_iota(jnp.int32, (n, n), 0) >> 6
    c = lax.broadcasted_iota(jnp.int32, (n, n), 1) >> 6
    return (r == c).astype(BF16)


def _sigmoid(x):
    return 1.0 / (1.0 + jnp.exp(-x))


def _ada_kernel(c_ref, w_ref, b_ref, o_ref):
    c = c_ref[...]
    o_ref[...] = _dot3(c * _sigmoid(c), w_ref[...]) + b_ref[...]


def _adaln(c, w_ada, b_ada):
    bsz, d = c.shape
    n = w_ada.shape[1]
    nb = 1024
    return pl.pallas_call(
        _ada_kernel,
        grid=(n // nb,),
        in_specs=[pl.BlockSpec((bsz, d), lambda j: (0, 0)),
                  pl.BlockSpec((d, nb), lambda j: (0, j)),
                  pl.BlockSpec((1, nb), lambda j: (0, j))],
        out_specs=pl.BlockSpec((bsz, nb), lambda j: (0, j)),
        out_shape=jax.ShapeDtypeStruct((bsz, n), F32),
        compiler_params=_cparams("arbitrary"),
        name="adaln_mod",
    )(c, w_ada, b_ada.reshape(1, n))


def _modnorm(x, g, mod, d, which):
    ms = jnp.mean(x * x, axis=-1, keepdims=True)
    y = x * lax.rsqrt(ms + RMS_EPS) * g
    sh = mod[:, (3 * which) * d:(3 * which + 1) * d]
    sc = mod[:, (3 * which + 1) * d:(3 * which + 2) * d]
    return y * (1.0 + sc) + sh


def _inproj_kernel(x_ref, mod_ref, g_ref, w_ref, o_ref):
    d = x_ref.shape[1]
    xn = _modnorm(x_ref[...], g_ref[...], mod_ref[0], d, 0)
    o_ref[...] = jnp.dot(xn.astype(BF16), w_ref[...], preferred_element_type=F32)


def _inproj(x2, mod3, g, w_bf16, tps):
    t, d = x2.shape
    n = w_bf16.shape[1]
    tm = TOKEN_TILE
    return pl.pallas_call(
        _inproj_kernel,
        grid=(t // tm,),
        in_specs=[pl.BlockSpec((tm, d), lambda i: (i, 0)),
                  pl.BlockSpec((1, 1, mod3.shape[2]), lambda i: (i // tps, 0, 0)),
                  pl.BlockSpec((1, d), lambda i: (0, 0)),
                  pl.BlockSpec((d, n), lambda i: (0, 0))],
        out_specs=pl.BlockSpec((tm, n), lambda i: (i, 0)),
        out_shape=jax.ShapeDtypeStruct((t, n), F32),
        compiler_params=_cparams("parallel"),
        name="in_proj",
    )(x2, mod3, g, w_bf16)


def _qkprep_kernel(qkv_ref, pos_ref, gq_ref, gk_ref, inv_ref, qo_ref, ko_ref, vo_ref, km_ref):
    w = WIDTH
    ones = _head_ones(w)
    ang = pos_ref[...].astype(F32) * inv_ref[...]
    cos = jnp.tile(jnp.cos(ang), (1, w // 128))
    sin = jnp.tile(jnp.sin(ang), (1, w // 128))
    dd = lax.broadcasted_iota(jnp.int32, (1, w), 1) & (HEAD_DIM - 1)
    half = ROPE_DIM // 2
    s_lo = jnp.where(dd < half, -1.0, 0.0) * sin
    s_hi = jnp.where((dd >= half) & (dd < ROPE_DIM), 1.0, 0.0) * sin

    def norm_rope(xh, g):
        ssq = _dot2(xh * xh, ones)
        y = xh * lax.rsqrt(ssq * (1.0 / HEAD_DIM) + RMS_EPS) * g
        return y * cos + pltpu.roll(y, w - half, 1) * s_lo + pltpu.roll(y, half, 1) * s_hi

    q = norm_rope(qkv_ref[:, 0:w], gq_ref[...])
    k = norm_rope(qkv_ref[:, w:2 * w], gk_ref[...])
    qo_ref[...] = (q * (HEAD_DIM ** -0.5 * LOG2_E)).astype(BF16)
    ko_ref[...] = k.astype(BF16)
    vo_ref[...] = qkv_ref[:, 2 * w:3 * w].astype(BF16)
    km_ref[0] = jnp.mean(k, axis=0, keepdims=True)


def _qkprep(qkv, pos2, gq, gk, inv128):
    t = qkv.shape[0]
    tm = MOBA_BLOCK
    w = WIDTH
    row = lambda i: (i, 0)
    fixed = lambda i: (0, 0)
    return pl.pallas_call(
        _qkprep_kernel,
        grid=(t // tm,),
        in_specs=[pl.BlockSpec((tm, 3 * w), row), pl.BlockSpec((tm, 1), row),
                  pl.BlockSpec((1, w), fixed), pl.BlockSpec((1, w), fixed), pl.BlockSpec((1, 128), fixed)],
        out_specs=[pl.BlockSpec((tm, w), row), pl.BlockSpec((tm, w), row), pl.BlockSpec((tm, w), row),
                   pl.BlockSpec((1, 1, w), lambda i: (i, 0, 0))],
        out_shape=[jax.ShapeDtypeStruct((t, w), BF16), jax.ShapeDtypeStruct((t, w), BF16),
                   jax.ShapeDtypeStruct((t, w), BF16), jax.ShapeDtypeStruct((t // tm, 1, w), F32)],
        compiler_params=_cparams("parallel"),
        name="moba_qk_prep",
    )(qkv, pos2, gq, gk, inv128)


def _moba_kernel(q_ref, k_ref, v_ref, km_ref, o_ref):
    blk = MOBA_BLOCK
    i = pl.program_id(2)
    q = q_ref[0]
    km = km_ref[0]
    nbp = km.shape[0]
    lane = lax.broadcasted_iota(jnp.int32, (blk, 128), 1)
    col = lax.broadcasted_iota(jnp.int32, (blk, nbp), 1)
    brow = lax.broadcasted_iota(jnp.int32, (nbp, blk), 0)
    qpos = lax.broadcasted_iota(jnp.int32, (blk, blk), 0)
    kpos = lax.broadcasted_iota(jnp.int32, (blk, blk), 1)
    row0 = pl.multiple_of(i * blk, blk)
    k_own = k_ref[0, pl.ds(row0, blk), :]
    v_own = v_ref[0, pl.ds(row0, blk), :]

    heads = range(2)
    in_head = [(lane >> 6) == hh for hh in heads]
    qs = [jnp.where(m, q, jnp.zeros_like(q)) for m in in_head]
    ones_other = lambda v, hh: jnp.where(in_head[hh], v, jnp.ones_like(v))
    gs = [jnp.where(brow < i, _dot_t(km, qh), -jnp.inf) for qh in qs]
    sels = [jnp.zeros((nbp, blk), jnp.bool_) for _ in heads]
    for _ in range(MOBA_TOPK):
        ms = [jnp.max(g, axis=0, keepdims=True) for g in gs]
        idxs = [jnp.min(jnp.where(g == m, brow, nbp), axis=0, keepdims=True) for g, m in zip(gs, ms)]
        picks = [(brow == ix) & (g > -jnp.inf) for g, ix in zip(gs, idxs)]
        sels = [sl | pk for sl, pk in zip(sels, picks)]
        gs = [jnp.where(pk, -jnp.inf, g) for g, pk in zip(gs, picks)]
    biases = [jnp.where(sl, 0.0, NEG_INF).T for sl in sels]
    s0 = [jnp.where(kpos <= qpos, _dot_t(qh, k_own), NEG_INF) for qh in qs]
    m0 = [jnp.max(a, axis=-1, keepdims=True) for a in s0]
    p0 = [jnp.exp2(a - m) for a, m in zip(s0, m0)]
    acc0 = [_dot(p, ones_other(v_own, hh)) for hh, p in zip(heads, p0)]
    carry0 = [m0[0], acc0[0], m0[1], acc0[1]]

    def body(jj, carry):
        r0 = pl.multiple_of(jj * (2 * blk), 2 * blk)
        kj = k_ref[0, pl.ds(r0, 2 * blk), :]
        vj = v_ref[0, pl.ds(r0, 2 * blk), :]
        hu = [(hh, u) for hh in heads for u in range(2)]
        raw = [_dot_t(qs[hh], kj[u * blk:(u + 1) * blk]) for hh, u in hu]
        bj = [jnp.sum(jnp.where(col == 2 * jj + u, biases[hh], 0.0), axis=-1, keepdims=True) for hh, u in hu]
        ss = [a + b for a, b in zip(raw, bj)]
        mx = [jnp.max(a, axis=-1, keepdims=True) for a in ss]
        m_new = [jnp.maximum(carry[2 * hh], jnp.maximum(mx[2 * hh], mx[2 * hh + 1])) for hh in heads]
        alpha = [jnp.exp2(carry[2 * hh] - m_new[hh]) for hh in heads]
        ps = [jnp.exp2(a - m_new[hh]) for a, (hh, u) in zip(ss, hu)]
        pcat = [jnp.concatenate([ps[2 * hh], ps[2 * hh + 1]], axis=1) for hh in heads]
        lane2 = lax.broadcasted_iota(jnp.int32, (2 * blk, 128), 1)
        pv = [_dot(pc, jnp.where((lane2 >> 6) == hh, vj, jnp.ones_like(vj))) for hh, pc in zip(heads, pcat)]
        out = []
        for hh in heads:
            out += [m_new[hh], alpha[hh] * carry[2 * hh + 1] + pv[hh]]
        return tuple(out)

    fin = lax.fori_loop(0, (i + 1) // 2, body, tuple(carry0))
    outs = [fin[2 * hh + 1] / pltpu.roll(fin[2 * hh + 1], HEAD_DIM, 1) for hh in heads]
    o_ref[0] = jnp.where(lane < HEAD_DIM, outs[0], outs[1]).astype(o_ref.dtype)


def _moba(qb, kb, vb, km, bsz, seq):
    w = WIDTH
    nb = seq // MOBA_BLOCK
    q3 = qb.reshape(bsz, seq, w)
    k3 = kb.reshape(bsz, seq, w)
    v3 = vb.reshape(bsz, seq, w)
    nbp = 128
    assert nb <= nbp
    km3 = jnp.pad(km.reshape(bsz, nb, w), ((0, 0), (0, nbp - nb), (0, 0)))
    return pl.pallas_call(
        _moba_kernel,
        grid=(bsz, w // 128, nb),
        in_specs=[pl.BlockSpec((1, MOBA_BLOCK, 128), lambda b, p, i: (b, i, p)),
                  pl.BlockSpec((1, seq, 128), lambda b, p, i: (b, 0, p)),
                  pl.BlockSpec((1, seq, 128), lambda b, p, i: (b, 0, p)),
                  pl.BlockSpec((1, nbp, 128), lambda b, p, i: (b, 0, p))],
        out_specs=pl.BlockSpec((1, MOBA_BLOCK, 128), lambda b, p, i: (b, i, p)),
        out_shape=jax.ShapeDtypeStruct((bsz, seq, w), BF16),
        compiler_params=_cparams("parallel", "parallel", "arbitrary"),
        name="moba_attention",
    )(q3, k3, v3, km3)


def _rwkvprep_kernel(z_ref, zp_ref, mu_ref, w0_ref, w2_ref, a0_ref, a2_ref, g2_ref, kk_ref, ka_ref, rk_ref,
                     r_o, lw_o, k_o, v_o, kn_o, b_o, bonus_o, g_o, *, tps):
    i = pl.program_id(0)
    w = WIDTH
    z = z_ref[...]
    tm = z.shape[0]
    prev_last = jnp.where((i % tps) == 0, 0.0, zp_ref[7:8, :])
    rows = lax.broadcasted_iota(jnp.int32, (tm, 1), 0)
    prev = jnp.where(rows == 0, prev_last, pltpu.roll(z, 1, 0))
    zs = z + (prev - z) * mu_ref[...]
    zr, zk, zv = zs[:, 0:w], zs[:, w:2 * w], zs[:, 2 * w:3 * w]
    zwa = zs[:, 3 * w:3 * w + 128]
    zg = zs[:, 3 * w + 128:3 * w + 256]
    nw = -(w0_ref[...] + _dot3(jnp.tanh(zwa), w2_ref[...]))
    softplus = jnp.maximum(nw, 0.0) + jnp.log(1.0 + jnp.exp(-jnp.abs(nw)))
    lw_o[...] = -jnp.exp(-softplus - 0.5)
    a = _sigmoid(a0_ref[...] + _dot3(zwa, a2_ref[...]))
    g_o[...] = _dot3(_sigmoid(zg), g2_ref[...])
    ones = _head_ones(w)
    kn = zk * kk_ref[...]
    kn = kn / jnp.maximum(jnp.sqrt(_dot2(kn * kn, ones)), 1e-12)
    k = zk * (1.0 + (a - 1.0) * ka_ref[...])
    r_o[...] = zr
    k_o[...] = k
    v_o[...] = zv
    kn_o[...] = kn
    b_o[...] = kn * a
    bonus_o[...] = _dot2(zr * k * rk_ref[...], ones) * zv


def _rwkvprep(zrw, tps, mu, w0, w2aug, a0, a2aug, g2, k_k, k_a, r_k):
    t = zrw.shape[0]
    tm = TOKEN_TILE
    w = WIDTH
    row = lambda i: (i, 0)
    fixed = lambda i: (0, 0)
    vec = pl.BlockSpec((1, w), fixed)
    outs = [jax.ShapeDtypeStruct((t, w), F32)] * 8
    return pl.pallas_call(
        functools.partial(_rwkvprep_kernel, tps=tps),
        grid=(t // tm,),
        in_specs=[pl.BlockSpec((tm, SHIFT_WIDTH), row),
                  pl.BlockSpec((8, SHIFT_WIDTH), lambda i: (jnp.maximum(i * (tm // 8) - 1, 0), 0)),
                  pl.BlockSpec((1, SHIFT_WIDTH), fixed),
                  vec, pl.BlockSpec((128, w), fixed), vec, pl.BlockSpec((128, w), fixed),
                  pl.BlockSpec((LORA_G, w), fixed), vec, vec, vec],
        out_specs=[pl.BlockSpec((tm, w), row)] * 8,
        out_shape=outs,
        compiler_params=_cparams("parallel"),
        name="rwkv_prep",
    )(zrw, zrw, mu, w0, w2aug, a0, a2aug, g2, k_k, k_a, r_k)


RWKV_CHUNK = 32


SCAN_GROUP = 4
SCAN_CHUNKS_PER_STEP = 8


def _scan_chunks(probs):
    cl, gw = probs[0][0].shape
    hc = SCAN_GROUP * cl
    bdot = lambda a, b: jnp.dot(a, b, preferred_element_type=F32)
    tn = lambda a, b: lax.dot_general(a, b, (((0,), (0,)), ((), ())), preferred_element_type=F32)
    tri = (lax.broadcasted_iota(jnp.int32, (cl, cl), 0) >= lax.broadcasted_iota(jnp.int32, (cl, cl), 1)).astype(BF16)
    hmask = (lax.broadcasted_iota(jnp.int32, (hc, gw), 0) // cl) == (lax.broadcasted_iota(jnp.int32, (hc, gw), 1) >> 6)
    stack = lambda a: jnp.where(hmask, jnp.tile(a, (SCAN_GROUP, 1)), 0.0).astype(BF16)
    unstack = lambda a: functools.reduce(lambda p, q: p + q, [a[h * cl:(h + 1) * cl] for h in range(SCAN_GROUP)])
    ri = lax.broadcasted_iota(jnp.int32, (hc, hc), 0)
    ci = lax.broadcasted_iota(jnp.int32, (hc, hc), 1)
    strict = ri > ci
    incl = ri >= ci
    eye = jnp.where(ri == ci, 1.0, 0.0)
    di = lax.broadcasted_iota(jnp.int32, (gw, gw), 0) == lax.broadcasted_iota(jnp.int32, (gw, gw), 1)

    cs2 = [bdot(tri, jnp.concatenate(_split(p[1]), axis=1)) for p in probs]
    ops = []
    for (r, lw, k, v, kn, b), c2 in zip(probs, cs2):
        cs = c2[:, :gw] + c2[:, gw:]
        cs_end = cs[cl - 1:cl, :]
        inv = jnp.exp(-cs)
        dec_end = jnp.exp(cs_end - cs)
        ops.append(dict(khs=stack(kn * jnp.exp(cs - lw)),
                        rhs=stack(r * jnp.exp(cs)),
                        bts=stack(b * inv), kts=stack(k * inv), vs=stack(v),
                        btes=stack(b * dec_end), ktes=stack(k * dec_end), gend=jnp.exp(cs_end)))
    grams = [lax.dot_general(jnp.concatenate([o['khs'], o['rhs']], axis=0),
                             jnp.concatenate([o['bts'], o['kts']], axis=0),
                             (((1,), (1,)), ((), ())), preferred_element_type=F32) for o in ops]
    for o, g in zip(ops, grams):
        lb = jnp.where(strict, g[:hc, :hc], 0.0)
        o['lkak'] = jnp.concatenate([jnp.where(strict, g[:hc, hc:], 0.0),
                                     jnp.where(incl, g[hc:, hc:], 0.0)], axis=0).astype(BF16)
        o['ab'] = jnp.where(incl, g[hc:, :hc], 0.0).astype(BF16)
        o['tinv'] = eye - lb
        o['pw'] = (-lb).astype(BF16)
    sq = [bdot(o['pw'], o['pw']) for o in ops]
    for o, q in zip(ops, sq):
        o['pw'] = q.astype(BF16)
    levels = int(np.log2(cl)) - 1
    for lvl in range(levels):
        both = [bdot(jnp.concatenate([o['pw'], o['tinv'].astype(BF16)], axis=0), o['pw']) for o in ops]
        for o, bo in zip(ops, both):
            o['tinv'] = o['tinv'] + bo[hc:]
            o['pw'] = bo[:hc].astype(BF16)
    lkak = [bdot(o['lkak'], o['vs']) for o in ops]
    wu = [bdot(o['tinv'].astype(BF16), jnp.concatenate([o['khs'], lv[:hc].astype(BF16)], axis=1))
          for o, lv in zip(ops, lkak)]
    wub = [jnp.concatenate([w[:, :gw], -w[:, gw:]], axis=1).astype(BF16) for w in wu]
    abw = [bdot(o['ab'], w) for o, w in zip(ops, wub)]
    gmat = [tn(o['btes'], w[:, :gw]) for o, w in zip(ops, wub)]
    hv = [tn(jnp.concatenate([o['btes'], o['ktes']], axis=0), jnp.concatenate([w[:, gw:], o['vs']], axis=0))
          for o, w in zip(ops, wub)]
    out = []
    for o, aw, lv, gm, h_v in zip(ops, abw, lkak, gmat, hv):
        qt = unstack(o['rhs'].astype(F32) - aw[:, :gw])
        yv = unstack(aw[:, gw:] + lv[hc:])
        out.append((qt, yv, jnp.where(di, o['gend'], 0.0) - gm, h_v))
    return out


def _scan_kernel(r_ref, lw_ref, k_ref, v_ref, kn_ref, b_ref, bonus_ref, g_ref, lng_ref, lnb_ref, o_ref, h_ref):
    gw = SCAN_GROUP * HEAD_DIM
    cl = RWKV_CHUNK
    ngrp = N_HEADS // SCAN_GROUP

    @pl.when(pl.program_id(1) == 0)
    def _():
        h_ref[...] = jnp.zeros_like(h_ref)

    where = [(grp, c) for grp in range(ngrp) for c in range(SCAN_CHUNKS_PER_STEP)]
    sl = lambda grp, c: (slice(c * cl, (c + 1) * cl), slice(grp * gw, (grp + 1) * gw))
    parts = _scan_chunks([tuple(ref[sl(grp, c)] for ref in (r_ref, lw_ref, k_ref, v_ref, kn_ref, b_ref))
                          for grp, c in where])
    ones = _head_ones(gw)
    bdot = lambda a, b: jnp.dot(a, b, preferred_element_type=F32)
    for grp in range(ngrp):
        h = h_ref[grp]
        for c in range(SCAN_CHUNKS_PER_STEP):
            rows, ln = sl(grp, c)
            qt, yv, abar, h_v = parts[where.index((grp, c))]
            hh, hl = _split(h)
            y = bdot(qt.astype(BF16), hh) + yv
            ah, al = _split(abar)
            ahl = bdot(jnp.concatenate([ah, al], axis=0), hh)
            h = ahl[:gw] + ahl[gw:] + bdot(ah, hl) + h_v
            mean = _dot2(y, ones) * (1.0 / HEAD_DIM)
            yc = y - mean
            var = _dot2(yc * yc, ones) * (1.0 / HEAD_DIM)
            yn = yc * lax.rsqrt(var + GN_EPS) * lng_ref[:, ln] + lnb_ref[:, ln] + bonus_ref[rows, ln]
            o_ref[rows, ln] = (yn * g_ref[rows, ln]).astype(o_ref.dtype)
        h_ref[grp] = h


def _rwkv_scan(r, lw, k, v, kn, b, bonus, g, ln_g, ln_b, bsz, seq):
    t, w = r.shape
    cl = RWKV_CHUNK * SCAN_CHUNKS_PER_STEP
    cps = seq // cl
    row = pl.BlockSpec((cl, w), lambda bi, c: (bi * cps + c, 0))
    vec = pl.BlockSpec((1, w), lambda bi, c: (0, 0))
    return pl.pallas_call(
        _scan_kernel,
        grid=(bsz, cps),
        in_specs=[row] * 8 + [vec, vec],
        out_specs=row,
        out_shape=jax.ShapeDtypeStruct((t, w), BF16),
        scratch_shapes=[pltpu.VMEM((N_HEADS // SCAN_GROUP, SCAN_GROUP * HEAD_DIM, SCAN_GROUP * HEAD_DIM), F32)],
        compiler_params=_cparams("parallel", "arbitrary"),
        name="rwkv_scan",
    )(r, lw, k, v, kn, b, bonus, g, ln_g, ln_b)


def _rwkv_branch(zrw, p, l, bsz, seq):
    w = WIDTH
    vec = lambda a: a.reshape(1, -1)
    zpad = lambda a, top: jnp.concatenate([a, jnp.zeros_like(a)] if top else [jnp.zeros_like(a), a], axis=0)
    outs = _rwkvprep(zrw, seq // TOKEN_TILE, vec(p['rwkv_mu'][l]), vec(p['rwkv_w0'][l]),
                     zpad(p['rwkv_w2'][l], True), vec(p['rwkv_a0'][l]), zpad(p['rwkv_a2'][l], False),
                     p['rwkv_g2'][l], vec(p['rwkv_k_k'][l]), vec(p['rwkv_k_a'][l]), vec(p['rwkv_r_k'][l]))
    return _rwkv_scan(*outs, vec(p['rwkv_ln_g'][l]), vec(p['rwkv_ln_b'][l]), bsz, seq)


def _merge_kernel(x_ref, om_ref, or_ref, gl_ref, mod_ref, g2_ref, wpm_ref, wpr_ref, wo_ref, wq_ref,
                  h_o, xn_o, q_o):
    d = x_ref.shape[1]
    mod = mod_ref[0]
    pm = jnp.dot(om_ref[...], wpm_ref[...], preferred_element_type=F32)
    pr = jnp.dot(or_ref[...], wpr_ref[...], preferred_element_type=F32)
    gl = gl_ref[...]
    mixed = _sigmoid(gl[:, :d]) * pm + _sigmoid(gl[:, d:]) * pr
    h = x_ref[...] + mod[:, 2 * d:3 * d] * _dot(mixed, wo_ref[...])
    h_o[...] = h
    xn2 = _modnorm(h, g2_ref[...], mod, d, 1)
    xn_o[...] = xn2
    q_o[...] = _dot(xn2, wq_ref[...])


def _merge(x2, o_moba, o_rwkv, gl, mod3, g2, wpm, wpr, wo, wq, tps):
    t, d = x2.shape
    tm = TOKEN_TILE
    w = WIDTH
    nq = wq.shape[1]
    row = lambda i: (i, 0)
    fixed = lambda i: (0, 0)
    return pl.pallas_call(
        _merge_kernel,
        grid=(t // tm,),
        in_specs=[pl.BlockSpec((tm, d), row), pl.BlockSpec((tm, w), row), pl.BlockSpec((tm, w), row),
                  pl.BlockSpec((tm, 2 * d), row),
                  pl.BlockSpec((1, 1, mod3.shape[2]), lambda i: (i // tps, 0, 0)),
                  pl.BlockSpec((1, d), fixed), pl.BlockSpec((w, d), fixed), pl.BlockSpec((w, d), fixed),
                  pl.BlockSpec((d, d), fixed), pl.BlockSpec((d, nq), fixed)],
        out_specs=[pl.BlockSpec((tm, d), row), pl.BlockSpec((tm, d), row), pl.BlockSpec((tm, nq), row)],
        out_shape=[jax.ShapeDtypeStruct((t, d), F32), jax.ShapeDtypeStruct((t, d), F32),
                   jax.ShapeDtypeStruct((t, nq), F32)],
        compiler_params=_cparams("parallel"),
        name="merge_proj",
    )(x2, o_moba, o_rwkv, gl, mod3, g2, wpm, wpr, wo, wq)


PEER_TILE = 128


def _top16_rows(problems):
    tm = problems[0][0].shape[1]
    rank = lax.broadcasted_iota(jnp.int32, (PEER_TOPK, tm), 0)
    keys = [k for _, k in problems]
    big = jnp.int32(1 << 30)

    def rnd(r, carry):
        out = []
        hit = rank == r
        for (s, vals, idxs), key in zip(carry, keys):
            m = jnp.max(s, axis=0, keepdims=True)
            ix = jnp.min(jnp.where(s == m, key, big), axis=0, keepdims=True)
            out.append((jnp.where(key == ix, -jnp.inf, s), jnp.where(hit, m, vals),
                        jnp.where(hit, ix.astype(F32), idxs)))
        return tuple(out)

    z = jnp.zeros((PEER_TOPK, tm), F32)
    fin = lax.fori_loop(0, PEER_TOPK, rnd, tuple((s, z, z) for s, _ in problems))
    return [(vals, idxs) for _, vals, idxs in fin]


def _peer_topk_kernel(q_ref, sk_ref, idx_o, gate_o):
    tm = q_ref.shape[0]
    kk = PEER_TOPK
    nk = PEER_NKEYS
    rank = lax.broadcasted_iota(jnp.int32, (kk, tm), 0)
    pos_key = lax.broadcasted_iota(jnp.int32, (nk, tm), 0)
    row8 = lax.broadcasted_iota(jnp.int32, (8, tm), 0)
    tdot = lambda a, b: lax.dot_general(a, b, (((1,), (1,)), ((), ())), preferred_element_type=F32)
    cand_key = jnp.concatenate([rank] + [row8 + a * kk for a in range(1, 8)] + [(row8 + 8) * kk], axis=0)

    tops = []
    for h0 in range(0, PEER_HEADS, 2):
        probs = []
        for hp in range(2 * h0, 2 * h0 + 4):
            qh, ql = _split(q_ref[:, hp * PEER_HALF:(hp + 1) * PEER_HALF])
            kh, kl = _split(sk_ref[hp])
            probs.append((tdot(kh, qh) + tdot(kh, ql) + tdot(kl, qh), pos_key))
        res = _top16_rows(probs)
        tops += [res[0:2], res[2:4]]

    idx_rows, gate_rows = [], []
    for h0 in range(0, PEER_HEADS, 2):
        probs = []
        for h in (h0, h0 + 1):
            (ts0, _), (ts1, _) = tops[h]
            cand = jnp.concatenate([ts0[0:1] + ts1] + [ts0[a:a + 1] + ts1[0:8] for a in range(1, 8)]
                                   + [ts0[8:16] + ts1[0:1]], axis=0)
            probs.append((cand, cand_key))
        for h, (best, pos) in zip((h0, h0 + 1), _top16_rows(probs)):
            (_, ti0), (_, ti1) = tops[h]
            pos = pos.astype(jnp.int32)
            ids = jnp.zeros((kk, tm), F32)
            for r in range(kk):
                pr = pos[r:r + 1, :]
                i0 = jnp.sum(jnp.where(rank == (pr >> 4), ti0, 0.0), axis=0, keepdims=True)
                i1 = jnp.sum(jnp.where(rank == (pr & (kk - 1)), ti1, 0.0), axis=0, keepdims=True)
                ids = jnp.where(rank == r, i0 * float(nk) + i1, ids)
            e = jnp.exp(best - jnp.max(best, axis=0, keepdims=True))
            gate_rows.append(e / jnp.sum(e, axis=0, keepdims=True))
            idx_rows.append(ids)
    idx_o[...] = jnp.concatenate(idx_rows, axis=0).T.astype(jnp.int32)
    gate_o[...] = jnp.concatenate(gate_rows, axis=0).T


def _peer_topk(q, subkeys16):
    t, nq = q.shape
    tm = PEER_TILE
    return pl.pallas_call(
        _peer_topk_kernel,
        grid=(t // tm,),
        in_specs=[pl.BlockSpec((tm, nq), lambda i: (i, 0)),
                  pl.BlockSpec(subkeys16.shape, lambda i: (0, 0, 0))],
        out_specs=[pl.BlockSpec((tm, 128), lambda i: (i, 0)), pl.BlockSpec((tm, 128), lambda i: (i, 0))],
        out_shape=[jax.ShapeDtypeStruct((t, 128), jnp.int32), jax.ShapeDtypeStruct((t, 128), F32)],
        compiler_params=_cparams("parallel"),
        name="peer_topk",
    )(q, subkeys16)


GATHER_TOKENS = 32
N_PICK = PEER_HEADS * PEER_TOPK
N_BURST = 4


def _peer_kernel(idx_cur, idx_nxt, gate_ref, xn_ref, h_ref, mod_ref, uv_hbm, o_ref, buf_a, buf_b, sem_a, sem_b):
    i = pl.program_id(0)
    n = pl.num_programs(0)
    tt = GATHER_TOKENS
    d = h_ref.shape[1]
    gt2 = mod_ref[0][:, 5 * d:6 * d]
    bufs = (buf_a, buf_b)
    sems = (sem_a, sem_b)
    per = N_PICK // N_BURST

    def row_copy(idx_ref, row, half, tok, e):
        return pltpu.make_async_copy(uv_hbm.at[idx_ref[row, e]], bufs[half].at[tok, pl.ds(e * 8, 8), :],
                                     sems[half].at[tok])

    def burst(idx_ref, row0, half, toks, b):
        for k, tok in enumerate(toks):
            for e in range(b * per, (b + 1) * per):
                row_copy(idx_ref, row0 + k, half, tok, e).start(priority=e % 2)

    def pair(half, tok0, idx_next, next_row0):
        toks = (tok0, tok0 + 1)
        rows = [half * tt + t for t in toks]
        for t, row in zip(toks, rows):
            for e in range(N_PICK):
                row_copy(idx_cur, row, half, t, e).wait()
        nch = d // 128
        words = [[bufs[half][t, pl.ds(c, N_PICK, stride=nch), :] for c in range(nch)] for t in toks]
        xts = [xn_ref[row:row + 1, :].astype(BF16) for row in rows]
        gates = [gate_ref[row:row + 1, :] for row in rows]
        hrows = [h_ref[row:row + 1, :] for row in rows]
        refill = lambda b: burst(idx_next, next_row0, 1 - half, toks, b)
        refill(0)
        us = [[lax.bitcast_convert_type(w << 16, F32).astype(BF16) for w in ws] for ws in words]
        refill(1)
        tdot = lambda a, b: lax.dot_general(a, b, (((1,), (1,)), ((), ())), preferred_element_type=F32)
        hds = [functools.reduce(lambda p, q: p + q, [tdot(x[:, c * 128:(c + 1) * 128], u[c]) for c in range(nch)])
               for x, u in zip(xts, us)]
        vs = [[lax.bitcast_convert_type(w & jnp.uint32(0xFFFF0000), F32).astype(BF16) for w in ws] for ws in words]
        refill(2)
        hds = [(0.5 * h * (1.0 + lax.erf(h * (2.0 ** -0.5))) * g).astype(BF16) for h, g in zip(hds, gates)]
        outs = [jnp.concatenate([jnp.dot(h, v[c], preferred_element_type=F32) for c in range(nch)], axis=1)
                for h, v in zip(hds, vs)]
        refill(3)
        for row, hr, out in zip(rows, hrows, outs):
            o_ref[row:row + 1, :] = hr + gt2 * out

    @pl.when(i == 0)
    def _():
        for tok in range(tt):
            for e in range(N_PICK):
                row_copy(idx_cur, tok, 0, tok, e).start(priority=e % 2)

    for tok0 in range(0, tt, 2):
        pair(0, tok0, idx_cur, tt + tok0)
    for tok0 in range(0, tt, 2):
        pair(1, tok0, idx_nxt, tok0)

    @pl.when(i == n - 1)
    def _():
        for tok in range(tt):
            for e in range(N_PICK):
                row_copy(idx_nxt, tok, 0, tok, e).wait()


def _peer(idx, gates, xn2, h1, mod3, uv_words, tokens_per_seq):
    t, d = h1.shape
    tt = GATHER_TOKENS
    blk = 2 * tt
    nsteps = t // blk
    row = lambda i: (i, 0)
    buf = pltpu.VMEM((tt, N_PICK * (d // 128), 128), jnp.uint32)
    return pl.pallas_call(
        _peer_kernel,
        grid=(nsteps,),
        in_specs=[pl.BlockSpec((blk, N_PICK), row, memory_space=pltpu.SMEM),
                  pl.BlockSpec((blk, N_PICK), lambda i: (jnp.minimum(i + 1, nsteps - 1), 0), memory_space=pltpu.SMEM),
                  pl.BlockSpec((blk, N_PICK), row), pl.BlockSpec((blk, d), row), pl.BlockSpec((blk, d), row),
                  pl.BlockSpec((1, 1, mod3.shape[2]), lambda i: ((i * blk) // tokens_per_seq, 0, 0)),
                  pl.BlockSpec(memory_space=pl.ANY)],
        out_specs=pl.BlockSpec((blk, d), row),
        out_shape=jax.ShapeDtypeStruct((t, d), F32),
        scratch_shapes=[buf, buf, pltpu.SemaphoreType.DMA((tt,)), pltpu.SemaphoreType.DMA((tt,))],
        compiler_params=_cparams("arbitrary"),
        name="peer_experts",
    )(idx, idx, gates, xn2, h1, mod3, uv_words)


def _pack_uv(u, v):
    ub = lax.bitcast_convert_type(u.astype(BF16), jnp.uint16).astype(jnp.uint32)
    vb = lax.bitcast_convert_type(v.astype(BF16), jnp.uint16).astype(jnp.uint32)
    return (ub | (vb << 16)).reshape(u.shape[0], u.shape[1] // 128, 128)


def _rope_inv128():
    half = ROPE_DIM // 2
    inv = ROPE_THETA ** (-(jnp.arange(half, dtype=F32) * 2.0) / ROPE_DIM)
    lane = np.arange(128) % HEAD_DIM
    out = jnp.where(lane < ROPE_DIM, inv[lane % half], 0.0)
    return out.reshape(1, 128)


def kernel(x, c, positions, w_ada, b_ada, norm1_g, w_in, q_norm_g, k_norm_g, rwkv_mu, rwkv_w0, rwkv_w2, rwkv_a0, rwkv_a2, rwkv_g2, rwkv_k_k, rwkv_k_a, rwkv_r_k, rwkv_ln_g, rwkv_ln_b, w_proj_moba, w_proj_rwkv, w_out, norm2_g, peer_wq, peer_subkeys, peer_u, peer_v):
    bsz, seq, d = x.shape
    t = bsz * seq
    tps = seq // TOKEN_TILE
    l = 0
    x2 = x.reshape(t, d)
    mod3 = _adaln(c, w_ada[l], b_ada[l]).reshape(bsz, 1, 6 * d)
    w_in_b = w_in[l].astype(BF16)
    g1 = norm1_g[l].reshape(1, d)
    qkv = _inproj(x2, mod3, g1, w_in_b[:, :3 * WIDTH], tps)
    qb, kb, vb, km = _qkprep(qkv, positions.reshape(t, 1), jnp.tile(q_norm_g[l], N_HEADS).reshape(1, WIDTH),
                             jnp.tile(k_norm_g[l], N_HEADS).reshape(1, WIDTH), _rope_inv128())
    o_moba = _moba(qb, kb, vb, km, bsz, seq)
    params = dict(rwkv_mu=rwkv_mu, rwkv_w0=rwkv_w0, rwkv_w2=rwkv_w2, rwkv_a0=rwkv_a0, rwkv_a2=rwkv_a2,
                  rwkv_g2=rwkv_g2, rwkv_k_k=rwkv_k_k, rwkv_k_a=rwkv_k_a, rwkv_r_k=rwkv_r_k.reshape(1, WIDTH),
                  rwkv_ln_g=rwkv_ln_g, rwkv_ln_b=rwkv_ln_b)
    zrw = _inproj(x2, mod3, g1, w_in_b[:, 3 * WIDTH:3 * WIDTH + SHIFT_WIDTH], tps)
    o_rwkv = _rwkv_branch(zrw, params, l, bsz, seq)
    gl = _inproj(x2, mod3, g1, w_in_b[:, 3 * WIDTH + SHIFT_WIDTH:], tps)
    h1, xn2, q = _merge(x2, o_moba.reshape(t, WIDTH), o_rwkv, gl, mod3, norm2_g[l].reshape(1, d),
                        w_proj_moba[l].astype(BF16), w_proj_rwkv[l].astype(BF16), w_out[l].astype(BF16),
                        peer_wq[l].astype(BF16), tps)
    idx, gates = _peer_topk(q, peer_subkeys[l].reshape(2 * PEER_HEADS, PEER_NKEYS, PEER_HALF))
    out = _peer(idx, gates, xn2, h1, mod3, _pack_uv(peer_u[l], peer_v[l]), seq)
    return out.reshape(bsz, seq, d)
```

```python
import functools

import numpy as np
import jax
import jax.numpy as jnp
from jax import lax
from jax.experimental import pallas as pl
from jax.experimental.pallas import tpu as pltpu

F32 = jnp.float32
BF16 = jnp.bfloat16

HEAD_DIM = 64
N_HEADS = 8
WIDTH = N_HEADS * HEAD_DIM
MOBA_BLOCK = 256
MOBA_TOPK = 3
ROPE_THETA = 500000.0
ROPE_DIM = HEAD_DIM // 4
LORA_W = 64
LORA_A = 64
LORA_G = 128
SHIFT_WIDTH = 3 * WIDTH + LORA_W + LORA_A + LORA_G
GN_EPS = 64e-5
RMS_EPS = 1e-6
NEG_INF = -1e30
LOG2_E = 1.4426950408889634
PEER_HEADS = 8
PEER_NKEYS = 128
PEER_HALF = 128
PEER_TOPK = 16

TOKEN_TILE = 512
V7X_VMEM_LIMIT = 48 * 1024 * 1024


def _cparams(*sem, flags=None):
    return pltpu.CompilerParams(dimension_semantics=sem, vmem_limit_bytes=V7X_VMEM_LIMIT, flags=flags)


def _dot(a, b):
    return jnp.dot(a.astype(BF16), b.astype(BF16), preferred_element_type=F32)


def _dot_t(a, b):
    return lax.dot_general(a.astype(BF16), b.astype(BF16), (((1,), (1,)), ((), ())),
                           preferred_element_type=F32)


def _split(a):
    hi = a.astype(BF16)
    lo = (a - hi.astype(F32)).astype(BF16)
    return hi, lo


def _dot3(a, b):
    ah, al = _split(a)
    bh, bl = _split(b)
    return (jnp.dot(ah, bh, preferred_element_type=F32) + jnp.dot(ah, bl, preferred_element_type=F32)
            + jnp.dot(al, bh, preferred_element_type=F32))


def _dot2(a, b_exact):
    ah, al = _split(a)
    return jnp.dot(ah, b_exact, preferred_element_type=F32) + jnp.dot(al, b_exact, preferred_element_type=F32)


def _head_ones(n):
    r = lax.broadcasted_iota(jnp.int32, (n, n), 0) >> 6
    c = lax.broadcasted_iota(jnp.int32, (n, n), 1) >> 6
    return (r == c).astype(BF16)


def _sigmoid(x):
    return 1.0 / (1.0 + jnp.exp(-x))


def _ada_kernel(c_ref, w_ref, b_ref, o_ref):
    c = c_ref[...]
    o_ref[...] = _dot3(c * _sigmoid(c), w_ref[...]) + b_ref[...]


def _adaln(c, w_ada, b_ada):
    bsz, d = c.shape
    n = w_ada.shape[1]
    nb = 1024
    return pl.pallas_call(
        _ada_kernel,
        grid=(n // nb,),
        in_specs=[pl.BlockSpec((bsz, d), lambda j: (0, 0)),
                  pl.BlockSpec((d, nb), lambda j: (0, j)),
                  pl.BlockSpec((1, nb), lambda j: (0, j))],
        out_specs=pl.BlockSpec((bsz, nb), lambda j: (0, j)),
        out_shape=jax.ShapeDtypeStruct((bsz, n), F32),
        compiler_params=_cparams("arbitrary"),
        name="adaln_mod",
    )(c, w_ada, b_ada.reshape(1, n))


def _modnorm(x, g, mod, d, which):
    ms = jnp.mean(x * x, axis=-1, keepdims=True)
    y = x * lax.rsqrt(ms + RMS_EPS) * g
    sh = mod[:, (3 * which) * d:(3 * which + 1) * d]
    sc = mod[:, (3 * which + 1) * d:(3 * which + 2) * d]
    return y * (1.0 + sc) + sh


def _inproj_kernel(x_ref, mod_ref, g_ref, w_ref, o_ref):
    d = x_ref.shape[1]
    xn = _modnorm(x_ref[...], g_ref[...], mod_ref[0], d, 0)
    o_ref[...] = jnp.dot(xn.astype(BF16), w_ref[...], preferred_element_type=F32)


def _inproj(x2, mod3, g, w_bf16, tps):
    t, d = x2.shape
    n = w_bf16.shape[1]
    tm = TOKEN_TILE
    return pl.pallas_call(
        _inproj_kernel,
        grid=(t // tm,),
        in_specs=[pl.BlockSpec((tm, d), lambda i: (i, 0)),
                  pl.BlockSpec((1, 1, mod3.shape[2]), lambda i: (i // tps, 0, 0)),
                  pl.BlockSpec((1, d), lambda i: (0, 0)),
                  pl.BlockSpec((d, n), lambda i: (0, 0))],
        out_specs=pl.BlockSpec((tm, n), lambda i: (i, 0)),
        out_shape=jax.ShapeDtypeStruct((t, n), F32),
        compiler_params=_cparams("parallel"),
        name="in_proj",
    )(x2, mod3, g, w_bf16)


def _qkprep_kernel(qkv_ref, pos_ref, gq_ref, gk_ref, inv_ref, qo_ref, ko_ref, vo_ref, km_ref):
    w = WIDTH
    ones = _head_ones(w)
    ang = pos_ref[...].astype(F32) * inv_ref[...]
    cos = jnp.tile(jnp.cos(ang), (1, w // 128))
    sin = jnp.tile(jnp.sin(ang), (1, w // 128))
    dd = lax.broadcasted_iota(jnp.int32, (1, w), 1) & (HEAD_DIM - 1)
    half = ROPE_DIM // 2
    s_lo = jnp.where(dd < half, -1.0, 0.0) * sin
    s_hi = jnp.where((dd >= half) & (dd < ROPE_DIM), 1.0, 0.0) * sin

    def norm_rope(xh, g):
        ssq = _dot2(xh * xh, ones)
        y = xh * lax.rsqrt(ssq * (1.0 / HEAD_DIM) + RMS_EPS) * g
        return y * cos + pltpu.roll(y, w - half, 1) * s_lo + pltpu.roll(y, half, 1) * s_hi

    q = norm_rope(qkv_ref[:, 0:w], gq_ref[...])
    k = norm_rope(qkv_ref[:, w:2 * w], gk_ref[...])
    qo_ref[...] = (q * (HEAD_DIM ** -0.5 * LOG2_E)).astype(BF16)
    ko_ref[...] = k.astype(BF16)
    vo_ref[...] = qkv_ref[:, 2 * w:3 * w].astype(BF16)
    km_ref[0] = jnp.mean(k, axis=0, keepdims=True)


def _qkprep(qkv, pos2, gq, gk, inv128):
    t = qkv.shape[0]
    tm = MOBA_BLOCK
    w = WIDTH
    row = lambda i: (i, 0)
    fixed = lambda i: (0, 0)
    return pl.pallas_call(
        _qkprep_kernel,
        grid=(t // tm,),
        in_specs=[pl.BlockSpec((tm, 3 * w), row), pl.BlockSpec((tm, 1), row),
                  pl.BlockSpec((1, w), fixed), pl.BlockSpec((1, w), fixed), pl.BlockSpec((1, 128), fixed)],
        out_specs=[pl.BlockSpec((tm, w), row), pl.BlockSpec((tm, w), row), pl.BlockSpec((tm, w), row),
                   pl.BlockSpec((1, 1, w), lambda i: (i, 0, 0))],
        out_shape=[jax.ShapeDtypeStruct((t, w), BF16), jax.ShapeDtypeStruct((t, w), BF16),
                   jax.ShapeDtypeStruct((t, w), BF16), jax.ShapeDtypeStruct((t // tm, 1, w), F32)],
        compiler_params=_cparams("parallel"),
        name="moba_qk_prep",
    )(qkv, pos2, gq, gk, inv128)


def _moba_kernel(q_ref, k_ref, v_ref, km_ref, o_ref):
    blk = MOBA_BLOCK
    i = pl.program_id(2)
    q = q_ref[0]
    km = km_ref[0]
    nbp = km.shape[0]
    lane = lax.broadcasted_iota(jnp.int32, (blk, 128), 1)
    col = lax.broadcasted_iota(jnp.int32, (blk, nbp), 1)
    brow = lax.broadcasted_iota(jnp.int32, (nbp, blk), 0)
    qpos = lax.broadcasted_iota(jnp.int32, (blk, blk), 0)
    kpos = lax.broadcasted_iota(jnp.int32, (blk, blk), 1)
    row0 = pl.multiple_of(i * blk, blk)
    k_own = k_ref[0, pl.ds(row0, blk), :]
    v_own = v_ref[0, pl.ds(row0, blk), :]

    heads = range(2)
    in_head = [(lane >> 6) == hh for hh in heads]
    qs = [jnp.where(m, q, jnp.zeros_like(q)) for m in in_head]
    ones_other = lambda v, hh: jnp.where(in_head[hh], v, jnp.ones_like(v))
    gs = [jnp.where(brow < i, _dot_t(km, qh), -jnp.inf) for qh in qs]
    sels = [jnp.zeros((nbp, blk), jnp.bool_) for _ in heads]
    for _ in range(MOBA_TOPK):
        ms = [jnp.max(g, axis=0, keepdims=True) for g in gs]
        idxs = [jnp.min(jnp.where(g == m, brow, nbp), axis=0, keepdims=True) for g, m in zip(gs, ms)]
        picks = [(brow == ix) & (g > -jnp.inf) for g, ix in zip(gs, idxs)]
        sels = [sl | pk for sl, pk in zip(sels, picks)]
        gs = [jnp.where(pk, -jnp.inf, g) for g, pk in zip(gs, picks)]
    biases = [jnp.where(sl, 0.0, NEG_INF).T for sl in sels]
    s0 = [jnp.where(kpos <= qpos, _dot_t(qh, k_own), NEG_INF) for qh in qs]
    m0 = [jnp.max(a, axis=-1, keepdims=True) for a in s0]
    p0 = [jnp.exp2(a - m) for a, m in zip(s0, m0)]
    acc0 = [_dot(p, ones_other(v_own, hh)) for hh, p in zip(heads, p0)]
    carry0 = [m0[0], acc0[0], m0[1], acc0[1]]

    def body(jj, carry):
        r0 = pl.multiple_of(jj * (2 * blk), 2 * blk)
        kj = k_ref[0, pl.ds(r0, 2 * blk), :]
        vj = v_ref[0, pl.ds(r0, 2 * blk), :]
        hu = [(hh, u) for hh in heads for u in range(2)]
        raw = [_dot_t(qs[hh], kj[u * blk:(u + 1) * blk]) for hh, u in hu]
        bj = [jnp.sum(jnp.where(col == 2 * jj + u, biases[hh], 0.0), axis=-1, keepdims=True) for hh, u in hu]
        ss = [a + b for a, b in zip(raw, bj)]
        mx = [jnp.max(a, axis=-1, keepdims=True) for a in ss]
        m_new = [jnp.maximum(carry[2 * hh], jnp.maximum(mx[2 * hh], mx[2 * hh + 1])) for hh in heads]
        alpha = [jnp.exp2(carry[2 * hh] - m_new[hh]) for hh in heads]
        ps = [jnp.exp2(a - m_new[hh]) for a, (hh, u) in zip(ss, hu)]
        pcat = [jnp.concatenate([ps[2 * hh], ps[2 * hh + 1]], axis=1) for hh in heads]
        lane2 = lax.broadcasted_iota(jnp.int32, (2 * blk, 128), 1)
        pv = [_dot(pc, jnp.where((lane2 >> 6) == hh, vj, jnp.ones_like(vj))) for hh, pc in zip(heads, pcat)]
        out = []
        for hh in heads:
            out += [m_new[hh], alpha[hh] * carry[2 * hh + 1] + pv[hh]]
        return tuple(out)

    fin = lax.fori_loop(0, (i + 1) // 2, body, tuple(carry0))
    outs = [fin[2 * hh + 1] / pltpu.roll(fin[2 * hh + 1], HEAD_DIM, 1) for hh in heads]
    o_ref[0] = jnp.where(lane < HEAD_DIM, outs[0], outs[1]).astype(o_ref.dtype)


def _moba(qb, kb, vb, km, bsz, seq):
    w = WIDTH
    nb = seq // MOBA_BLOCK
    q3 = qb.reshape(bsz, seq, w)
    k3 = kb.reshape(bsz, seq, w)
    v3 = vb.reshape(bsz, seq, w)
    nbp = 128
    assert nb <= nbp
    km3 = jnp.pad(km.reshape(bsz, nb, w), ((0, 0), (0, nbp - nb), (0, 0)))
    return pl.pallas_call(
        _moba_kernel,
        grid=(bsz, w // 128, nb),
        in_specs=[pl.BlockSpec((1, MOBA_BLOCK, 128), lambda b, p, i: (b, i, p)),
                  pl.BlockSpec((1, seq, 128), lambda b, p, i: (b, 0, p)),
                  pl.BlockSpec((1, seq, 128), lambda b, p, i: (b, 0, p)),
                  pl.BlockSpec((1, nbp, 128), lambda b, p, i: (b, 0, p))],
        out_specs=pl.BlockSpec((1, MOBA_BLOCK, 128), lambda b, p, i: (b, i, p)),
        out_shape=jax.ShapeDtypeStruct((bsz, seq, w), BF16),
        compiler_params=_cparams("parallel", "parallel", "arbitrary"),
        name="moba_attention",
    )(q3, k3, v3, km3)


def _rwkvprep_kernel(z_ref, zp_ref, mu_ref, w0_ref, w2_ref, a0_ref, a2_ref, g2_ref, kk_ref, ka_ref, rk_ref,
                     r_o, lw_o, k_o, v_o, kn_o, b_o, bonus_o, g_o, *, tps):
    i = pl.program_id(0)
    w = WIDTH
    z = z_ref[...]
    tm = z.shape[0]
    prev_last = jnp.where((i % tps) == 0, 0.0, zp_ref[7:8, :])
    rows = lax.broadcasted_iota(jnp.int32, (tm, 1), 0)
    prev = jnp.where(rows == 0, prev_last, pltpu.roll(z, 1, 0))
    zs = z + (prev - z) * mu_ref[...]
    zr, zk, zv = zs[:, 0:w], zs[:, w:2 * w], zs[:, 2 * w:3 * w]
    zwa = zs[:, 3 * w:3 * w + 128]
    zg = zs[:, 3 * w + 128:3 * w + 256]
    nw = -(w0_ref[...] + _dot3(jnp.tanh(zwa), w2_ref[...]))
    softplus = jnp.maximum(nw, 0.0) + jnp.log(1.0 + jnp.exp(-jnp.abs(nw)))
    lw_o[...] = -jnp.exp(-softplus - 0.5)
    a = _sigmoid(a0_ref[...] + _dot3(zwa, a2_ref[...]))
    g_o[...] = _dot3(_sigmoid(zg), g2_ref[...])
    ones = _head_ones(w)
    kn = zk * kk_ref[...]
    kn = kn / jnp.maximum(jnp.sqrt(_dot2(kn * kn, ones)), 1e-12)
    k = zk * (1.0 + (a - 1.0) * ka_ref[...])
    r_o[...] = zr
    k_o[...] = k
    v_o[...] = zv
    kn_o[...] = kn
    b_o[...] = kn * a
    bonus_o[...] = _dot2(zr * k * rk_ref[...], ones) * zv


def _rwkvprep(zrw, tps, mu, w0, w2aug, a0, a2aug, g2, k_k, k_a, r_k):
    t = zrw.shape[0]
    tm = TOKEN_TILE
    w = WIDTH
    row = lambda i: (i, 0)
    fixed = lambda i: (0, 0)
    vec = pl.BlockSpec((1, w), fixed)
    outs = [jax.ShapeDtypeStruct((t, w), F32)] * 8
    return pl.pallas_call(
        functools.partial(_rwkvprep_kernel, tps=tps),
        grid=(t // tm,),
        in_specs=[pl.BlockSpec((tm, SHIFT_WIDTH), row),
                  pl.BlockSpec((8, SHIFT_WIDTH), lambda i: (jnp.maximum(i * (tm // 8) - 1, 0), 0)),
                  pl.BlockSpec((1, SHIFT_WIDTH), fixed),
                  vec, pl.BlockSpec((128, w), fixed), vec, pl.BlockSpec((128, w), fixed),
                  pl.BlockSpec((LORA_G, w), fixed), vec, vec, vec],
        out_specs=[pl.BlockSpec((tm, w), row)] * 8,
        out_shape=outs,
        compiler_params=_cparams("parallel"),
        name="rwkv_prep",
    )(zrw, zrw, mu, w0, w2aug, a0, a2aug, g2, k_k, k_a, r_k)


RWKV_CHUNK = 32


SCAN_GROUP = 4
SCAN_CHUNKS_PER_STEP = 8


def _scan_chunks(probs):
    cl, gw = probs[0][0].shape
    hc = SCAN_GROUP * cl
    bdot = lambda a, b: jnp.dot(a, b, preferred_element_type=F32)
    tn = lambda a, b: lax.dot_general(a, b, (((0,), (0,)), ((), ())), preferred_element_type=F32)
    tri = (lax.broadcasted_iota(jnp.int32, (cl, cl), 0) >= lax.broadcasted_iota(jnp.int32, (cl, cl), 1)).astype(BF16)
    hmask = (lax.broadcasted_iota(jnp.int32, (hc, gw), 0) // cl) == (lax.broadcasted_iota(jnp.int32, (hc, gw), 1) >> 6)
    stack = lambda a: jnp.where(hmask, jnp.tile(a, (SCAN_GROUP, 1)), 0.0).astype(BF16)
    unstack = lambda a: functools.reduce(lambda p, q: p + q, [a[h * cl:(h + 1) * cl] for h in range(SCAN_GROUP)])
    ri = lax.broadcasted_iota(jnp.int32, (hc, hc), 0)
    ci = lax.broadcasted_iota(jnp.int32, (hc, hc), 1)
    strict = ri > ci
    incl = ri >= ci
    eye = jnp.where(ri == ci, 1.0, 0.0)
    di = lax.broadcasted_iota(jnp.int32, (gw, gw), 0) == lax.broadcasted_iota(jnp.int32, (gw, gw), 1)

    cs2 = [bdot(tri, jnp.concatenate(_split(p[1]), axis=1)) for p in probs]
    ops = []
    for (r, lw, k, v, kn, b), c2 in zip(probs, cs2):
        cs = c2[:, :gw] + c2[:, gw:]
        cs_end = cs[cl - 1:cl, :]
        inv = jnp.exp(-cs)
        dec_end = jnp.exp(cs_end - cs)
        ops.append(dict(khs=stack(kn * jnp.exp(cs - lw)),
                        rhs=stack(r * jnp.exp(cs)),
                        bts=stack(b * inv), kts=stack(k * inv), vs=stack(v),
                        btes=stack(b * dec_end), ktes=stack(k * dec_end), gend=jnp.exp(cs_end)))
    grams = [lax.dot_general(jnp.concatenate([o['khs'], o['rhs']], axis=0),
                             jnp.concatenate([o['bts'], o['kts']], axis=0),
                             (((1,), (1,)), ((), ())), preferred_element_type=F32) for o in ops]
    for o, g in zip(ops, grams):
        lb = jnp.where(strict, g[:hc, :hc], 0.0)
        o['lkak'] = jnp.concatenate([jnp.where(strict, g[:hc, hc:], 0.0),
                                     jnp.where(incl, g[hc:, hc:], 0.0)], axis=0).astype(BF16)
        o['ab'] = jnp.where(incl, g[hc:, :hc], 0.0).astype(BF16)
        o['tinv'] = eye - lb
        o['pw'] = (-lb).astype(BF16)
    sq = [bdot(o['pw'], o['pw']) for o in ops]
    for o, q in zip(ops, sq):
        o['pw'] = q.astype(BF16)
    levels = int(np.log2(cl)) - 1
    for lvl in range(levels):
        both = [bdot(jnp.concatenate([o['pw'], o['tinv'].astype(BF16)], axis=0), o['pw']) for o in ops]
        for o, bo in zip(ops, both):
            o['tinv'] = o['tinv'] + bo[hc:]
            o['pw'] = bo[:hc].astype(BF16)
    lkak = [bdot(o['lkak'], o['vs']) for o in ops]
    wu = [bdot(o['tinv'].astype(BF16), jnp.concatenate([o['khs'], lv[:hc].astype(BF16)], axis=1))
          for o, lv in zip(ops, lkak)]
    wub = [jnp.concatenate([w[:, :gw], -w[:, gw:]], axis=1).astype(BF16) for w in wu]
    abw = [bdot(o['ab'], w) for o, w in zip(ops, wub)]
    gmat = [tn(o['btes'], w[:, :gw]) for o, w in zip(ops, wub)]
    hv = [tn(jnp.concatenate([o['btes'], o['ktes']], axis=0), jnp.concatenate([w[:, gw:], o['vs']], axis=0))
          for o, w in zip(ops, wub)]
    out = []
    for o, aw, lv, gm, h_v in zip(ops, abw, lkak, gmat, hv):
        qt = unstack(o['rhs'].astype(F32) - aw[:, :gw])
        yv = unstack(aw[:, gw:] + lv[hc:])
        out.append((qt, yv, jnp.where(di, o['gend'], 0.0) - gm, h_v))
    return out


def _scan_kernel(r_ref, lw_ref, k_ref, v_ref, kn_ref, b_ref, bonus_ref, g_ref, lng_ref, lnb_ref, o_ref, h_ref):
    gw = SCAN_GROUP * HEAD_DIM
    cl = RWKV_CHUNK
    ngrp = N_HEADS // SCAN_GROUP

    @pl.when(pl.program_id(1) == 0)
    def _():
        h_ref[...] = jnp.zeros_like(h_ref)

    where = [(grp, c) for grp in range(ngrp) for c in range(SCAN_CHUNKS_PER_STEP)]
    sl = lambda grp, c: (slice(c * cl, (c + 1) * cl), slice(grp * gw, (grp + 1) * gw))
    parts = _scan_chunks([tuple(ref[sl(grp, c)] for ref in (r_ref, lw_ref, k_ref, v_ref, kn_ref, b_ref))
                          for grp, c in where])
    ones = _head_ones(gw)
    bdot = lambda a, b: jnp.dot(a, b, preferred_element_type=F32)
    for grp in range(ngrp):
        h = h_ref[grp]
        for c in range(SCAN_CHUNKS_PER_STEP):
            rows, ln = sl(grp, c)
            qt, yv, abar, h_v = parts[where.index((grp, c))]
            hh, hl = _split(h)
            y = bdot(qt.astype(BF16), hh) + yv
            ah, al = _split(abar)
            ahl = bdot(jnp.concatenate([ah, al], axis=0), hh)
            h = ahl[:gw] + ahl[gw:] + bdot(ah, hl) + h_v
            mean = _dot2(y, ones) * (1.0 / HEAD_DIM)
            yc = y - mean
            var = _dot2(yc * yc, ones) * (1.0 / HEAD_DIM)
            yn = yc * lax.rsqrt(var + GN_EPS) * lng_ref[:, ln] + lnb_ref[:, ln] + bonus_ref[rows, ln]
            o_ref[rows, ln] = (yn * g_ref[rows, ln]).astype(o_ref.dtype)
        h_ref[grp] = h


def _rwkv_scan(r, lw, k, v, kn, b, bonus, g, ln_g, ln_b, bsz, seq):
    t, w = r.shape
    cl = RWKV_CHUNK * SCAN_CHUNKS_PER_STEP
    cps = seq // cl
    row = pl.BlockSpec((cl, w), lambda bi, c: (bi * cps + c, 0))
    vec = pl.BlockSpec((1, w), lambda bi, c: (0, 0))
    return pl.pallas_call(
        _scan_kernel,
        grid=(bsz, cps),
        in_specs=[row] * 8 + [vec, vec],
        out_specs=row,
        out_shape=jax.ShapeDtypeStruct((t, w), BF16),
        scratch_shapes=[pltpu.VMEM((N_HEADS // SCAN_GROUP, SCAN_GROUP * HEAD_DIM, SCAN_GROUP * HEAD_DIM), F32)],
        compiler_params=_cparams("parallel", "arbitrary"),
        name="rwkv_scan",
    )(r, lw, k, v, kn, b, bonus, g, ln_g, ln_b)


def _rwkv_branch(zrw, p, l, bsz, seq):
    w = WIDTH
    vec = lambda a: a.reshape(1, -1)
    zpad = lambda a, top: jnp.concatenate([a, jnp.zeros_like(a)] if top else [jnp.zeros_like(a), a], axis=0)
    outs = _rwkvprep(zrw, seq // TOKEN_TILE, vec(p['rwkv_mu'][l]), vec(p['rwkv_w0'][l]),
                     zpad(p['rwkv_w2'][l], True), vec(p['rwkv_a0'][l]), zpad(p['rwkv_a2'][l], False),
                     p['rwkv_g2'][l], vec(p['rwkv_k_k'][l]), vec(p['rwkv_k_a'][l]), vec(p['rwkv_r_k'][l]))
    return _rwkv_scan(*outs, vec(p['rwkv_ln_g'][l]), vec(p['rwkv_ln_b'][l]), bsz, seq)


def _merge_kernel(x_ref, om_ref, or_ref, gl_ref, mod_ref, g2_ref, wpm_ref, wpr_ref, wo_ref, wq_ref,
                  h_o, xn_o, q_o):
    d = x_ref.shape[1]
    mod = mod_ref[0]
    pm = jnp.dot(om_ref[...], wpm_ref[...], preferred_element_type=F32)
    pr = jnp.dot(or_ref[...], wpr_ref[...], preferred_element_type=F32)
    gl = gl_ref[...]
    mixed = _sigmoid(gl[:, :d]) * pm + _sigmoid(gl[:, d:]) * pr
    h = x_ref[...] + mod[:, 2 * d:3 * d] * _dot(mixed, wo_ref[...])
    h_o[...] = h
    xn2 = _modnorm(h, g2_ref[...], mod, d, 1)
    xn_o[...] = xn2
    q_o[...] = _dot(xn2, wq_ref[...])


def _merge(x2, o_moba, o_rwkv, gl, mod3, g2, wpm, wpr, wo, wq, tps):
    t, d = x2.shape
    tm = TOKEN_TILE
    w = WIDTH
    nq = wq.shape[1]
    row = lambda i: (i, 0)
    fixed = lambda i: (0, 0)
    return pl.pallas_call(
        _merge_kernel,
        grid=(t // tm,),
        in_specs=[pl.BlockSpec((tm, d), row), pl.BlockSpec((tm, w), row), pl.BlockSpec((tm, w), row),
                  pl.BlockSpec((tm, 2 * d), row),
                  pl.BlockSpec((1, 1, mod3.shape[2]), lambda i: (i // tps, 0, 0)),
                  pl.BlockSpec((1, d), fixed), pl.BlockSpec((w, d), fixed), pl.BlockSpec((w, d), fixed),
                  pl.BlockSpec((d, d), fixed), pl.BlockSpec((d, nq), fixed)],
        out_specs=[pl.BlockSpec((tm, d), row), pl.BlockSpec((tm, d), row), pl.BlockSpec((tm, nq), row)],
        out_shape=[jax.ShapeDtypeStruct((t, d), F32), jax.ShapeDtypeStruct((t, d), F32),
                   jax.ShapeDtypeStruct((t, nq), F32)],
        compiler_params=_cparams("parallel"),
        name="merge_proj",
    )(x2, o_moba, o_rwkv, gl, mod3, g2, wpm, wpr, wo, wq)


PEER_TILE = 128


def _top16_rows(problems):
    tm = problems[0][0].shape[1]
    rank = lax.broadcasted_iota(jnp.int32, (PEER_TOPK, tm), 0)
    keys = [k for _, k in problems]
    big = jnp.int32(1 << 30)

    def rnd(r, carry):
        out = []
        hit = rank == r
        for (s, vals, idxs), key in zip(carry, keys):
            m = jnp.max(s, axis=0, keepdims=True)
            ix = jnp.min(jnp.where(s == m, key, big), axis=0, keepdims=True)
            out.append((jnp.where(key == ix, -jnp.inf, s), jnp.where(hit, m, vals),
                        jnp.where(hit, ix.astype(F32), idxs)))
        return tuple(out)

    z = jnp.zeros((PEER_TOPK, tm), F32)
    fin = lax.fori_loop(0, PEER_TOPK, rnd, tuple((s, z, z) for s, _ in problems))
    return [(vals, idxs) for _, vals, idxs in fin]


def _peer_topk_kernel(q_ref, sk_ref, idx_o, gate_o):
    tm = q_ref.shape[0]
    kk = PEER_TOPK
    nk = PEER_NKEYS
    rank = lax.broadcasted_iota(jnp.int32, (kk, tm), 0)
    pos_key = lax.broadcasted_iota(jnp.int32, (nk, tm), 0)
    row8 = lax.broadcasted_iota(jnp.int32, (8, tm), 0)
    tdot = lambda a, b: lax.dot_general(a, b, (((1,), (1,)), ((), ())), preferred_element_type=F32)
    cand_key = jnp.concatenate([rank] + [row8 + a * kk for a in range(1, 8)] + [(row8 + 8) * kk], axis=0)

    tops = []
    for h0 in range(0, PEER_HEADS, 2):
        probs = []
        for hp in range(2 * h0, 2 * h0 + 4):
            qh, ql = _split(q_ref[:, hp * PEER_HALF:(hp + 1) * PEER_HALF])
            kh, kl = _split(sk_ref[hp])
            probs.append((tdot(kh, qh) + tdot(kh, ql) + tdot(kl, qh), pos_key))
        res = _top16_rows(probs)
        tops += [res[0:2], res[2:4]]

    idx_rows, gate_rows = [], []
    for h0 in range(0, PEER_HEADS, 2):
        probs = []
        for h in (h0, h0 + 1):
            (ts0, _), (ts1, _) = tops[h]
            cand = jnp.concatenate([ts0[0:1] + ts1] + [ts0[a:a + 1] + ts1[0:8] for a in range(1, 8)]
                                   + [ts0[8:16] + ts1[0:1]], axis=0)
            probs.append((cand, cand_key))
        for h, (best, pos) in zip((h0, h0 + 1), _top16_rows(probs)):
            (_, ti0), (_, ti1) = tops[h]
            pos = pos.astype(jnp.int32)
            ids = jnp.zeros((kk, tm), F32)
            for r in range(kk):
                pr = pos[r:r + 1, :]
                i0 = jnp.sum(jnp.where(rank == (pr >> 4), ti0, 0.0), axis=0, keepdims=True)
                i1 = jnp.sum(jnp.where(rank == (pr & (kk - 1)), ti1, 0.0), axis=0, keepdims=True)
                ids = jnp.where(rank == r, i0 * float(nk) + i1, ids)
            e = jnp.exp(best - jnp.max(best, axis=0, keepdims=True))
            gate_rows.append(e / jnp.sum(e, axis=0, keepdims=True))
            idx_rows.append(ids)
    idx_o[...] = jnp.concatenate(idx_rows, axis=0).T.astype(jnp.int32)
    gate_o[...] = jnp.concatenate(gate_rows, axis=0).T


def _peer_topk(q, subkeys16):
    t, nq = q.shape
    tm = PEER_TILE
    return pl.pallas_call(
        _peer_topk_kernel,
        grid=(t // tm,),
        in_specs=[pl.BlockSpec((tm, nq), lambda i: (i, 0)),
                  pl.BlockSpec(subkeys16.shape, lambda i: (0, 0, 0))],
        out_specs=[pl.BlockSpec((tm, 128), lambda i: (i, 0)), pl.BlockSpec((tm, 128), lambda i: (i, 0))],
        out_shape=[jax.ShapeDtypeStruct((t, 128), jnp.int32), jax.ShapeDtypeStruct((t, 128), F32)],
        compiler_params=_cparams("parallel"),
        name="peer_topk",
    )(q, subkeys16)


GATHER_TOKENS = 32
N_PICK = PEER_HEADS * PEER_TOPK
N_BURST = 4


def _peer_kernel(idx_cur, idx_nxt, gate_ref, xn_ref, h_ref, mod_ref, uv_hbm, o_ref, buf_a, buf_b, sem_a, sem_b):
    i = pl.program_id(0)
    n = pl.num_programs(0)
    tt = GATHER_TOKENS
    d = h_ref.shape[1]
    gt2 = mod_ref[0][:, 5 * d:6 * d]
    bufs = (buf_a, buf_b)
    sems = (sem_a, sem_b)
    per = N_PICK // N_BURST

    def row_copy(idx_ref, row, half, tok, e):
        return pltpu.make_async_copy(uv_hbm.at[idx_ref[row, e]], bufs[half].at[tok, pl.ds(e * 8, 8), :],
                                     sems[half].at[tok])

    def burst(idx_ref, row0, half, toks, b):
        for k, tok in enumerate(toks):
            for e in range(b * per, (b + 1) * per):
                row_copy(idx_ref, row0 + k, half, tok, e).start(priority=e % 2)

    def pair(half, tok0, idx_next, next_row0):
        toks = (tok0, tok0 + 1)
        rows = [half * tt + t for t in toks]
        for t, row in zip(toks, rows):
            for e in range(N_PICK):
                row_copy(idx_cur, row, half, t, e).wait()
        nch = d // 128
        words = [[bufs[half][t, pl.ds(c, N_PICK, stride=nch), :] for c in range(nch)] for t in toks]
        xts = [xn_ref[row:row + 1, :].astype(BF16) for row in rows]
        gates = [gate_ref[row:row + 1, :] for row in rows]
        hrows = [h_ref[row:row + 1, :] for row in rows]
        refill = lambda b: burst(idx_next, next_row0, 1 - half, toks, b)
        refill(0)
        us = [[lax.bitcast_convert_type(w << 16, F32).astype(BF16) for w in ws] for ws in words]
        refill(1)
        tdot = lambda a, b: lax.dot_general(a, b, (((1,), (1,)), ((), ())), preferred_element_type=F32)
        hds = [functools.reduce(lambda p, q: p + q, [tdot(x[:, c * 128:(c + 1) * 128], u[c]) for c in range(nch)])
               for x, u in zip(xts, us)]
        vs = [[lax.bitcast_convert_type(w & jnp.uint32(0xFFFF0000), F32).astype(BF16) for w in ws] for ws in words]
        refill(2)
        hds = [(0.5 * h * (1.0 + lax.erf(h * (2.0 ** -0.5))) * g).astype(BF16) for h, g in zip(hds, gates)]
        outs = [jnp.concatenate([jnp.dot(h, v[c], preferred_element_type=F32) for c in range(nch)], axis=1)
                for h, v in zip(hds, vs)]
        refill(3)
        for row, hr, out in zip(rows, hrows, outs):
            o_ref[row:row + 1, :] = hr + gt2 * out

    @pl.when(i == 0)
    def _():
        for tok in range(tt):
            for e in range(N_PICK):
                row_copy(idx_cur, tok, 0, tok, e).start(priority=e % 2)

    for tok0 in range(0, tt, 2):
        pair(0, tok0, idx_cur, tt + tok0)
    for tok0 in range(0, tt, 2):
        pair(1, tok0, idx_nxt, tok0)

    @pl.when(i == n - 1)
    def _():
        for tok in range(tt):
            for e in range(N_PICK):
                row_copy(idx_nxt, tok, 0, tok, e).wait()


def _peer(idx, gates, xn2, h1, mod3, uv_words, tokens_per_seq):
    t, d = h1.shape
    tt = GATHER_TOKENS
    blk = 2 * tt
    nsteps = t // blk
    row = lambda i: (i, 0)
    buf = pltpu.VMEM((tt, N_PICK * (d // 128), 128), jnp.uint32)
    return pl.pallas_call(
        _peer_kernel,
        grid=(nsteps,),
        in_specs=[pl.BlockSpec((blk, N_PICK), row, memory_space=pltpu.SMEM),
                  pl.BlockSpec((blk, N_PICK), lambda i: (jnp.minimum(i + 1, nsteps - 1), 0), memory_space=pltpu.SMEM),
                  pl.BlockSpec((blk, N_PICK), row), pl.BlockSpec((blk, d), row), pl.BlockSpec((blk, d), row),
                  pl.BlockSpec((1, 1, mod3.shape[2]), lambda i: ((i * blk) // tokens_per_seq, 0, 0)),
                  pl.BlockSpec(memory_space=pl.ANY)],
        out_specs=pl.BlockSpec((blk, d), row),
        out_shape=jax.ShapeDtypeStruct((t, d), F32),
        scratch_shapes=[buf, buf, pltpu.SemaphoreType.DMA((tt,)), pltpu.SemaphoreType.DMA((tt,))],
        compiler_params=_cparams("arbitrary"),
        name="peer_experts",
    )(idx, idx, gates, xn2, h1, mod3, uv_words)


def _pack_uv(u, v):
    ub = lax.bitcast_convert_type(u.astype(BF16), jnp.uint16).astype(jnp.uint32)
    vb = lax.bitcast_convert_type(v.astype(BF16), jnp.uint16).astype(jnp.uint32)
    return (ub | (vb << 16)).reshape(u.shape[0], u.shape[1] // 128, 128)


def _rope_inv128():
    half = ROPE_DIM // 2
    inv = ROPE_THETA ** (-(jnp.arange(half, dtype=F32) * 2.0) / ROPE_DIM)
    lane = np.arange(128) % HEAD_DIM
    out = jnp.where(lane < ROPE_DIM, inv[lane % half], 0.0)
    return out.reshape(1, 128)


def kernel(x, c, positions, w_ada, b_ada, norm1_g, w_in, q_norm_g, k_norm_g, rwkv_mu, rwkv_w0, rwkv_w2, rwkv_a0, rwkv_a2, rwkv_g2, rwkv_k_k, rwkv_k_a, rwkv_r_k, rwkv_ln_g, rwkv_ln_b, w_proj_moba, w_proj_rwkv, w_out, norm2_g, peer_wq, peer_subkeys, peer_u, peer_v):
    bsz, seq, d = x.shape
    t = bsz * seq
    tps = seq // TOKEN_TILE
    l = 0
    x2 = x.reshape(t, d)
    mod3 = _adaln(c, w_ada[l], b_ada[l]).reshape(bsz, 1, 6 * d)
    w_in_b = w_in[l].astype(BF16)
    g1 = norm1_g[l].reshape(1, d)
    qkv = _inproj(x2, mod3, g1, w_in_b[:, :3 * WIDTH], tps)
    qb, kb, vb, km = _qkprep(qkv, positions.reshape(t, 1), jnp.tile(q_norm_g[l], N_HEADS).reshape(1, WIDTH),
                             jnp.tile(k_norm_g[l], N_HEADS).reshape(1, WIDTH), _rope_inv128())
    o_moba = _moba(qb, kb, vb, km, bsz, seq)
    params = dict(rwkv_mu=rwkv_mu, rwkv_w0=rwkv_w0, rwkv_w2=rwkv_w2, rwkv_a0=rwkv_a0, rwkv_a2=rwkv_a2,
                  rwkv_g2=rwkv_g2, rwkv_k_k=rwkv_k_k, rwkv_k_a=rwkv_k_a, rwkv_r_k=rwkv_r_k.reshape(1, WIDTH),
                  rwkv_ln_g=rwkv_ln_g, rwkv_ln_b=rwkv_ln_b)
    zrw = _inproj(x2, mod3, g1, w_in_b[:, 3 * WIDTH:3 * WIDTH + SHIFT_WIDTH], tps)
    o_rwkv = _rwkv_branch(zrw, params, l, bsz, seq)
    gl = _inproj(x2, mod3, g1, w_in_b[:, 3 * WIDTH + SHIFT_WIDTH:], tps)
    h1, xn2, q = _merge(x2, o_moba.reshape(t, WIDTH), o_rwkv, gl, mod3, norm2_g[l].reshape(1, d),
                        w_proj_moba[l].astype(BF16), w_proj_rwkv[l].astype(BF16), w_out[l].astype(BF16),
                        peer_wq[l].astype(BF16), tps)
    idx, gates = _peer_topk(q, peer_subkeys[l].reshape(2 * PEER_HEADS, PEER_NKEYS, PEER_HALF))
    out = _peer(idx, gates, xn2, h1, mod3, _pack_uv(peer_u[l], peer_v[l]), seq)
    return out.reshape(bsz, seq, d)
```
